```python
import jax, jax.numpy as jnp
from jax import lax
import numpy as np

D_MODEL = 1024
BATCH = 1
SEQ = 16384
DEPTH = 2

CHUNK = 64
Q_BLOCK = 128
N_MEM = 256
EPS = 1e-6
FOX_HEADS = 8
FOX_HEAD_DIM = 64
FOX_WIDTH = FOX_HEADS * FOX_HEAD_DIM
GDN_HEADS = 4
GDN_HEAD_DIM = 128
GDN_WIDTH = GDN_HEADS * GDN_HEAD_DIM
CONV_K = 4
MEM_HEADS = 4
MEM_HEAD_DIM = 128
MEM_WIDTH = MEM_HEADS * MEM_HEAD_DIM
N_BRANCH = 3
BRANCH_WIDTH = 512

IN_SIZES = (FOX_WIDTH, FOX_WIDTH, FOX_WIDTH, FOX_HEADS, FOX_WIDTH,
            GDN_WIDTH, GDN_WIDTH, GDN_WIDTH, GDN_HEADS, GDN_HEADS, GDN_WIDTH,
            MEM_WIDTH, MEM_WIDTH,
            N_BRANCH * D_MODEL)
N_IN = sum(IN_SIZES)

kernel_name = 'hybrid_fox_gdn_memory_gated_merge'


def _split_cols(z, sizes):
    idx = []
    acc = 0
    for s in sizes[:-1]:
        acc += s
        idx.append(acc)
    return jnp.split(z, idx, axis=-1)


def rms_norm(x, g):
    xf = x.astype(jnp.float32)
    y = xf * lax.rsqrt(jnp.mean(xf * xf, axis=-1, keepdims=True) + EPS)
    return (y * g.astype(jnp.float32)).astype(x.dtype)


def l2_normalize(x):
    return x * lax.rsqrt(jnp.sum(x * x, axis=-1, keepdims=True) + EPS)


def forgetting_attention(q, k, v, f_logit):
    B, S, H, d = q.shape
    log_f = jax.nn.log_sigmoid(f_logit.astype(jnp.float32))
    F = jnp.cumsum(log_f, axis=1).transpose(0, 2, 1)
    nb = S // Q_BLOCK
    pos = jnp.arange(S)
    q_blocks = q.reshape(B, nb, Q_BLOCK, H, d).swapaxes(0, 1)
    F_blocks = F.reshape(B, H, nb, Q_BLOCK).transpose(2, 0, 1, 3)
    p_blocks = pos.reshape(nb, Q_BLOCK)
    scale = d ** -0.5

    def block(args):
        q_blk, F_blk, p_blk = args
        s = jnp.einsum('bqhd,bkhd->bhqk', q_blk, k,
                       preferred_element_type=jnp.float32) * scale
        s = s + F_blk[..., :, None] - F[:, :, None, :]
        s = jnp.where(pos[None, None, None, :] <= p_blk[None, None, :, None], s, -jnp.inf)
        p = jax.nn.softmax(s, axis=-1)
        return jnp.einsum('bhqk,bkhd->bqhd', p.astype(v.dtype), v)

    o = lax.map(block, (q_blocks, F_blocks, p_blocks))
    return o.swapaxes(0, 1).reshape(B, S, H * d)


def causal_dwconv(x, w):
    K = w.shape[0]
    return lax.conv_general_dilated(
        x, w[:, None, :].astype(x.dtype), window_strides=(1,),
        padding=[(K - 1, 0)], dimension_numbers=('NWC', 'WIO', 'NWC'),
        feature_group_count=x.shape[-1])


def gated_delta_rule(q, k, v, g, beta):
    B, S, H, dk = q.shape
    dv = v.shape[-1]
    N = S // CHUNK
    C = CHUNK

    def chunks(t):
        t = t.reshape((B, N, C, H) + t.shape[3:])
        return jnp.moveaxis(t, 3, 1)

    qc, kc, vc = chunks(q), chunks(k), chunks(v)
    gc, bc = chunks(g), chunks(beta)
    G = jnp.cumsum(gc, axis=-1)
    tril_incl = jnp.tril(jnp.ones((C, C), dtype=bool))
    tril_strict = jnp.tril(jnp.ones((C, C), dtype=bool), -1)
    gamma = jnp.exp(jnp.where(tril_incl, G[..., :, None] - G[..., None, :], -jnp.inf))
    kb = kc * bc[..., None]
    A = jnp.where(tril_strict, jnp.einsum('bhncd,bhnsd->bhncs', kb, kc) * gamma, 0.0)
    eye = jnp.eye(C, dtype=A.dtype)
    rhs = jnp.concatenate([vc * bc[..., None], kb * jnp.exp(G)[..., None]], axis=-1)
    sol = lax.linalg.triangular_solve(A + eye, rhs, left_side=True, lower=True,
                                      unit_diagonal=True)
    u, w = sol[..., :dv], sol[..., dv:]
    a_qk = jnp.einsum('bhncd,bhnsd->bhncs', qc, kc) * gamma
    q_dec = qc * jnp.exp(G)[..., None]
    G_last = G[..., -1]
    k_dec = kc * jnp.exp(G_last[..., None] - G)[..., None]

    def step(state, inp):
        dq, kd, uu, ww, aqk, gl = inp
        v_new = uu - jnp.einsum('bhcd,bhde->bhce', ww, state)
        o = (jnp.einsum('bhcd,bhde->bhce', dq, state)
             + jnp.einsum('bhcs,bhse->bhce', aqk, v_new))
        state = state * jnp.exp(gl)[..., None, None] + jnp.einsum('bhcd,bhce->bhde', kd, v_new)
        return state, o

    xs = tuple(jnp.moveaxis(t, 2, 0) for t in (q_dec, k_dec, u, w, a_qk, G_last))
    state0 = jnp.zeros((B, H, dk, dv), jnp.float32)
    _, o = lax.scan(step, state0, xs)
    o = jnp.moveaxis(o, 0, 2)
    return jnp.moveaxis(o, 1, 3).reshape(B, S, H, dv)


def hybrid_layer(x, mem, norm_g, w_in, b_fg, b_merge, conv_w, a_log, dt_bias,
                 gdn_norm_g, mem_norm_g, w_mem_kv, w_branch, w_out):
    B, S, D = x.shape
    dt = x.dtype
    h = rms_norm(x, norm_g)
    z = h @ w_in
    (aq, ak, av, af, az, bq, bk, bv, ba, bb, bz, mq, mz, gates) = _split_cols(z, IN_SIZES)

    o_a = forgetting_attention(aq.reshape(B, S, FOX_HEADS, FOX_HEAD_DIM),
                               ak.reshape(B, S, FOX_HEADS, FOX_HEAD_DIM),
                               av.reshape(B, S, FOX_HEADS, FOX_HEAD_DIM),
                               af + b_fg)
    y_a = (o_a * jax.nn.silu(az)).astype(dt)

    qkv = jax.nn.silu(causal_dwconv(jnp.concatenate([bq, bk, bv], axis=-1), conv_w))
    gq, gk, gv = jnp.split(qkv.astype(jnp.float32), 3, axis=-1)
    gq = l2_normalize(gq.reshape(B, S, GDN_HEADS, GDN_HEAD_DIM)) * (GDN_HEAD_DIM ** -0.5)
    gk = l2_normalize(gk.reshape(B, S, GDN_HEADS, GDN_HEAD_DIM))
    gv = gv.reshape(B, S, GDN_HEADS, GDN_HEAD_DIM)
    g_log = -jnp.exp(a_log.astype(jnp.float32)) * jax.nn.softplus(
        ba.astype(jnp.float32) + dt_bias.astype(jnp.float32))
    beta = jax.nn.sigmoid(bb.astype(jnp.float32))
    o_b = gated_delta_rule(gq, gk, gv, g_log, beta)
    y_b = (rms_norm(o_b, gdn_norm_g).reshape(B, S, GDN_WIDTH) * jax.nn.silu(bz)).astype(dt)

    mem_n = rms_norm(mem, mem_norm_g)
    mk, mv = jnp.split(mem_n @ w_mem_kv, 2, axis=-1)
    M = mem.shape[1]
    mk = mk.reshape(B, M, MEM_HEADS, MEM_HEAD_DIM)
    mv = mv.reshape(B, M, MEM_HEADS, MEM_HEAD_DIM)
    s_m = jnp.einsum('bshd,bmhd->bhsm', mq.reshape(B, S, MEM_HEADS, MEM_HEAD_DIM), mk,
                     preferred_element_type=jnp.float32) * (MEM_HEAD_DIM ** -0.5)
    p_m = jax.nn.softmax(s_m, axis=-1)
    o_m = jnp.einsum('bhsm,bmhd->bshd', p_m.astype(mv.dtype), mv).reshape(B, S, MEM_WIDTH)
    y_m = (o_m * jax.nn.silu(mz)).astype(dt)

    ys = jnp.stack([y_a, y_b, y_m], axis=2)
    proj = jnp.einsum('bsnc,ncd->bsnd', ys, w_branch)
    gate = jax.nn.sigmoid(gates + b_merge).reshape(B, S, N_BRANCH, D)
    merged = jnp.sum(gate * proj, axis=2)
    return x + merged @ w_out


def setup_inputs(seed: int = 0) -> dict:
    key = jax.random.key(seed)
    ks = jax.random.split(key, 16)
    f32 = jnp.float32
    x = jax.random.normal(ks[0], (BATCH, SEQ, D_MODEL), f32)
    mem = jax.random.normal(ks[1], (BATCH, N_MEM, D_MODEL), f32)
    norm_g = 1.0 + 0.02 * jax.random.normal(ks[2], (DEPTH, D_MODEL), f32)
    w_in = jax.random.normal(ks[3], (DEPTH, D_MODEL, N_IN), f32) * (D_MODEL ** -0.5)
    b_fg = 1.0 + 3.0 * jax.random.uniform(ks[4], (DEPTH, FOX_HEADS), f32)
    b_merge = 0.02 * jax.random.normal(ks[5], (DEPTH, N_BRANCH * D_MODEL), f32)
    conv_w = jax.random.normal(ks[6], (DEPTH, CONV_K, 3 * GDN_WIDTH), f32) * (CONV_K ** -0.5)
    a_log = jnp.log(jax.random.uniform(ks[7], (DEPTH, GDN_HEADS), f32, 1.0, 16.0))
    dt0 = jnp.exp(jax.random.uniform(ks[8], (DEPTH, GDN_HEADS), f32,
                                     float(np.log(1e-3)), float(np.log(1e-1))))
    dt_bias = dt0 + jnp.log(-jnp.expm1(-dt0))
    gdn_norm_g = 1.0 + 0.02 * jax.random.normal(ks[9], (DEPTH, GDN_HEAD_DIM), f32)
    mem_norm_g = 1.0 + 0.02 * jax.random.normal(ks[10], (DEPTH, D_MODEL), f32)
    w_mem_kv = jax.random.normal(ks[11], (DEPTH, D_MODEL, 2 * MEM_WIDTH), f32) * (D_MODEL ** -0.5)
    w_branch = jax.random.normal(ks[12], (DEPTH, N_BRANCH, BRANCH_WIDTH, D_MODEL), f32) * (BRANCH_WIDTH ** -0.5)
    w_out = jax.random.normal(ks[13], (DEPTH, D_MODEL, D_MODEL), f32) * (0.5 * D_MODEL ** -0.5)
    final_norm_g = 1.0 + 0.02 * jax.random.normal(ks[14], (D_MODEL,), f32)
    return {'x': x, 'mem': mem, 'norm_g': norm_g, 'w_in': w_in, 'b_fg': b_fg,
            'b_merge': b_merge, 'conv_w': conv_w, 'a_log': a_log, 'dt_bias': dt_bias,
            'gdn_norm_g': gdn_norm_g, 'mem_norm_g': mem_norm_g, 'w_mem_kv': w_mem_kv,
            'w_branch': w_branch, 'w_out': w_out, 'final_norm_g': final_norm_g}


def reference(x, mem, norm_g, w_in, b_fg, b_merge, conv_w, a_log, dt_bias,
              gdn_norm_g, mem_norm_g, w_mem_kv, w_branch, w_out, final_norm_g):
    for l in range(DEPTH):
        x = hybrid_layer(x, mem, norm_g[l], w_in[l], b_fg[l], b_merge[l], conv_w[l],
                         a_log[l], dt_bias[l], gdn_norm_g[l], mem_norm_g[l],
                         w_mem_kv[l], w_branch[l], w_out[l])
    return rms_norm(x, final_norm_g)
```

```python
import functools

import jax
import jax.numpy as jnp
import numpy as np
from jax import lax
from jax.experimental import pallas as pl
from jax.experimental.pallas import tpu as pltpu

F32 = jnp.float32
BF16 = jnp.bfloat16
HIGHEST = lax.Precision.HIGHEST

D_MODEL = 1024
EPS = 1e-6
FOX_HEADS = 8
FOX_HEAD_DIM = 64
GDN_HEADS = 4
GDN_HEAD_DIM = 128
MEM_HEADS = 4
MEM_HEAD_DIM = 128
WIDTH = 512
N_BRANCH = 3
CHUNK = 64
CONV_K = 4
LANES = 128
GDN_TILE = 256
CHUNKS_PER_TILE = GDN_TILE // CHUNK

COL_BQKV = 0
COL_AQ = 1536
COL_AZ = 3584
COL_GATES = 5120
COL_SMALL = 8192
N_COLS = 8320
SMALL_BA = 8
SMALL_BB = 12

VMEM_LIMIT = 56 * 1024 * 1024


def _cparams(sem):
    return pltpu.CompilerParams(dimension_semantics=sem, vmem_limit_bytes=VMEM_LIMIT)


def _bf(x):
    return x.astype(BF16)


def _dot(a, b):
    return jnp.dot(_bf(a), _bf(b), preferred_element_type=F32)


def _dot_nt(a, b):
    return lax.dot_general(_bf(a), _bf(b), (((1,), (1,)), ((), ())), preferred_element_type=F32)


def _sigmoid(x):
    return 1.0 / (1.0 + jnp.exp(-x))


def _silu(x):
    return x * _sigmoid(x)


def _norm_matmul_kernel(x_ref, g_ref, w_ref, o_ref, h_ref):
    @pl.when(pl.program_id(1) == 0)
    def _():
        x = x_ref[...]
        y = x * lax.rsqrt(jnp.mean(x * x, axis=-1, keepdims=True) + EPS)
        h_ref[...] = _bf(y * g_ref[...])

    o_ref[...] = jnp.dot(h_ref[...], w_ref[...], preferred_element_type=F32).astype(o_ref.dtype)


def _norm_matmul(x, g, w, tm, tn, out_dtype):
    s, d = x.shape
    n = w.shape[1]
    return pl.pallas_call(
        _norm_matmul_kernel,
        grid=(s // tm, n // tn),
        in_specs=[pl.BlockSpec((tm, d), lambda i, j: (i, 0)),
                  pl.BlockSpec((1, d), lambda i, j: (0, 0)),
                  pl.BlockSpec((d, tn), lambda i, j: (0, j))],
        out_specs=pl.BlockSpec((tm, tn), lambda i, j: (i, j)),
        out_shape=jax.ShapeDtypeStruct((s, n), out_dtype),
        scratch_shapes=[pltpu.VMEM((tm, d), BF16)],
        compiler_params=_cparams(("parallel", "arbitrary")),
        name="norm_matmul",
    )(x, g, w)


def _split3(x):
    hi = _bf(x).astype(F32)
    r1 = x - hi
    mid = _bf(r1).astype(F32)
    lo = _bf(r1 - mid).astype(F32)
    return hi, mid, lo


def _attn_prep_kernel(q_ref, k_ref, v_ref, small_ref, bfg_ref, tril_ref,
                      qa_ref, ka_ref, va_ref, carry_ref):
    tm = q_ref.shape[0]

    @pl.when(pl.program_id(0) == 0)
    def _():
        carry_ref[...] = jnp.zeros_like(carry_ref)

    af = small_ref[...] + bfg_ref[...]
    logf = jnp.minimum(af, 0.0) - jnp.log1p(jnp.exp(-jnp.abs(af)))
    cum = carry_ref[...] + jnp.dot(tril_ref[...], logf, precision=HIGHEST,
                                   preferred_element_type=F32)
    carry_ref[...] = cum[tm - 1:tm, :]

    lane = lax.broadcasted_iota(jnp.int32, (tm, LANES), 1)
    for h in range(FOX_HEADS):
        p, odd = divmod(h, 2)
        cols = slice(p * LANES, (p + 1) * LANES)
        fb = jnp.broadcast_to(cum[:, h:h + 1], (tm, LANES))
        hi, mid, lo = _split3(fb)
        aux = lane - (0 if odd else FOX_HEAD_DIM)
        is_feat = (lane >= FOX_HEAD_DIM) if odd else (lane < FOX_HEAD_DIM)
        one = jnp.ones((tm, LANES), F32)
        zero = jnp.zeros((tm, LANES), F32)
        q_aux = jnp.where(aux == 0, hi, jnp.where(aux == 1, mid, jnp.where(aux == 2, lo,
                          jnp.where(aux < 6, one, zero))))
        k_aux = jnp.where(aux < 3, one, jnp.where(aux == 3, -hi, jnp.where(aux == 4, -mid,
                          jnp.where(aux == 5, -lo, zero))))
        qa_ref[h] = _bf(jnp.where(is_feat, q_ref[:, cols], q_aux))
        ka_ref[h] = _bf(jnp.where(is_feat, k_ref[:, cols], k_aux))
        va_ref[h] = _bf(jnp.where(is_feat, v_ref[:, cols], one))


def _attn_prep(z, bfg_row, tm):
    s = z.shape[0]
    tril = jnp.asarray(np.tril(np.ones((tm, tm), np.float32)))
    cb = COL_AQ // WIDTH
    out_sds = jax.ShapeDtypeStruct((FOX_HEADS, s, LANES), BF16)
    out_spec = pl.BlockSpec((FOX_HEADS, tm, LANES), lambda i: (0, i, 0))
    return pl.pallas_call(
        _attn_prep_kernel,
        grid=(s // tm,),
        in_specs=[pl.BlockSpec((tm, WIDTH), lambda i: (i, cb)),
                  pl.BlockSpec((tm, WIDTH), lambda i: (i, cb + 1)),
                  pl.BlockSpec((tm, WIDTH), lambda i: (i, cb + 2)),
                  pl.BlockSpec((tm, LANES), lambda i: (i, COL_SMALL // LANES)),
                  pl.BlockSpec((1, LANES), lambda i: (0, 0)),
                  pl.BlockSpec((tm, tm), lambda i: (0, 0))],
        out_specs=[out_spec, out_spec, out_spec],
        out_shape=[out_sds, out_sds, out_sds],
        scratch_shapes=[pltpu.VMEM((1, LANES), F32)],
        compiler_params=_cparams(("arbitrary",)),
        name="attn_prep",
    )(z, z, z, z, bfg_row, tril)


def _attn_kernel(q_ref, k_ref, v_ref, o_ref, *, tq, tk):
    i = pl.program_id(1)
    lane = lax.broadcasted_iota(jnp.int32, (tq, LANES), 1)
    n_tile = tk // LANES

    def step(q, hh, j, carry, masked):
        m, acc = carry
        start = pl.multiple_of(j * tk, tk)
        k = k_ref[hh, pl.ds(start, tk), :]
        v = v_ref[hh, pl.ds(start, tk), :]
        s = lax.dot_general(q, k, (((1,), (1,)), ((), ())), preferred_element_type=F32)
        if masked:
            row = lax.broadcasted_iota(jnp.int32, (tq, tk), 0)
            col = lax.broadcasted_iota(jnp.int32, (tq, tk), 1)
            s = jnp.where(col <= row, s, -jnp.inf)
        m_new = jnp.maximum(m, jnp.max(s, axis=1, keepdims=True))
        p = jnp.exp(s - jnp.tile(m_new, (1, n_tile)))
        alpha = jnp.exp(m - m_new)
        acc = alpha * acc + jnp.dot(_bf(p), v, preferred_element_type=F32)
        return m_new, acc

    accs = []
    for hh in range(2):
        q = q_ref[hh]
        carry = (jnp.full((tq, LANES), -jnp.inf, F32), jnp.zeros((tq, LANES), F32))
        carry = lax.fori_loop(0, i, lambda j, c: step(q, hh, j, c, False), carry)
        _, acc = step(q, hh, i, carry, True)
        accs.append(acc)
    o0 = accs[0] / pltpu.roll(accs[0], FOX_HEAD_DIM, 1)
    o1 = accs[1] / pltpu.roll(accs[1], FOX_HEAD_DIM, 1)
    o_ref[...] = jnp.where(lane < FOX_HEAD_DIM, o0, o1)


def _attention(qa, ka, va, tq):
    s = qa.shape[1]
    tk = tq
    kv_spec = pl.BlockSpec((2, s, LANES), lambda p, i: (p, 0, 0), pipeline_mode=pl.Buffered(1))
    return pl.pallas_call(
        functools.partial(_attn_kernel, tq=tq, tk=tk),
        grid=(FOX_HEADS // 2, s // tq),
        in_specs=[pl.BlockSpec((2, tq, LANES), lambda p, i: (p, i, 0)), kv_spec, kv_spec],
        out_specs=pl.BlockSpec((tq, LANES), lambda p, i: (i, p)),
        out_shape=jax.ShapeDtypeStruct((s, WIDTH), F32),
        compiler_params=_cparams(("parallel", "arbitrary")),
        name="fox_attention",
    )(qa, ka, va)


def _gdn_consts():
    t, c = GDN_TILE, CHUNK
    r = np.arange(t)
    same = (r[:, None] // c) == (r[None, :] // c)
    kl = (same & (r[None, :] <= r[:, None])).astype(np.float32)
    ko = same.astype(np.float32)
    mbd = (same & (r[:, None] > r[None, :])).astype(np.float32)
    sel_a = np.zeros((LANES, WIDTH), np.float32)
    sel_b = np.zeros((LANES, WIDTH), np.float32)
    for h in range(GDN_HEADS):
        sel_a[SMALL_BA + h, h * GDN_HEAD_DIM:(h + 1) * GDN_HEAD_DIM] = 1.0
        sel_b[SMALL_BB + h, h * GDN_HEAD_DIM:(h + 1) * GDN_HEAD_DIM] = 1.0
    return [jnp.asarray(a) for a in (kl, ko, mbd, sel_a, sel_b)]


def _gdn_prep_kernel(x_ref, halo_ref, small_ref, cw_ref, aneg_ref, dtb_ref,
                     kl_ref, ko_ref, mbd_ref, sela_ref, selb_ref,
                     qd_ref, kd_ref, u_ref, w_ref, aqk_ref, egl_ref, xx_ref):
    t = GDN_TILE
    hd = GDN_HEAD_DIM
    i = pl.program_id(0)

    halo = halo_ref[...]
    xx_ref[0:8, :] = jnp.where(i == 0, jnp.zeros_like(halo), halo)
    xx_ref[8:8 + t, :] = x_ref[...]
    conv = cw_ref[CONV_K - 1:CONV_K, :] * xx_ref[8:8 + t, :]
    for j in range(CONV_K - 1):
        off = 8 - (CONV_K - 1) + j
        conv = conv + cw_ref[j:j + 1, :] * xx_ref[off:off + t, :]
    qkv = _silu(conv)

    small = small_ref[...]
    ba = jnp.dot(small, sela_ref[...], precision=HIGHEST, preferred_element_type=F32)
    bb = jnp.dot(small, selb_ref[...], precision=HIGHEST, preferred_element_type=F32)
    sp_in = ba + dtb_ref[...]
    softplus = jnp.maximum(sp_in, 0.0) + jnp.log1p(jnp.exp(-jnp.abs(sp_in)))
    g = aneg_ref[...] * softplus
    beta = _sigmoid(bb)
    gcum = jnp.dot(kl_ref[...], g, precision=HIGHEST, preferred_element_type=F32)
    glast = jnp.dot(ko_ref[...], g, precision=HIGHEST, preferred_element_type=F32)
    eg = jnp.exp(gcum)
    egl = jnp.exp(glast)
    egd = jnp.exp(glast - gcum)
    row8 = lax.broadcasted_iota(jnp.int32, (8, WIDTH), 0)
    egl_rows = jnp.zeros((8, WIDTH), F32)
    for c in range(CHUNKS_PER_TILE):
        egl_rows = jnp.where(row8 == c, egl[c * CHUNK:c * CHUNK + 8, :], egl_rows)
    egl_ref[...] = egl_rows

    row = lax.broadcasted_iota(jnp.int32, (t, t), 0)
    col = lax.broadcasted_iota(jnp.int32, (t, t), 1)
    same = (row // CHUNK) == (col // CHUNK)
    incl = jnp.logical_and(same, col <= row)
    strict = jnp.logical_and(same, col < row)
    eye = (row == col).astype(F32)

    for h in range(GDN_HEADS):
        cs = slice(h * hd, (h + 1) * hd)
        qh = qkv[:, h * hd:(h + 1) * hd]
        kh = qkv[:, WIDTH + h * hd:WIDTH + (h + 1) * hd]
        vh = qkv[:, 2 * WIDTH + h * hd:2 * WIDTH + (h + 1) * hd]
        qh = qh * lax.rsqrt(jnp.sum(qh * qh, axis=-1, keepdims=True) + EPS) * (hd ** -0.5)
        kh = kh * lax.rsqrt(jnp.sum(kh * kh, axis=-1, keepdims=True) + EPS)
        gh, bh = g[:, cs], beta[:, cs]
        xg = jnp.concatenate([gh, gh], axis=1) * mbd_ref[...]
        dmat = jnp.dot(kl_ref[...], xg, precision=HIGHEST, preferred_element_type=F32)
        gamma = jnp.exp(jnp.where(incl, dmat, -jnp.inf))
        kb = kh * bh
        a = jnp.where(strict, _dot_nt(kb, kh) * gamma, 0.0)
        bp = -a
        inv = eye + bp
        for _ in range(5):
            bp = _dot(bp, bp)
            inv = inv + _dot(inv, bp)
        rhs = jnp.concatenate([vh * bh, kb * eg[:, cs]], axis=1)
        sol = _dot(inv, rhs)
        u_ref[:, cs] = sol[:, :hd]
        w_ref[:, cs] = _bf(sol[:, hd:])
        aqk_ref[h] = _bf(_dot_nt(qh, kh) * gamma)
        qd_ref[:, cs] = _bf(qh * eg[:, cs])
        kd_ref[:, cs] = _bf(kh * egd[:, cs])


def _gdn_prep(z, conv_w, aneg_row, dtb_row):
    s = z.shape[0]
    t = GDN_TILE
    nt = s // t
    consts = _gdn_consts()
    full = lambda a: pl.BlockSpec(a.shape, lambda i: (0,) * a.ndim)
    row_spec = pl.BlockSpec((t, WIDTH), lambda i: (i, 0))
    return pl.pallas_call(
        _gdn_prep_kernel,
        grid=(nt,),
        in_specs=[pl.BlockSpec((t, 3 * WIDTH), lambda i: (i, COL_BQKV // (3 * WIDTH))),
                  pl.BlockSpec((8, 3 * WIDTH), lambda i: (jnp.maximum(i * (t // 8) - 1, 0), 0)),
                  pl.BlockSpec((t, LANES), lambda i: (i, COL_SMALL // LANES)),
                  full(conv_w), full(aneg_row), full(dtb_row)] + [full(a) for a in consts],
        out_specs=[row_spec, row_spec, row_spec, row_spec,
                   pl.BlockSpec((GDN_HEADS, t, t), lambda i: (0, i, 0)),
                   pl.BlockSpec((8, WIDTH), lambda i: (i, 0))],
        out_shape=[jax.ShapeDtypeStruct((s, WIDTH), BF16),
                   jax.ShapeDtypeStruct((s, WIDTH), BF16),
                   jax.ShapeDtypeStruct((s, WIDTH), F32),
                   jax.ShapeDtypeStruct((s, WIDTH), BF16),
                   jax.ShapeDtypeStruct((GDN_HEADS, s, t), BF16),
                   jax.ShapeDtypeStruct((nt * 8, WIDTH), F32)],
        scratch_shapes=[pltpu.VMEM((t + 8, 3 * WIDTH), F32)],
        compiler_params=_cparams(("parallel",)),
        name="gdn_prep",
    )(z, z, z, conv_w, aneg_row, dtb_row, *consts)


def _gdn_scan_kernel(qd_ref, kd_ref, u_ref, w_ref, aqk_ref, egl_ref, o_ref, state_ref, vn_ref):
    hd = GDN_HEAD_DIM

    @pl.when(pl.program_id(0) == 0)
    def _():
        state_ref[...] = jnp.zeros_like(state_ref)
        vn_ref[...] = jnp.zeros_like(vn_ref)

    for c in range(CHUNKS_PER_TILE):
        rs = slice(c * CHUNK, (c + 1) * CHUNK)
        for h in range(GDN_HEADS):
            cs = slice(h * hd, (h + 1) * hd)
            st = state_ref[h]
            lhs = jnp.concatenate([w_ref[rs, cs], qd_ref[rs, cs]], axis=0)
            r = jnp.dot(lhs, _bf(st), preferred_element_type=F32)
            v_new = u_ref[rs, cs] - r[:CHUNK]
            v_new_b = _bf(v_new)
            vn_ref[h, rs, :] = v_new_b
            o_ref[rs, cs] = r[CHUNK:] + jnp.dot(aqk_ref[h, rs, :], vn_ref[h],
                                                preferred_element_type=F32)
            upd = lax.dot_general(kd_ref[rs, cs], v_new_b, (((0,), (0,)), ((), ())),
                                  preferred_element_type=F32)
            state_ref[h] = st * egl_ref[c:c + 1, cs] + upd


def _gdn_scan(qd, kd, u, w, aqk, egl):
    s = qd.shape[0]
    t = GDN_TILE
    row_spec = pl.BlockSpec((t, WIDTH), lambda i: (i, 0))
    return pl.pallas_call(
        _gdn_scan_kernel,
        grid=(s // t,),
        in_specs=[row_spec, row_spec, row_spec, row_spec,
                  pl.BlockSpec((GDN_HEADS, t, t), lambda i: (0, i, 0)),
                  pl.BlockSpec((8, WIDTH), lambda i: (i, 0))],
        out_specs=row_spec,
        out_shape=jax.ShapeDtypeStruct((s, WIDTH), F32),
        scratch_shapes=[pltpu.VMEM((GDN_HEADS, GDN_HEAD_DIM, GDN_HEAD_DIM), F32),
                        pltpu.VMEM((GDN_HEADS, t, GDN_HEAD_DIM), BF16)],
        compiler_params=_cparams(("arbitrary",)),
        name="gdn_scan",
    )(qd, kd, u, w, aqk, egl)


def _merge_kernel(x_ref, oa_ref, ob_ref, mq_ref, az_ref, bz_ref, mz_ref, g0_ref, g1_ref, g2_ref,
                  mk_ref, mv_ref, gng_ref, bm_ref, wb_ref, wo_ref, fg_ref, o_ref, *, final_norm):
    hd = GDN_HEAD_DIM
    y_a = oa_ref[...] * _silu(az_ref[...])

    ob = ob_ref[...]
    normed = []
    for h in range(GDN_HEADS):
        oh = ob[:, h * hd:(h + 1) * hd]
        normed.append(oh * lax.rsqrt(jnp.mean(oh * oh, axis=-1, keepdims=True) + EPS))
    y_b = jnp.concatenate(normed, axis=1) * gng_ref[...] * _silu(bz_ref[...])

    om = []
    for h in range(MEM_HEADS):
        cs = slice(h * MEM_HEAD_DIM, (h + 1) * MEM_HEAD_DIM)
        sc = _dot_nt(mq_ref[:, cs], mk_ref[:, cs]) * (MEM_HEAD_DIM ** -0.5)
        p = jnp.exp(sc - jnp.max(sc, axis=-1, keepdims=True))
        om.append(jnp.dot(_bf(p), mv_ref[:, cs], preferred_element_type=F32)
                  / jnp.sum(p, axis=-1, keepdims=True))
    y_m = jnp.concatenate(om, axis=1) * _silu(mz_ref[...])

    merged = None
    for n, (y, g_ref) in enumerate(((y_a, g0_ref), (y_b, g1_ref), (y_m, g2_ref))):
        proj = jnp.dot(_bf(y), wb_ref[n], preferred_element_type=F32)
        term = _sigmoid(g_ref[...] + bm_ref[n:n + 1, :]) * proj
        merged = term if merged is None else merged + term
    out = x_ref[...] + jnp.dot(_bf(merged), wo_ref[...], preferred_element_type=F32)
    if final_norm:
        out = out * lax.rsqrt(jnp.mean(out * out, axis=-1, keepdims=True) + EPS) * fg_ref[...]
    o_ref[...] = out


def _merge(x, o_a, o_b, z, memkv, gng_row, bm, wb, wo, fg_row, tm, final_norm):
    s = x.shape[0]
    zc = lambda col, width: pl.BlockSpec((tm, width), lambda i: (i, col // width))
    full = lambda a: pl.BlockSpec(a.shape, lambda i: (0,) * a.ndim)
    n_mem = memkv.shape[0]
    return pl.pallas_call(
        functools.partial(_merge_kernel, final_norm=final_norm),
        grid=(s // tm,),
        in_specs=[pl.BlockSpec((tm, D_MODEL), lambda i: (i, 0)),
                  pl.BlockSpec((tm, WIDTH), lambda i: (i, 0)),
                  pl.BlockSpec((tm, WIDTH), lambda i: (i, 0)),
                  zc(COL_AQ + 3 * WIDTH, WIDTH),
                  zc(COL_AZ, WIDTH), zc(COL_AZ + WIDTH, WIDTH), zc(COL_AZ + 2 * WIDTH, WIDTH),
                  zc(COL_GATES, D_MODEL), zc(COL_GATES + D_MODEL, D_MODEL),
                  zc(COL_GATES + 2 * D_MODEL, D_MODEL),
                  pl.BlockSpec((n_mem, WIDTH), lambda i: (0, 0)),
                  pl.BlockSpec((n_mem, WIDTH), lambda i: (0, 1)),
                  full(gng_row), full(bm), full(wb), full(wo), full(fg_row)],
        out_specs=pl.BlockSpec((tm, D_MODEL), lambda i: (i, 0)),
        out_shape=jax.ShapeDtypeStruct((s, D_MODEL), F32),
        compiler_params=_cparams(("parallel",)),
        name="merge",
    )(x, o_a, o_b, z, z, z, z, z, z, z, memkv, memkv, gng_row, bm, wb, wo, fg_row)


def _permute_w_in(w_in):
    w = WIDTH
    sizes = (w, w, w, FOX_HEADS, w, w, w, w, GDN_HEADS, GDN_HEADS, w, w, w, N_BRANCH * D_MODEL)
    offs = np.concatenate([[0], np.cumsum(sizes)])
    part = lambda k: w_in[:, offs[k]:offs[k + 1]]
    aq, ak, av, af, az, bq, bk, bv, ba, bb, bz, mq, mz, gates = (part(k) for k in range(14))
    pad = jnp.zeros((w_in.shape[0], LANES - FOX_HEADS - 2 * GDN_HEADS), w_in.dtype)
    cols = [bq, bk, bv, aq * (FOX_HEAD_DIM ** -0.5), ak, av, mq, az, bz, mz, gates, af, ba, bb, pad]
    return _bf(jnp.concatenate(cols, axis=1))


def _layer(x, mem, norm_g, w_in, b_fg, b_merge, conv_w, a_log, dt_bias, gdn_norm_g, mem_norm_g,
           w_mem_kv, w_branch, w_out, final_norm_g, final_norm):
    s = x.shape[0]
    row = lambda v: v.reshape(1, -1).astype(F32)
    z = _norm_matmul(x, row(norm_g), _permute_w_in(w_in), tm=min(s, 1024), tn=N_COLS // 5,
                     out_dtype=F32)
    memkv = _norm_matmul(mem, row(mem_norm_g), _bf(w_mem_kv), tm=mem.shape[0], tn=2 * WIDTH,
                         out_dtype=BF16)

    bfg_row = jnp.zeros((1, LANES), F32).at[0, :FOX_HEADS].set(b_fg)
    qa, ka, va = _attn_prep(z, bfg_row, tm=256)
    o_a = _attention(qa, ka, va, tq=min(s, 512))

    aneg_row = row(jnp.repeat(-jnp.exp(a_log.astype(F32)), GDN_HEAD_DIM))
    dtb_row = row(jnp.repeat(dt_bias, GDN_HEAD_DIM))
    qd, kd, u, w, aqk, egl = _gdn_prep(z, conv_w, aneg_row, dtb_row)
    o_b = _gdn_scan(qd, kd, u, w, aqk, egl)

    return _merge(x, o_a, o_b, z, memkv, row(jnp.tile(gdn_norm_g, GDN_HEADS)),
                  b_merge.reshape(N_BRANCH, D_MODEL), _bf(w_branch), _bf(w_out),
                  row(final_norm_g), tm=256, final_norm=final_norm)


def kernel(x, mem, norm_g, w_in, b_fg, b_merge, conv_w, a_log, dt_bias, gdn_norm_g, mem_norm_g,
           w_mem_kv, w_branch, w_out, final_norm_g):
    assert x.shape[0] == 1 and mem.shape[0] == 1
    depth = w_in.shape[0]
    h = x[0]
    for l in range(depth):
        h = _layer(h, mem[0], norm_g[l], w_in[l], b_fg[l], b_merge[l], conv_w[l], a_log[l],
                   dt_bias[l], gdn_norm_g[l], mem_norm_g[l], w_mem_kv[l], w_branch[l], w_out[l],
                   final_norm_g, final_norm=(l == depth - 1))
    return h[None]
```

```python
import functools

import jax
import jax.numpy as jnp
import numpy as np
from jax import lax
from jax.experimental import pallas as pl
from jax.experimental.pallas import tpu as pltpu

F32 = jnp.float32
BF16 = jnp.bfloat16
HIGHEST = lax.Precision.HIGHEST

D_MODEL = 1024
EPS = 1e-6
FOX_HEADS = 8
FOX_HEAD_DIM = 64
GDN_HEADS = 4
GDN_HEAD_DIM = 128
MEM_HEADS = 4
MEM_HEAD_DIM = 128
WIDTH = 512
N_BRANCH = 3
CHUNK = 64
CONV_K = 4
LANES = 128
GDN_TILE = 256
CHUNKS_PER_TILE = GDN_TILE // CHUNK

COL_BQKV = 0
COL_AQ = 1536
COL_AZ = 3584
COL_GATES = 5120
COL_SMALL = 8192
N_COLS = 8320
SMALL_BA = 8
SMALL_BB = 12

ATTN_PREP_TILE = 256
SKIP_LOG_MARGIN = 100.0
NORM_SLACK = 1.001

VMEM_LIMIT = 56 * 1024 * 1024


def _cparams(sem):
    return pltpu.CompilerParams(dimension_semantics=sem, vmem_limit_bytes=VMEM_LIMIT)


def _bf(x):
    return x.astype(BF16)


def _dot(a, b):
    return jnp.dot(_bf(a), _bf(b), preferred_element_type=F32)


def _dot_nt(a, b):
    return lax.dot_general(_bf(a), _bf(b), (((1,), (1,)), ((), ())), preferred_element_type=F32)


def _sigmoid(x):
    return 1.0 / (1.0 + jnp.exp(-x))


def _silu(x):
    return x * _sigmoid(x)


def _norm_matmul_kernel(x_ref, g_ref, w_ref, o_ref, h_ref):
    @pl.when(pl.program_id(1) == 0)
    def _():
        x = x_ref[...]
        y = x * lax.rsqrt(jnp.mean(x * x, axis=-1, keepdims=True) + EPS)
        h_ref[...] = _bf(y * g_ref[...])

    o_ref[...] = jnp.dot(h_ref[...], w_ref[...], preferred_element_type=F32).astype(o_ref.dtype)


def _norm_matmul(x, g, w, tm, tn, out_dtype):
    s, d = x.shape
    n = w.shape[1]
    return pl.pallas_call(
        _norm_matmul_kernel,
        grid=(s // tm, n // tn),
        in_specs=[pl.BlockSpec((tm, d), lambda i, j: (i, 0)),
                  pl.BlockSpec((1, d), lambda i, j: (0, 0)),
                  pl.BlockSpec((d, tn), lambda i, j: (0, j))],
        out_specs=pl.BlockSpec((tm, tn), lambda i, j: (i, j)),
        out_shape=jax.ShapeDtypeStruct((s, n), out_dtype),
        scratch_shapes=[pltpu.VMEM((tm, d), BF16)],
        compiler_params=_cparams(("parallel", "arbitrary")),
        name="norm_matmul",
    )(x, g, w)


def _split3(x):
    hi = _bf(x).astype(F32)
    r1 = x - hi
    mid = _bf(r1).astype(F32)
    lo = _bf(r1 - mid).astype(F32)
    return hi, mid, lo


def _attn_prep_kernel(q_ref, k_ref, v_ref, small_ref, bfg_ref, tril_ref,
                      qa_ref, ka_ref, va_ref, stats_ref, carry_ref):
    tm = q_ref.shape[0]
    lane1 = lax.broadcasted_iota(jnp.int32, (1, LANES), 1)
    row8 = lax.broadcasted_iota(jnp.int32, (FOX_HEADS, LANES), 0)
    stats = jnp.zeros((FOX_HEADS, LANES), F32)

    @pl.when(pl.program_id(0) == 0)
    def _():
        carry_ref[...] = jnp.zeros_like(carry_ref)

    af = small_ref[...] + bfg_ref[...]
    logf = jnp.minimum(af, 0.0) - jnp.log1p(jnp.exp(-jnp.abs(af)))
    cum = carry_ref[...] + jnp.dot(tril_ref[...], logf, precision=HIGHEST,
                                   preferred_element_type=F32)
    carry_ref[...] = cum[tm - 1:tm, :]

    lane = lax.broadcasted_iota(jnp.int32, (tm, LANES), 1)
    for h in range(FOX_HEADS):
        p, odd = divmod(h, 2)
        cols = slice(p * LANES, (p + 1) * LANES)
        fb = jnp.broadcast_to(cum[:, h:h + 1], (tm, LANES))
        hi, mid, lo = _split3(fb)
        aux = lane - (0 if odd else FOX_HEAD_DIM)
        is_feat = (lane >= FOX_HEAD_DIM) if odd else (lane < FOX_HEAD_DIM)
        one = jnp.ones((tm, LANES), F32)
        zero = jnp.zeros((tm, LANES), F32)
        q_aux = jnp.where(aux == 0, hi, jnp.where(aux == 1, mid, jnp.where(aux == 2, lo,
                          jnp.where(aux < 6, one, zero))))
        k_aux = jnp.where(aux < 3, one, jnp.where(aux == 3, -hi, jnp.where(aux == 4, -mid,
                          jnp.where(aux == 5, -lo, zero))))
        qa_ref[h] = _bf(jnp.where(is_feat, q_ref[:, cols], q_aux))
        ka_ref[h] = _bf(jnp.where(is_feat, k_ref[:, cols], k_aux))
        va_ref[h] = _bf(jnp.where(is_feat, v_ref[:, cols], one))
        qf = _bf(jnp.where(is_feat, q_ref[:, cols], zero)).astype(F32)
        kf = _bf(jnp.where(is_feat, k_ref[:, cols], zero)).astype(F32)
        qn2 = jnp.max(jnp.sum(qf * qf, axis=1, keepdims=True), axis=0, keepdims=True)
        kn2 = jnp.max(jnp.sum(kf * kf, axis=1, keepdims=True), axis=0, keepdims=True)
        srow = jnp.where(lane1 == 0, cum[0:1, h:h + 1],
                         jnp.where(lane1 == 1, cum[tm - 1:tm, h:h + 1],
                                   jnp.where(lane1 == 2, qn2, jnp.where(lane1 == 3, kn2, 0.0))))
        stats = jnp.where(row8 == h, jnp.broadcast_to(srow, (FOX_HEADS, LANES)), stats)
    stats_ref[0] = stats


def _attn_prep(z, bfg_row, tm):
    s = z.shape[0]
    tril = jnp.asarray(np.tril(np.ones((tm, tm), np.float32)))
    cb = COL_AQ // WIDTH
    out_sds = jax.ShapeDtypeStruct((FOX_HEADS, s, LANES), BF16)
    out_spec = pl.BlockSpec((FOX_HEADS, tm, LANES), lambda i: (0, i, 0))
    return pl.pallas_call(
        _attn_prep_kernel,
        grid=(s // tm,),
        in_specs=[pl.BlockSpec((tm, WIDTH), lambda i: (i, cb)),
                  pl.BlockSpec((tm, WIDTH), lambda i: (i, cb + 1)),
                  pl.BlockSpec((tm, WIDTH), lambda i: (i, cb + 2)),
                  pl.BlockSpec((tm, LANES), lambda i: (i, COL_SMALL // LANES)),
                  pl.BlockSpec((1, LANES), lambda i: (0, 0)),
                  pl.BlockSpec((tm, tm), lambda i: (0, 0))],
        out_specs=[out_spec, out_spec, out_spec,
                   pl.BlockSpec((1, FOX_HEADS, LANES), lambda i: (i, 0, 0))],
        out_shape=[out_sds, out_sds, out_sds,
                   jax.ShapeDtypeStruct((s // tm, FOX_HEADS, LANES), F32)],
        scratch_shapes=[pltpu.VMEM((1, LANES), F32)],
        compiler_params=_cparams(("arbitrary",)),
        name="attn_prep",
    )(z, z, z, z, bfg_row, tril)


def _attn_kernel(jlo_ref, q_ref, k_ref, v_ref, o_ref, *, tq, tk):
    pair = pl.program_id(0)
    i = pl.program_id(1)
    lane = lax.broadcasted_iota(jnp.int32, (tq, LANES), 1)
    n_tile = tk // LANES

    def step(q, hh, j, carry, masked):
        m, acc = carry
        start = pl.multiple_of(j * tk, tk)
        k = k_ref[hh, pl.ds(start, tk), :]
        v = v_ref[hh, pl.ds(start, tk), :]
        s = lax.dot_general(q, k, (((1,), (1,)), ((), ())), preferred_element_type=F32)
        if masked:
            row = lax.broadcasted_iota(jnp.int32, (tq, tk), 0)
            col = lax.broadcasted_iota(jnp.int32, (tq, tk), 1)
            s = jnp.where(col <= row, s, -jnp.inf)
        m_new = jnp.maximum(m, jnp.max(s, axis=1, keepdims=True))
        p = jnp.exp(s - jnp.tile(m_new, (1, n_tile)))
        alpha = jnp.exp(m - m_new)
        acc = alpha * acc + jnp.dot(_bf(p), v, preferred_element_type=F32)
        return m_new, acc

    accs = []
    for hh in range(2):
        q = q_ref[hh]
        carry = (jnp.full((tq, LANES), -jnp.inf, F32), jnp.zeros((tq, LANES), F32))
        carry = lax.fori_loop(jlo_ref[(2 * pair + hh) * pl.num_programs(1) + i], i,
                              lambda j, c: step(q, hh, j, c, False), carry)
        _, acc = step(q, hh, i, carry, True)
        accs.append(acc)
    o0 = accs[0] / pltpu.roll(accs[0], FOX_HEAD_DIM, 1)
    o1 = accs[1] / pltpu.roll(accs[1], FOX_HEAD_DIM, 1)
    o_ref[...] = jnp.where(lane < FOX_HEAD_DIM, o0, o1)


def _skip_starts(stats, tm, tq):
    r = tq // tm
    nq = stats.shape[0] // r
    st = stats.reshape(nq, r, FOX_HEADS, LANES)
    f_first = st[:, 0, :, 0].T
    f_last = st[:, r - 1, :, 1].T
    qn = jnp.sqrt(jnp.max(st[:, :, :, 2], axis=1)).T
    kn = jnp.sqrt(jnp.max(st[:, :, :, 3], axis=(0, 1)))[:, None]
    thresh = f_first + 2.0 * NORM_SLACK * qn * kn + SKIP_LOG_MARGIN
    need = f_last[:, None, :] <= thresh[:, :, None]
    idx = jnp.arange(nq)
    first = jnp.min(jnp.where(need, idx[None, None, :], nq), axis=-1)
    return jnp.minimum(first, idx[None, :]).astype(jnp.int32).reshape(-1)


def _attention(jlo, qa, ka, va, tq):
    s = qa.shape[1]
    tk = tq
    kv_spec = pl.BlockSpec((2, s, LANES), lambda p, i, jl: (p, 0, 0), pipeline_mode=pl.Buffered(1))
    return pl.pallas_call(
        functools.partial(_attn_kernel, tq=tq, tk=tk),
        grid_spec=pltpu.PrefetchScalarGridSpec(
            num_scalar_prefetch=1,
            grid=(FOX_HEADS // 2, s // tq),
            in_specs=[pl.BlockSpec((2, tq, LANES), lambda p, i, jl: (p, i, 0)), kv_spec, kv_spec],
            out_specs=pl.BlockSpec((tq, LANES), lambda p, i, jl: (i, p))),
        out_shape=jax.ShapeDtypeStruct((s, WIDTH), F32),
        compiler_params=_cparams(("parallel", "arbitrary")),
        name="fox_attention",
    )(jlo, qa, ka, va)


def _gdn_consts():
    t, c = GDN_TILE, CHUNK
    r = np.arange(t)
    same = (r[:, None] // c) == (r[None, :] // c)
    kl = (same & (r[None, :] <= r[:, None])).astype(np.float32)
    ko = same.astype(np.float32)
    mbd = (same & (r[:, None] > r[None, :])).astype(np.float32)
    sel_a = np.zeros((LANES, WIDTH), np.float32)
    sel_b = np.zeros((LANES, WIDTH), np.float32)
    for h in range(GDN_HEADS):
        sel_a[SMALL_BA + h, h * GDN_HEAD_DIM:(h + 1) * GDN_HEAD_DIM] = 1.0
        sel_b[SMALL_BB + h, h * GDN_HEAD_DIM:(h + 1) * GDN_HEAD_DIM] = 1.0
    return [jnp.asarray(a) for a in (kl, ko, mbd, sel_a, sel_b)]


def _gdn_prep_kernel(x_ref, halo_ref, small_ref, cw_ref, aneg_ref, dtb_ref,
                     kl_ref, ko_ref, mbd_ref, sela_ref, selb_ref,
                     qd_ref, kd_ref, u_ref, w_ref, aqk_ref, egl_ref, xx_ref):
    t = GDN_TILE
    hd = GDN_HEAD_DIM
    i = pl.program_id(0)

    halo = halo_ref[...]
    xx_ref[0:8, :] = jnp.where(i == 0, jnp.zeros_like(halo), halo)
    xx_ref[8:8 + t, :] = x_ref[...]
    conv = cw_ref[CONV_K - 1:CONV_K, :] * xx_ref[8:8 + t, :]
    for j in range(CONV_K - 1):
        off = 8 - (CONV_K - 1) + j
        conv = conv + cw_ref[j:j + 1, :] * xx_ref[off:off + t, :]
    qkv = _silu(conv)

    small = small_ref[...]
    ba = jnp.dot(small, sela_ref[...], precision=HIGHEST, preferred_element_type=F32)
    bb = jnp.dot(small, selb_ref[...], precision=HIGHEST, preferred_element_type=F32)
    sp_in = ba + dtb_ref[...]
    softplus = jnp.maximum(sp_in, 0.0) + jnp.log1p(jnp.exp(-jnp.abs(sp_in)))
    g = aneg_ref[...] * softplus
    beta = _sigmoid(bb)
    gcum = jnp.dot(kl_ref[...], g, precision=HIGHEST, preferred_element_type=F32)
    glast = jnp.dot(ko_ref[...], g, precision=HIGHEST, preferred_element_type=F32)
    eg = jnp.exp(gcum)
    egl = jnp.exp(glast)
    egd = jnp.exp(glast - gcum)
    row8 = lax.broadcasted_iota(jnp.int32, (8, WIDTH), 0)
    egl_rows = jnp.zeros((8, WIDTH), F32)
    for c in range(CHUNKS_PER_TILE):
        egl_rows = jnp.where(row8 == c, egl[c * CHUNK:c * CHUNK + 8, :], egl_rows)
    egl_ref[...] = egl_rows

    row = lax.broadcasted_iota(jnp.int32, (t, t), 0)
    col = lax.broadcasted_iota(jnp.int32, (t, t), 1)
    same = (row // CHUNK) == (col // CHUNK)
    incl = jnp.logical_and(same, col <= row)
    strict = jnp.logical_and(same, col < row)
    eye = (row == col).astype(F32)

    for h in range(GDN_HEADS):
        cs = slice(h * hd, (h + 1) * hd)
        qh = qkv[:, h * hd:(h + 1) * hd]
        kh = qkv[:, WIDTH + h * hd:WIDTH + (h + 1) * hd]
        vh = qkv[:, 2 * WIDTH + h * hd:2 * WIDTH + (h + 1) * hd]
        qh = qh * lax.rsqrt(jnp.sum(qh * qh, axis=-1, keepdims=True) + EPS) * (hd ** -0.5)
        kh = kh * lax.rsqrt(jnp.sum(kh * kh, axis=-1, keepdims=True) + EPS)
        gh, bh = g[:, cs], beta[:, cs]
        xg = jnp.concatenate([gh, gh], axis=1) * mbd_ref[...]
        dmat = jnp.dot(kl_ref[...], xg, precision=HIGHEST, preferred_element_type=F32)
        gamma = jnp.exp(jnp.where(incl, dmat, -jnp.inf))
        kb = kh * bh
        a = jnp.where(strict, _dot_nt(kb, kh) * gamma, 0.0)
        bp = -a
        inv = eye + bp
        for _ in range(5):
            bp = _dot(bp, bp)
            inv = inv + _dot(inv, bp)
        rhs = jnp.concatenate([vh * bh, kb * eg[:, cs]], axis=1)
        sol = _dot(inv, rhs)
        u_ref[:, cs] = sol[:, :hd]
        w_ref[:, cs] = _bf(sol[:, hd:])
        aqk_ref[h] = _bf(_dot_nt(qh, kh) * gamma)
        qd_ref[:, cs] = _bf(qh * eg[:, cs])
        kd_ref[:, cs] = _bf(kh * egd[:, cs])


def _gdn_prep(z, conv_w, aneg_row, dtb_row):
    s = z.shape[0]
    t = GDN_TILE
    nt = s // t
    consts = _gdn_consts()
    full = lambda a: pl.BlockSpec(a.shape, lambda i: (0,) * a.ndim)
    row_spec = pl.BlockSpec((t, WIDTH), lambda i: (i, 0))
    return pl.pallas_call(
        _gdn_prep_kernel,
        grid=(nt,),
        in_specs=[pl.BlockSpec((t, 3 * WIDTH), lambda i: (i, COL_BQKV // (3 * WIDTH))),
                  pl.BlockSpec((8, 3 * WIDTH), lambda i: (jnp.maximum(i * (t // 8) - 1, 0), 0)),
                  pl.BlockSpec((t, LANES), lambda i: (i, COL_SMALL // LANES)),
                  full(conv_w), full(aneg_row), full(dtb_row)] + [full(a) for a in consts],
        out_specs=[row_spec, row_spec, row_spec, row_spec,
                   pl.BlockSpec((GDN_HEADS, t, t), lambda i: (0, i, 0)),
                   pl.BlockSpec((8, WIDTH), lambda i: (i, 0))],
        out_shape=[jax.ShapeDtypeStruct((s, WIDTH), BF16),
                   jax.ShapeDtypeStruct((s, WIDTH), BF16),
                   jax.ShapeDtypeStruct((s, WIDTH), F32),
                   jax.ShapeDtypeStruct((s, WIDTH), BF16),
                   jax.ShapeDtypeStruct((GDN_HEADS, s, t), BF16),
                   jax.ShapeDtypeStruct((nt * 8, WIDTH), F32)],
        scratch_shapes=[pltpu.VMEM((t + 8, 3 * WIDTH), F32)],
        compiler_params=_cparams(("parallel",)),
        name="gdn_prep",
    )(z, z, z, conv_w, aneg_row, dtb_row, *consts)


def _gdn_scan_kernel(qd_ref, kd_ref, u_ref, w_ref, aqk_ref, egl_ref, o_ref, state_ref, vn_ref):
    hd = GDN_HEAD_DIM

    @pl.when(pl.program_id(0) == 0)
    def _():
        state_ref[...] = jnp.zeros_like(state_ref)
        vn_ref[...] = jnp.zeros_like(vn_ref)

    for c in range(CHUNKS_PER_TILE):
        rs = slice(c * CHUNK, (c + 1) * CHUNK)
        for h in range(GDN_HEADS):
            cs = slice(h * hd, (h + 1) * hd)
            st = state_ref[h]
            lhs = jnp.concatenate([w_ref[rs, cs], qd_ref[rs, cs]], axis=0)
            r = jnp.dot(lhs, _bf(st), preferred_element_type=F32)
            v_new = u_ref[rs, cs] - r[:CHUNK]
            v_new_b = _bf(v_new)
            vn_ref[h, rs, :] = v_new_b
            o_ref[rs, cs] = r[CHUNK:] + jnp.dot(aqk_ref[h, rs, :], vn_ref[h],
                                                preferred_element_type=F32)
            upd = lax.dot_general(kd_ref[rs, cs], v_new_b, (((0,), (0,)), ((), ())),
                                  preferred_element_type=F32)
            state_ref[h] = st * egl_ref[c:c + 1, cs] + upd


def _gdn_scan(qd, kd, u, w, aqk, egl):
    s = qd.shape[0]
    t = GDN_TILE
    row_spec = pl.BlockSpec((t, WIDTH), lambda i: (i, 0))
    return pl.pallas_call(
        _gdn_scan_kernel,
        grid=(s // t,),
        in_specs=[row_spec, row_spec, row_spec, row_spec,
                  pl.BlockSpec((GDN_HEADS, t, t), lambda i: (0, i, 0)),
                  pl.BlockSpec((8, WIDTH), lambda i: (i, 0))],
        out_specs=row_spec,
        out_shape=jax.ShapeDtypeStruct((s, WIDTH), F32),
        scratch_shapes=[pltpu.VMEM((GDN_HEADS, GDN_HEAD_DIM, GDN_HEAD_DIM), F32),
                        pltpu.VMEM((GDN_HEADS, t, GDN_HEAD_DIM), BF16)],
        compiler_params=_cparams(("arbitrary",)),
        name="gdn_scan",
    )(qd, kd, u, w, aqk, egl)


def _merge_kernel(x_ref, oa_ref, ob_ref, mq_ref, az_ref, bz_ref, mz_ref, g0_ref, g1_ref, g2_ref,
                  mk_ref, mv_ref, gng_ref, bm_ref, wb_ref, wo_ref, fg_ref, o_ref, *, final_norm):
    hd = GDN_HEAD_DIM
    y_a = oa_ref[...] * _silu(az_ref[...])

    ob = ob_ref[...]
    normed = []
    for h in range(GDN_HEADS):
        oh = ob[:, h * hd:(h + 1) * hd]
        normed.append(oh * lax.rsqrt(jnp.mean(oh * oh, axis=-1, keepdims=True) + EPS))
    y_b = jnp.concatenate(normed, axis=1) * gng_ref[...] * _silu(bz_ref[...])

    om = []
    for h in range(MEM_HEADS):
        cs = slice(h * MEM_HEAD_DIM, (h + 1) * MEM_HEAD_DIM)
        sc = _dot_nt(mq_ref[:, cs], mk_ref[:, cs]) * (MEM_HEAD_DIM ** -0.5)
        p = jnp.exp(sc - jnp.max(sc, axis=-1, keepdims=True))
        om.append(jnp.dot(_bf(p), mv_ref[:, cs], preferred_element_type=F32)
                  / jnp.sum(p, axis=-1, keepdims=True))
    y_m = jnp.concatenate(om, axis=1) * _silu(mz_ref[...])

    merged = None
    for n, (y, g_ref) in enumerate(((y_a, g0_ref), (y_b, g1_ref), (y_m, g2_ref))):
        proj = jnp.dot(_bf(y), wb_ref[n], preferred_element_type=F32)
        term = _sigmoid(g_ref[...] + bm_ref[n:n + 1, :]) * proj
        merged = term if merged is None else merged + term
    out = x_ref[...] + jnp.dot(_bf(merged), wo_ref[...], preferred_element_type=F32)
    if final_norm:
        out = out * lax.rsqrt(jnp.mean(out * out, axis=-1, keepdims=True) + EPS) * fg_ref[...]
    o_ref[...] = out


def _merge(x, o_a, o_b, z, memkv, gng_row, bm, wb, wo, fg_row, tm, final_norm):
    s = x.shape[0]
    zc = lambda col, width: pl.BlockSpec((tm, width), lambda i: (i, col // width))
    full = lambda a: pl.BlockSpec(a.shape, lambda i: (0,) * a.ndim)
    n_mem = memkv.shape[0]
    return pl.pallas_call(
        functools.partial(_merge_kernel, final_norm=final_norm),
        grid=(s // tm,),
        in_specs=[pl.BlockSpec((tm, D_MODEL), lambda i: (i, 0)),
                  pl.BlockSpec((tm, WIDTH), lambda i: (i, 0)),
                  pl.BlockSpec((tm, WIDTH), lambda i: (i, 0)),
                  zc(COL_AQ + 3 * WIDTH, WIDTH),
                  zc(COL_AZ, WIDTH), zc(COL_AZ + WIDTH, WIDTH), zc(COL_AZ + 2 * WIDTH, WIDTH),
                  zc(COL_GATES, D_MODEL), zc(COL_GATES + D_MODEL, D_MODEL),
                  zc(COL_GATES + 2 * D_MODEL, D_MODEL),
                  pl.BlockSpec((n_mem, WIDTH), lambda i: (0, 0)),
                  pl.BlockSpec((n_mem, WIDTH), lambda i: (0, 1)),
                  full(gng_row), full(bm), full(wb), full(wo), full(fg_row)],
        out_specs=pl.BlockSpec((tm, D_MODEL), lambda i: (i, 0)),
        out_shape=jax.ShapeDtypeStruct((s, D_MODEL), F32),
        compiler_params=_cparams(("parallel",)),
        name="merge",
    )(x, o_a, o_b, z, z, z, z, z, z, z, memkv, memkv, gng_row, bm, wb, wo, fg_row)


def _permute_w_in(w_in):
    w = WIDTH
    sizes = (w, w, w, FOX_HEADS, w, w, w, w, GDN_HEADS, GDN_HEADS, w, w, w, N_BRANCH * D_MODEL)
    offs = np.concatenate([[0], np.cumsum(sizes)])
    part = lambda k: w_in[:, offs[k]:offs[k + 1]]
    aq, ak, av, af, az, bq, bk, bv, ba, bb, bz, mq, mz, gates = (part(k) for k in range(14))
    pad = jnp.zeros((w_in.shape[0], LANES - FOX_HEADS - 2 * GDN_HEADS), w_in.dtype)
    cols = [bq, bk, bv, aq * (FOX_HEAD_DIM ** -0.5), ak, av, mq, az, bz, mz, gates, af, ba, bb, pad]
    return _bf(jnp.concatenate(cols, axis=1))


def _layer(x, mem, norm_g, w_in, b_fg, b_merge, conv_w, a_log, dt_bias, gdn_norm_g, mem_norm_g,
           w_mem_kv, w_branch, w_out, final_norm_g, final_norm):
    s = x.shape[0]
    row = lambda v: v.reshape(1, -1).astype(F32)
    z = _norm_matmul(x, row(norm_g), _permute_w_in(w_in), tm=min(s, 1024), tn=N_COLS // 5,
                     out_dtype=F32)
    memkv = _norm_matmul(mem, row(mem_norm_g), _bf(w_mem_kv), tm=mem.shape[0], tn=2 * WIDTH,
                         out_dtype=BF16)

    bfg_row = jnp.zeros((1, LANES), F32).at[0, :FOX_HEADS].set(b_fg)
    tq = min(s, 512)
    qa, ka, va, stats = _attn_prep(z, bfg_row, tm=ATTN_PREP_TILE)
    o_a = _attention(_skip_starts(stats, ATTN_PREP_TILE, tq), qa, ka, va, tq=tq)

    aneg_row = row(jnp.repeat(-jnp.exp(a_log.astype(F32)), GDN_HEAD_DIM))
    dtb_row = row(jnp.repeat(dt_bias, GDN_HEAD_DIM))
    qd, kd, u, w, aqk, egl = _gdn_prep(z, conv_w, aneg_row, dtb_row)
    o_b = _gdn_scan(qd, kd, u, w, aqk, egl)

    return _merge(x, o_a, o_b, z, memkv, row(jnp.tile(gdn_norm_g, GDN_HEADS)),
                  b_merge.reshape(N_BRANCH, D_MODEL), _bf(w_branch), _bf(w_out),
                  row(final_norm_g), tm=256, final_norm=final_norm)


def kernel(x, mem, norm_g, w_in, b_fg, b_merge, conv_w, a_log, dt_bias, gdn_norm_g, mem_norm_g,
           w_mem_kv, w_branch, w_out, final_norm_g):
    assert x.shape[0] == 1 and mem.shape[0] == 1
    depth = w_in.shape[0]
    h = x[0]
    for l in range(depth):
        h = _layer(h, mem[0], norm_g[l], w_in[l], b_fg[l], b_merge[l], conv_w[l], a_log[l],
                   dt_bias[l], gdn_norm_g[l], mem_norm_g[l], w_mem_kv[l], w_branch[l], w_out[l],
                   final_norm_g, final_norm=(l == depth - 1))
    return h[None]
```

```python
import functools

import jax
import jax.numpy as jnp
import numpy as np
from jax import lax
from jax.experimental import pallas as pl
from jax.experimental.pallas import tpu as pltpu

F32 = jnp.float32
BF16 = jnp.bfloat16

D_MODEL = 1024
EPS = 1e-6
FOX_HEADS = 8
FOX_HEAD_DIM = 64
GDN_HEADS = 4
GDN_HEAD_DIM = 128
MEM_HEADS = 4
MEM_HEAD_DIM = 128
WIDTH = 512
N_BRANCH = 3
CHUNK = 64
CONV_K = 4
LANES = 128
BF16_SUBLANES = 16
GDN_TILE = 256
CHUNKS_PER_TILE = GDN_TILE // CHUNK

COL_BQKV = 0
COL_AQ = 1536
COL_AZ = 3584
COL_GATES = 5120
N_MAIN = 8192
SMALL_BA = 8
SMALL_BB = 12

NORM_TM = 1024
NORM_TN = 2048
ATTN_PREP_TILE = 256
ATTN_TQ = 512
GDN_PREP_ROWS = 512
MERGE_TM = 512
MERGE_SUB = 256

SKIP_LOG_MARGIN = 100.0
NORM_SLACK = 1.001

VMEM_LIMIT = 56 * 1024 * 1024


def _cparams(sem):
    return pltpu.CompilerParams(dimension_semantics=sem, vmem_limit_bytes=VMEM_LIMIT)


def _bf(x):
    return x.astype(BF16)


def _dot(a, b):
    return jnp.dot(_bf(a), _bf(b), preferred_element_type=F32)


def _dot_nt(a, b):
    return lax.dot_general(_bf(a), _bf(b), (((1,), (1,)), ((), ())), preferred_element_type=F32)


def _sigmoid(x):
    return 1.0 / (1.0 + jnp.exp(-x))


def _silu(x):
    return x * _sigmoid(x)


def _split3(x):
    hi = _bf(x).astype(F32)
    r1 = x - hi
    mid = _bf(r1).astype(F32)
    lo = _bf(r1 - mid).astype(F32)
    return hi, mid, lo


def _dot_exact_lhs(mat01, x):
    hi, mid, lo = _split3(x)
    return (jnp.dot(mat01, _bf(hi), preferred_element_type=F32)
            + jnp.dot(mat01, _bf(mid), preferred_element_type=F32)
            + jnp.dot(mat01, _bf(lo), preferred_element_type=F32))


def _layer_spec(shape, layer, n_grid):
    zeros = (0,) * len(shape)
    if n_grid == 1:
        return pl.BlockSpec((None,) + tuple(shape), lambda i: (layer,) + zeros)
    return pl.BlockSpec((None,) + tuple(shape), lambda i, j: (layer,) + zeros)


def _norm_matmul_kernel(x_ref, g_ref, w_ref, *rest, with_small):
    if with_small:
        ws_ref, o_ref, os_ref, h_ref = rest
    else:
        o_ref, h_ref = rest

    @pl.when(pl.program_id(1) == 0)
    def _():
        x = x_ref[...]
        y = x * lax.rsqrt(jnp.mean(x * x, axis=-1, keepdims=True) + EPS)
        h = _bf(y * g_ref[...])
        h_ref[...] = h
        if with_small:
            os_ref[...] = jnp.dot(h, ws_ref[...], preferred_element_type=F32)

    o_ref[...] = jnp.dot(h_ref[...], w_ref[...], preferred_element_type=F32).astype(o_ref.dtype)


def _norm_matmul(x, g, w, w_small, layer, tm, tn):
    s, d = x.shape
    n = w.shape[2]
    with_small = w_small is not None
    in_specs = [pl.BlockSpec((tm, d), lambda i, j: (i, 0)),
                _layer_spec((1, d), layer, 2),
                pl.BlockSpec((None, d, tn), lambda i, j: (layer, 0, j))]
    out_specs = [pl.BlockSpec((tm, tn), lambda i, j: (i, j))]
    out_shape = [jax.ShapeDtypeStruct((s, n), BF16)]
    args = [x, g, w]
    if with_small:
        in_specs.append(_layer_spec((d, LANES), layer, 2))
        out_specs.append(pl.BlockSpec((tm, LANES), lambda i, j: (i, 0)))
        out_shape.append(jax.ShapeDtypeStruct((s, LANES), F32))
        args.append(w_small)
    return pl.pallas_call(
        functools.partial(_norm_matmul_kernel, with_small=with_small),
        grid=(s // tm, n // tn),
        in_specs=in_specs, out_specs=out_specs, out_shape=out_shape,
        scratch_shapes=[pltpu.VMEM((tm, d), BF16)],
        compiler_params=_cparams(("parallel", "arbitrary")),
        name="norm_matmul",
    )(*args)


def _attn_prep_kernel(q_ref, k_ref, v_ref, small_ref, bfg_ref, tril_ref,
                      qa_ref, ka_ref, va_ref, stats_ref, carry_ref):
    tm = q_ref.shape[0]
    lane1 = lax.broadcasted_iota(jnp.int32, (1, LANES), 1)
    row8 = lax.broadcasted_iota(jnp.int32, (FOX_HEADS, LANES), 0)
    stats = jnp.zeros((FOX_HEADS, LANES), F32)

    @pl.when(pl.program_id(0) == 0)
    def _():
        carry_ref[...] = jnp.zeros_like(carry_ref)

    af = small_ref[...] + bfg_ref[...]
    logf = jnp.minimum(af, 0.0) - jnp.log1p(jnp.exp(-jnp.abs(af)))
    cum = carry_ref[...] + _dot_exact_lhs(tril_ref[...], logf)
    carry_ref[...] = cum[tm - 1:tm, :]

    lane = lax.broadcasted_iota(jnp.int32, (tm, LANES), 1)
    one = jnp.ones((tm, LANES), F32)
    zero = jnp.zeros((tm, LANES), F32)
    for h in range(FOX_HEADS):
        p, odd = divmod(h, 2)
        cols = slice(p * LANES, (p + 1) * LANES)
        q = q_ref[:, cols].astype(F32)
        k = k_ref[:, cols].astype(F32)
        v = v_ref[:, cols].astype(F32)
        fb = jnp.broadcast_to(cum[:, h:h + 1], (tm, LANES))
        hi, mid, lo = _split3(fb)
        aux = lane - (0 if odd else FOX_HEAD_DIM)
        is_feat = (lane >= FOX_HEAD_DIM) if odd else (lane < FOX_HEAD_DIM)
        q_aux = jnp.where(aux == 0, hi, jnp.where(aux == 1, mid, jnp.where(aux == 2, lo,
                          jnp.where(aux < 6, one, zero))))
        k_aux = jnp.where(aux < 3, one, jnp.where(aux == 3, -hi, jnp.where(aux == 4, -mid,
                          jnp.where(aux == 5, -lo, zero))))
        qa_ref[h] = _bf(jnp.where(is_feat, q, q_aux))
        ka_ref[h] = _bf(jnp.where(is_feat, k, k_aux))
        va_ref[h] = _bf(jnp.where(is_feat, v, one))
        qf = jnp.where(is_feat, q, zero)
        kf = jnp.where(is_feat, k, zero)
        qn2 = jnp.max(jnp.sum(qf * qf, axis=1, keepdims=True), axis=0, keepdims=True)
        kn2 = jnp.max(jnp.sum(kf * kf, axis=1, keepdims=True), axis=0, keepdims=True)
        srow = jnp.where(lane1 == 0, cum[0:1, h:h + 1],
                         jnp.where(lane1 == 1, cum[tm - 1:tm, h:h + 1],
                                   jnp.where(lane1 == 2, qn2, jnp.where(lane1 == 3, kn2, 0.0))))
        stats = jnp.where(row8 == h, jnp.broadcast_to(srow, (FOX_HEADS, LANES)), stats)
    stats_ref[0] = stats


def _attn_prep(z, small, bfg, layer, tm):
    s = z.shape[0]
    tril = jnp.asarray(np.tril(np.ones((tm, tm), np.float32)), BF16)
    cb = COL_AQ // WIDTH
    out_sds = jax.ShapeDtypeStruct((FOX_HEADS, s, LANES), BF16)
    out_spec = pl.BlockSpec((FOX_HEADS, tm, LANES), lambda i: (0, i, 0))
    return pl.pallas_call(
        _attn_prep_kernel,
        grid=(s // tm,),
        in_specs=[pl.BlockSpec((tm, WIDTH), lambda i: (i, cb)),
                  pl.BlockSpec((tm, WIDTH), lambda i: (i, cb + 1)),
                  pl.BlockSpec((tm, WIDTH), lambda i: (i, cb + 2)),
                  pl.BlockSpec((tm, LANES), lambda i: (i, 0)),
                  _layer_spec((1, LANES), layer, 1),
                  pl.BlockSpec((tm, tm), lambda i: (0, 0))],
        out_specs=[out_spec, out_spec, out_spec,
                   pl.BlockSpec((1, FOX_HEADS, LANES), lambda i: (i, 0, 0))],
        out_shape=[out_sds, out_sds, out_sds,
                   jax.ShapeDtypeStruct((s // tm, FOX_HEADS, LANES), F32)],
        scratch_shapes=[pltpu.VMEM((1, LANES), F32)],
        compiler_params=_cparams(("arbitrary",)),
        name="attn_prep",
    )(z, z, z, small, bfg, tril)


def _attn_kernel(jlo_ref, q_ref, k_ref, v_ref, o_ref, s0_ref, s1_ref, *, tq, tk):
    pair = pl.program_id(0)
    i = pl.program_id(1)
    lane = lax.broadcasted_iota(jnp.int32, (tq, LANES), 1)
    n_tile = tk // LANES

    def logits(q, hh, j, s_ref):
        start = pl.multiple_of(j * tk, tk)
        s_ref[...] = lax.dot_general(q, k_ref[hh, pl.ds(start, tk), :], (((1,), (1,)), ((), ())),
                                     preferred_element_type=F32)

    def softmax_pv(hh, j, s_ref, carry, masked):
        m, acc = carry
        start = pl.multiple_of(j * tk, tk)
        if masked:
            row = lax.broadcasted_iota(jnp.int32, (tq, tk), 0)
            col = lax.broadcasted_iota(jnp.int32, (tq, tk), 1)
            s_ref[...] = jnp.where(col <= row, s_ref[...], -jnp.inf)
        m_new = jnp.maximum(m, jnp.max(s_ref[...], axis=1, keepdims=True))
        p = jnp.exp(s_ref[...] - jnp.tile(m_new, (1, n_tile)))
        alpha = jnp.exp(m - m_new)
        acc = alpha * acc + jnp.dot(_bf(p), v_ref[hh, pl.ds(start, tk), :],
                                    preferred_element_type=F32)
        return m_new, acc

    accs = []
    for hh in range(2):
        q = q_ref[hh]
        jlo = jlo_ref[(2 * pair + hh) * pl.num_programs(1) + i]
        n_full = i - jlo
        logits(q, hh, jlo, s0_ref)

        def two_blocks(t, carry, q=q, hh=hh, jlo=jlo):
            j = jlo + 2 * t
            logits(q, hh, j + 1, s1_ref)
            carry = softmax_pv(hh, j, s0_ref, carry, False)
            logits(q, hh, j + 2, s0_ref)
            return softmax_pv(hh, j + 1, s1_ref, carry, False)

        def tail_odd(carry, q=q, hh=hh):
            logits(q, hh, i, s1_ref)
            carry = softmax_pv(hh, i - 1, s0_ref, carry, False)
            return softmax_pv(hh, i, s1_ref, carry, True)

        def tail_even(carry, hh=hh):
            return softmax_pv(hh, i, s0_ref, carry, True)

        carry = (jnp.full((tq, LANES), -jnp.inf, F32), jnp.zeros((tq, LANES), F32))
        carry = lax.fori_loop(0, n_full // 2, two_blocks, carry)
        _, acc = lax.cond(n_full % 2 == 1, tail_odd, tail_even, carry)
        accs.append(acc)
    o0 = accs[0] / pltpu.roll(accs[0], FOX_HEAD_DIM, 1)
    o1 = accs[1] / pltpu.roll(accs[1], FOX_HEAD_DIM, 1)
    o_ref[...] = jnp.where(lane < FOX_HEAD_DIM, o0, o1)


def _skip_starts(stats, tm, tq):
    r = tq // tm
    nq = stats.shape[0] // r
    st = stats.reshape(nq, r, FOX_HEADS, LANES)
    f_first = st[:, 0, :, 0].T
    f_last = st[:, r - 1, :, 1].T
    qn = jnp.sqrt(jnp.max(st[:, :, :, 2], axis=1)).T
    kn = jnp.sqrt(jnp.max(st[:, :, :, 3], axis=(0, 1)))[:, None]
    thresh = f_first + 2.0 * NORM_SLACK * qn * kn + SKIP_LOG_MARGIN
    need = f_last[:, None, :] <= thresh[:, :, None]
    idx = jnp.arange(nq)
    first = jnp.min(jnp.where(need, idx[None, None, :], nq), axis=-1)
    return jnp.minimum(first, idx[None, :]).astype(jnp.int32).reshape(-1)


def _attention(jlo, qa, ka, va, tq):
    s = qa.shape[1]
    tk = tq
    kv_spec = pl.BlockSpec((2, s, LANES), lambda p, i, jl: (p, 0, 0), pipeline_mode=pl.Buffered(1))
    return pl.pallas_call(
        functools.partial(_attn_kernel, tq=tq, tk=tk),
        grid_spec=pltpu.PrefetchScalarGridSpec(
            num_scalar_prefetch=1,
            grid=(FOX_HEADS // 2, s // tq),
            in_specs=[pl.BlockSpec((2, tq, LANES), lambda p, i, jl: (p, i, 0)), kv_spec, kv_spec],
            out_specs=pl.BlockSpec((tq, LANES), lambda p, i, jl: (i, p)),
            scratch_shapes=[pltpu.VMEM((tq, tk), F32), pltpu.VMEM((tq, tk), F32)]),
        out_shape=jax.ShapeDtypeStruct((s, WIDTH), F32),
        compiler_params=_cparams(("parallel", "arbitrary")),
        name="fox_attention",
    )(jlo, qa, ka, va)


def _chunk_cumsum_matrix():
    r = np.arange(GDN_TILE)
    same = (r[:, None] // CHUNK) == (r[None, :] // CHUNK)
    return jnp.asarray((same & (r[None, :] <= r[:, None])).astype(np.float32), BF16)


def _gdn_prep_kernel(x_ref, halo_ref, small_ref, cw_ref, aneg_ref, dtb_ref, kl_ref,
                     qd_ref, kd_ref, u_ref, w_ref, aqk_ref, egl_ref, xx_ref):
    t = GDN_TILE
    hd = GDN_HEAD_DIM
    pad = BF16_SUBLANES
    rows_in = x_ref.shape[0]

    halo = halo_ref[...].astype(F32)
    xx_ref[0:pad, :] = jnp.where(pl.program_id(0) == 0, jnp.zeros_like(halo), halo)
    xx_ref[pad:pad + rows_in, :] = x_ref[...].astype(F32)

    row = lax.broadcasted_iota(jnp.int32, (t, t), 0)
    col = lax.broadcasted_iota(jnp.int32, (t, t), 1)
    same = (row // CHUNK) == (col // CHUNK)
    incl = jnp.logical_and(same, col <= row)
    strict = jnp.logical_and(same, col < row)
    eye = (row == col).astype(F32)
    row8 = lax.broadcasted_iota(jnp.int32, (8, WIDTH), 0)

    invs, bps, rhss, dests = [], [], [], []
    for tile in range(rows_in // t):
        r0 = tile * t
        rs = slice(r0, r0 + t)
        conv = cw_ref[CONV_K - 1:CONV_K, :] * xx_ref[pad + r0:pad + r0 + t, :]
        for j in range(CONV_K - 1):
            off = pad + r0 - (CONV_K - 1) + j
            conv = conv + cw_ref[j:j + 1, :] * xx_ref[off:off + t, :]
        qkv = _silu(conv)

        small = small_ref[rs, :]
        bcast = lambda c0: jnp.concatenate(
            [jnp.broadcast_to(small[:, c0 + h:c0 + h + 1], (t, hd)) for h in range(GDN_HEADS)],
            axis=1)
        sp_in = bcast(SMALL_BA) + dtb_ref[...]
        softplus = jnp.maximum(sp_in, 0.0) + jnp.log1p(jnp.exp(-jnp.abs(sp_in)))
        g = aneg_ref[...] * softplus
        beta = _sigmoid(bcast(SMALL_BB))
        gcum = _dot_exact_lhs(kl_ref[...], g)
        glast = jnp.concatenate(
            [jnp.broadcast_to(gcum[(c + 1) * CHUNK - 1:(c + 1) * CHUNK, :], (CHUNK, WIDTH))
             for c in range(CHUNKS_PER_TILE)], axis=0)
        eg = jnp.exp(gcum)
        egl = jnp.exp(glast)
        egd = jnp.exp(glast - gcum)
        egl_rows = jnp.zeros((8, WIDTH), F32)
        for c in range(CHUNKS_PER_TILE):
            egl_rows = jnp.where(row8 == c, egl[c * CHUNK:c * CHUNK + 8, :], egl_rows)
        egl_ref[tile * 8:(tile + 1) * 8, :] = egl_rows

        for h in range(GDN_HEADS):
            cs = slice(h * hd, (h + 1) * hd)
            qh = qkv[:, h * hd:(h + 1) * hd]
            kh = qkv[:, WIDTH + h * hd:WIDTH + (h + 1) * hd]
            vh = qkv[:, 2 * WIDTH + h * hd:2 * WIDTH + (h + 1) * hd]
            qh = qh * lax.rsqrt(jnp.sum(qh * qh, axis=-1, keepdims=True) + EPS) * (hd ** -0.5)
            kh = kh * lax.rsqrt(jnp.sum(kh * kh, axis=-1, keepdims=True) + EPS)
            bh = beta[:, cs]
            gc = gcum[:, cs]
            dmat = jnp.concatenate([gc, gc], axis=1) - gc.T[0:1, :]
            gamma = jnp.exp(jnp.where(incl, dmat, -jnp.inf))
            kb = kh * bh
            a = jnp.where(strict, _dot_nt(kb, kh) * gamma, 0.0)
            invs.append(eye - a)
            bps.append(_bf(-a))
            rhss.append(_bf(jnp.concatenate([vh * bh, kb * eg[:, cs]], axis=1)))
            aqk_ref[h, rs, :] = _bf(_dot_nt(qh, kh) * gamma)
            qd_ref[rs, cs] = _bf(qh * eg[:, cs])
            kd_ref[rs, cs] = _bf(kh * egd[:, cs])
            dests.append((rs, cs))

    for _ in range(5):
        bps = [_bf(jnp.dot(b, b, preferred_element_type=F32)) for b in bps]
        invs = [inv + jnp.dot(_bf(inv), b, preferred_element_type=F32)
                for inv, b in zip(invs, bps)]
    for inv, rhs, (rs, cs) in zip(invs, rhss, dests):
        sol = jnp.dot(_bf(inv), rhs, preferred_element_type=F32)
        u_ref[rs, cs] = sol[:, :hd]
        w_ref[rs, cs] = _bf(sol[:, hd:])


def _gdn_prep(z, small, conv_w, aneg, dtb, layer, rows):
    s = z.shape[0]
    t = GDN_TILE
    pad = BF16_SUBLANES
    tiles = rows // t
    kl = _chunk_cumsum_matrix()
    row_spec = pl.BlockSpec((rows, WIDTH), lambda i: (i, 0))
    return pl.pallas_call(
        _gdn_prep_kernel,
        grid=(s // rows,),
        in_specs=[pl.BlockSpec((rows, 3 * WIDTH), lambda i: (i, COL_BQKV // (3 * WIDTH))),
                  pl.BlockSpec((pad, 3 * WIDTH),
                               lambda i: (jnp.maximum(i * (rows // pad) - 1, 0), 0)),
                  pl.BlockSpec((rows, LANES), lambda i: (i, 0)),
                  _layer_spec((CONV_K, 3 * WIDTH), layer, 1),
                  _layer_spec((1, WIDTH), layer, 1),
                  _layer_spec((1, WIDTH), layer, 1),
                  pl.BlockSpec((t, t), lambda i: (0, 0))],
        out_specs=[row_spec, row_spec, row_spec, row_spec,
                   pl.BlockSpec((GDN_HEADS, rows, t), lambda i: (0, i, 0)),
                   pl.BlockSpec((8 * tiles, WIDTH), lambda i: (i, 0))],
        out_shape=[jax.ShapeDtypeStruct((s, WIDTH), BF16),
                   jax.ShapeDtypeStruct((s, WIDTH), BF16),
                   jax.ShapeDtypeStruct((s, WIDTH), F32),
                   jax.ShapeDtypeStruct((s, WIDTH), BF16),
                   jax.ShapeDtypeStruct((GDN_HEADS, s, t), BF16),
                   jax.ShapeDtypeStruct((s // t * 8, WIDTH), F32)],
        scratch_shapes=[pltpu.VMEM((rows + pad, 3 * WIDTH), F32)],
        compiler_params=_cparams(("parallel",)),
        name="gdn_prep",
    )(z, z, small, conv_w, aneg, dtb, kl)


def _gdn_scan_kernel(qd_ref, kd_ref, u_ref, w_ref, aqk_ref, egl_ref, o_ref, state_ref, vn_ref):
    hd = GDN_HEAD_DIM

    @pl.when(pl.program_id(0) == 0)
    def _():
        state_ref[...] = jnp.zeros_like(state_ref)
        vn_ref[...] = jnp.zeros_like(vn_ref)

    heads = range(GDN_HEADS)
    col = lambda h: slice(h * hd, (h + 1) * hd)
    for c in range(CHUNKS_PER_TILE):
        rs = slice(c * CHUNK, (c + 1) * CHUNK)
        sts = [state_ref[h] for h in heads]
        rr = [jnp.dot(jnp.concatenate([w_ref[rs, col(h)], qd_ref[rs, col(h)]], axis=0),
                      _bf(sts[h]), preferred_element_type=F32) for h in heads]
        vns = [_bf(u_ref[rs, col(h)] - rr[h][:CHUNK]) for h in heads]
        for h in heads:
            vn_ref[h, rs, :] = vns[h]
        for h in heads:
            upd = lax.dot_general(kd_ref[rs, col(h)], vns[h], (((0,), (0,)), ((), ())),
                                  preferred_element_type=F32)
            state_ref[h] = sts[h] * egl_ref[c:c + 1, col(h)] + upd
        for h in heads:
            o_ref[rs, col(h)] = rr[h][CHUNK:] + jnp.dot(aqk_ref[h, rs, :], vn_ref[h],
                                                        preferred_element_type=F32)


def _gdn_scan(qd, kd, u, w, aqk, egl):
    s = qd.shape[0]
    t = GDN_TILE
    row_spec = pl.BlockSpec((t, WIDTH), lambda i: (i, 0))
    return pl.pallas_call(
        _gdn_scan_kernel,
        grid=(s // t,),
        in_specs=[row_spec, row_spec, row_spec, row_spec,
                  pl.BlockSpec((GDN_HEADS, t, t), lambda i: (0, i, 0)),
                  pl.BlockSpec((8, WIDTH), lambda i: (i, 0))],
        out_specs=row_spec,
        out_shape=jax.ShapeDtypeStruct((s, WIDTH), F32),
        scratch_shapes=[pltpu.VMEM((GDN_HEADS, GDN_HEAD_DIM, GDN_HEAD_DIM), F32),
                        pltpu.VMEM((GDN_HEADS, t, GDN_HEAD_DIM), BF16)],
        compiler_params=_cparams(("arbitrary",)),
        name="gdn_scan",
    )(qd, kd, u, w, aqk, egl)


def _merge_kernel(x_ref, oa_ref, ob_ref, mq_ref, az_ref, bz_ref, mz_ref, g0_ref, g1_ref, g2_ref,
                  mk_ref, mv_ref, gng_ref, bm_ref, wb_ref, wo_ref, fg_ref, o_ref, *, final_norm):
    hd = GDN_HEAD_DIM
    groups = [slice(r0, r0 + MERGE_SUB) for r0 in range(0, x_ref.shape[0], MERGE_SUB)]
    g_refs = (g0_ref, g1_ref, g2_ref)

    def branch_a(rs):
        return oa_ref[rs, :] * _silu(az_ref[rs, :].astype(F32))

    def branch_b(rs):
        normed = []
        for h in range(GDN_HEADS):
            oh = ob_ref[rs, h * hd:(h + 1) * hd]
            normed.append(oh * lax.rsqrt(jnp.mean(oh * oh, axis=-1, keepdims=True) + EPS))
        return jnp.concatenate(normed, axis=1) * gng_ref[...] * _silu(bz_ref[rs, :].astype(F32))

    def branch_m(rs):
        om = []
        for h in range(MEM_HEADS):
            cs = slice(h * MEM_HEAD_DIM, (h + 1) * MEM_HEAD_DIM)
            sc = _dot_nt(mq_ref[rs, cs], mk_ref[:, cs]) * (MEM_HEAD_DIM ** -0.5)
            p = jnp.exp(sc - jnp.max(sc, axis=-1, keepdims=True))
            om.append(jnp.dot(_bf(p), mv_ref[:, cs], preferred_element_type=F32)
                      / jnp.sum(p, axis=-1, keepdims=True))
        return jnp.concatenate(om, axis=1) * _silu(mz_ref[rs, :].astype(F32))

    merged = [None] * len(groups)
    for n, branch in enumerate((branch_a, branch_b, branch_m)):
        for k, rs in enumerate(groups):
            proj = jnp.dot(_bf(branch(rs)), wb_ref[n], preferred_element_type=F32)
            term = _sigmoid(g_refs[n][rs, :].astype(F32) + bm_ref[n:n + 1, :]) * proj
            merged[k] = term if merged[k] is None else merged[k] + term
    for k, rs in enumerate(groups):
        out = x_ref[rs, :] + jnp.dot(_bf(merged[k]), wo_ref[...], preferred_element_type=F32)
        if final_norm:
            out = out * lax.rsqrt(jnp.mean(out * out, axis=-1, keepdims=True) + EPS) * fg_ref[...]
        o_ref[rs, :] = out


def _merge(x, o_a, o_b, z, memkv, gng, bm, wb, wo, fg_row, layer, tm, final_norm):
    s = x.shape[0]
    zc = lambda col, width: pl.BlockSpec((tm, width), lambda i: (i, col // width))
    n_mem = memkv.shape[0]
    return pl.pallas_call(
        functools.partial(_merge_kernel, final_norm=final_norm),
        grid=(s // tm,),
        in_specs=[pl.BlockSpec((tm, D_MODEL), lambda i: (i, 0)),
                  pl.BlockSpec((tm, WIDTH), lambda i: (i, 0)),
                  pl.BlockSpec((tm, WIDTH), lambda i: (i, 0)),
                  zc(COL_AQ + 3 * WIDTH, WIDTH),
                  zc(COL_AZ, WIDTH), zc(COL_AZ + WIDTH, WIDTH), zc(COL_AZ + 2 * WIDTH, WIDTH),
                  zc(COL_GATES, D_MODEL), zc(COL_GATES + D_MODEL, D_MODEL),
                  zc(COL_GATES + 2 * D_MODEL, D_MODEL),
                  pl.BlockSpec((n_mem, WIDTH), lambda i: (0, 0)),
                  pl.BlockSpec((n_mem, WIDTH), lambda i: (0, 1)),
                  _layer_spec((1, WIDTH), layer, 1),
                  _layer_spec((N_BRANCH, D_MODEL), layer, 1),
                  _layer_spec((N_BRANCH, WIDTH, D_MODEL), layer, 1),
                  _layer_spec((D_MODEL, D_MODEL), layer, 1),
                  pl.BlockSpec((1, D_MODEL), lambda i: (0, 0))],
        out_specs=pl.BlockSpec((tm, D_MODEL), lambda i: (i, 0)),
        out_shape=jax.ShapeDtypeStruct((s, D_MODEL), F32),
        compiler_params=_cparams(("parallel",)),
        name="merge",
    )(x, o_a, o_b, z, z, z, z, z, z, z, memkv, memkv, gng, bm, wb, wo, fg_row)


def _permute_w_in(w_in):
    w = WIDTH
    sizes = (w, w, w, FOX_HEADS, w, w, w, w, GDN_HEADS, GDN_HEADS, w, w, w, N_BRANCH * D_MODEL)
    offs = np.concatenate([[0], np.cumsum(sizes)])
    part = lambda k: w_in[..., offs[k]:offs[k + 1]]
    aq, ak, av, af, az, bq, bk, bv, ba, bb, bz, mq, mz, gates = (part(k) for k in range(14))
    main = [bq, bk, bv, aq * (FOX_HEAD_DIM ** -0.5), ak, av, mq, az, bz, mz, gates]
    pad = jnp.zeros(w_in.shape[:-1] + (LANES - FOX_HEADS - 2 * GDN_HEADS,), w_in.dtype)
    return _bf(jnp.concatenate(main, axis=-1)), _bf(jnp.concatenate([af, ba, bb, pad], axis=-1))


def kernel(x, mem, norm_g, w_in, b_fg, b_merge, conv_w, a_log, dt_bias, gdn_norm_g, mem_norm_g,
           w_mem_kv, w_branch, w_out, final_norm_g):
    assert x.shape[0] == 1 and mem.shape[0] == 1
    depth = w_in.shape[0]
    s = x.shape[1]
    rows = lambda v: v.astype(F32).reshape(depth, 1, -1)
    w_main, w_small = _permute_w_in(w_in)
    w_mem_b, w_branch_b, w_out_b = _bf(w_mem_kv), _bf(w_branch), _bf(w_out)
    norm_g3, mem_norm_g3 = rows(norm_g), rows(mem_norm_g)
    bfg3 = rows(jnp.pad(b_fg, ((0, 0), (0, LANES - FOX_HEADS))))
    aneg3 = rows(jnp.repeat(-jnp.exp(a_log.astype(F32)), GDN_HEAD_DIM, axis=1))
    dtb3 = rows(jnp.repeat(dt_bias, GDN_HEAD_DIM, axis=1))
    gng3 = rows(jnp.tile(gdn_norm_g, (1, GDN_HEADS)))
    bm3 = b_merge.reshape(depth, N_BRANCH, D_MODEL)
    fg_row = final_norm_g.reshape(1, D_MODEL)
    tq = min(s, ATTN_TQ)

    h = x[0]
    for l in range(depth):
        z, small = _norm_matmul(h, norm_g3, w_main, w_small, l, tm=min(s, NORM_TM), tn=NORM_TN)
        memkv, = _norm_matmul(mem[0], mem_norm_g3, w_mem_b, None, l, tm=mem.shape[1], tn=2 * WIDTH)
        qa, ka, va, stats = _attn_prep(z, small, bfg3, l, tm=ATTN_PREP_TILE)
        o_a = _attention(_skip_starts(stats, ATTN_PREP_TILE, tq), qa, ka, va, tq=tq)
        qd, kd, u, w, aqk, egl = _gdn_prep(z, small, conv_w, aneg3, dtb3, l, rows=GDN_PREP_ROWS)
        o_b = _gdn_scan(qd, kd, u, w, aqk, egl)
        h = _merge(h, o_a, o_b, z, memkv, gng3, bm3, w_branch_b, w_out_b, fg_row, l,
                   tm=MERGE_TM, final_norm=(l == depth - 1))
    return h[None]
```

```python
import functools

import jax
import jax.numpy as jnp
import numpy as np
from jax import lax
from jax.experimental import pallas as pl
from jax.experimental.pallas import tpu as pltpu

F32 = jnp.float32
BF16 = jnp.bfloat16

D_MODEL = 1024
EPS = 1e-6
FOX_HEADS = 8
FOX_HEAD_DIM = 64
GDN_HEADS = 4
GDN_HEAD_DIM = 128
MEM_HEADS = 4
MEM_HEAD_DIM = 128
WIDTH = 512
N_BRANCH = 3
CHUNK = 64
CONV_K = 4
LANES = 128
BF16_SUBLANES = 16
GDN_TILE = 256
CHUNKS_PER_TILE = GDN_TILE // CHUNK

COL_BQKV = 0
COL_AQ = 1536
COL_AZ = 3584
COL_GATES = 5120
N_MAIN = 8192
SMALL_BA = 8
SMALL_BB = 12

NORM_TM = 1024
NORM_TN = 2048
ATTN_PREP_ROWS = 1024
ATTN_STAT_TILE = 256
ATTN_TQ = 512
GDN_PREP_ROWS = 512
MERGE_TM = 512
MERGE_SUB = 256

SKIP_LOG_MARGIN = 100.0
NORM_SLACK = 1.01

VMEM_LIMIT = 56 * 1024 * 1024


def _cparams(sem):
    return pltpu.CompilerParams(dimension_semantics=sem, vmem_limit_bytes=VMEM_LIMIT)


def _bf(x):
    return x.astype(BF16)


def _dot(a, b):
    return jnp.dot(_bf(a), _bf(b), preferred_element_type=F32)


def _dot_nt(a, b):
    return lax.dot_general(_bf(a), _bf(b), (((1,), (1,)), ((), ())), preferred_element_type=F32)


def _sigmoid(x):
    return 1.0 / (1.0 + jnp.exp(-x))


def _silu(x):
    return x * _sigmoid(x)


def _split3(x):
    hi = _bf(x).astype(F32)
    r1 = x - hi
    mid = _bf(r1).astype(F32)
    lo = _bf(r1 - mid).astype(F32)
    return hi, mid, lo


def _dot_exact_lhs(mat01, x):
    hi, mid, lo = _split3(x)
    return (jnp.dot(mat01, _bf(hi), preferred_element_type=F32)
            + jnp.dot(mat01, _bf(mid), preferred_element_type=F32)
            + jnp.dot(mat01, _bf(lo), preferred_element_type=F32))


def _layer_spec(shape, layer, n_grid):
    zeros = (0,) * len(shape)
    if n_grid == 1:
        return pl.BlockSpec((None,) + tuple(shape), lambda i: (layer,) + zeros)
    return pl.BlockSpec((None,) + tuple(shape), lambda i, j: (layer,) + zeros)


def _norm_matmul_kernel(x_ref, g_ref, w_ref, *rest, with_small):
    if with_small:
        ws_ref, o_ref, os_ref, h_ref = rest
    else:
        o_ref, h_ref = rest

    @pl.when(pl.program_id(1) == 0)
    def _():
        x = x_ref[...]
        y = x * lax.rsqrt(jnp.mean(x * x, axis=-1, keepdims=True) + EPS)
        h = _bf(y * g_ref[...])
        h_ref[...] = h
        if with_small:
            os_ref[...] = jnp.dot(h, ws_ref[...], preferred_element_type=F32)

    o_ref[...] = jnp.dot(h_ref[...], w_ref[...], preferred_element_type=F32).astype(o_ref.dtype)


def _norm_matmul(x, g, w, w_small, layer, tm, tn):
    s, d = x.shape
    n = w.shape[2]
    with_small = w_small is not None
    in_specs = [pl.BlockSpec((tm, d), lambda i, j: (i, 0)),
                _layer_spec((1, d), layer, 2),
                pl.BlockSpec((None, d, tn), lambda i, j: (layer, 0, j))]
    out_specs = [pl.BlockSpec((tm, tn), lambda i, j: (i, j))]
    out_shape = [jax.ShapeDtypeStruct((s, n), BF16)]
    args = [x, g, w]
    if with_small:
        in_specs.append(_layer_spec((d, LANES), layer, 2))
        out_specs.append(pl.BlockSpec((tm, LANES), lambda i, j: (i, 0)))
        out_shape.append(jax.ShapeDtypeStruct((s, LANES), F32))
        args.append(w_small)
    return pl.pallas_call(
        functools.partial(_norm_matmul_kernel, with_small=with_small),
        grid=(s // tm, n // tn),
        in_specs=in_specs, out_specs=out_specs, out_shape=out_shape,
        scratch_shapes=[pltpu.VMEM((tm, d), BF16)],
        compiler_params=_cparams(("parallel", "arbitrary")),
        name="norm_matmul",
    )(*args)


FL_HI, FL_MID, FL_LO, FL_ONE = 0, 8, 16, 24
ST_FFIRST, ST_FLAST, ST_QN2, ST_KN2, ST_ROWS = 0, 1, 2, 6, 16


def _aug_placement():
    d = FOX_HEAD_DIM
    wq, wk, wv = (np.zeros((FOX_HEADS // 2, 2 * LANES, 2 * LANES), np.float32) for _ in range(3))
    for p in range(FOX_HEADS // 2):
        for half in range(2):
            h = 2 * p + half
            feat = half * LANES + half * d
            aux = half * LANES + (1 - half) * d
            for w in (wq, wk, wv):
                w[p, half * d + np.arange(d), feat + np.arange(d)] = 1.0
            for a, src in enumerate((FL_HI, FL_MID, FL_LO)):
                wq[p, LANES + src + h, aux + a] = 1.0
                wk[p, LANES + src + h, aux + 3 + a] = -1.0
            wq[p, LANES + FL_ONE, aux + 3:aux + 6] = 1.0
            wk[p, LANES + FL_ONE, aux:aux + 3] = 1.0
            wv[p, LANES + FL_ONE, aux:aux + d] = 1.0
    return [jnp.asarray(w, BF16) for w in (wq, wk, wv)]


def _attn_prep_kernel(q_ref, k_ref, v_ref, small_ref, bfg_ref, tril_ref, wq_ref, wk_ref, wv_ref,
                      nrm_ref, qa_ref, ka_ref, va_ref, stats_ref, carry_ref):
    rows = q_ref.shape[0]
    t = ATTN_STAT_TILE
    subs = [slice(r0, r0 + t) for r0 in range(0, rows, t)]

    @pl.when(pl.program_id(0) == 0)
    def _():
        carry_ref[...] = jnp.zeros_like(carry_ref)

    af = small_ref[...] + bfg_ref[...]
    logf = jnp.minimum(af, 0.0) - jnp.log1p(jnp.exp(-jnp.abs(af)))
    offset = carry_ref[...]
    cums = []
    for rs in subs:
        cums.append(_dot_exact_lhs(tril_ref[...], logf[rs, :]) + offset)
        offset = cums[-1][t - 1:t, :]
    carry_ref[...] = offset
    cum = jnp.concatenate(cums, axis=0)

    lane = lax.broadcasted_iota(jnp.int32, (rows, LANES), 1)
    hi, mid, lo = _split3(cum)
    fl = jnp.where(lane < FL_MID, hi,
                   jnp.where(lane < FL_LO, pltpu.roll(mid, FL_MID, 1),
                             jnp.where(lane < FL_ONE, pltpu.roll(lo, FL_LO, 1),
                                       jnp.where(lane == FL_ONE, 1.0, 0.0))))
    fl = _bf(fl)

    row = lax.broadcasted_iota(jnp.int32, (ST_ROWS, LANES), 0)
    put = lambda st, r, vec: jnp.where(row == r, jnp.broadcast_to(vec, (ST_ROWS, LANES)), st)
    stats = [put(put(jnp.zeros((ST_ROWS, LANES), F32), ST_FFIRST, c[0:1, :]),
                 ST_FLAST, c[t - 1:t, :]) for c in cums]
    for p in range(FOX_HEADS // 2):
        cols = slice(p * LANES, (p + 1) * LANES)
        for x_ref, w_ref, out_ref in ((q_ref, wq_ref, qa_ref), (k_ref, wk_ref, ka_ref),
                                      (v_ref, wv_ref, va_ref)):
            aug = jnp.dot(jnp.concatenate([x_ref[:, cols], fl], axis=1), w_ref[p],
                          preferred_element_type=F32)
            out_ref[2 * p] = _bf(aug[:, :LANES])
            out_ref[2 * p + 1] = _bf(aug[:, LANES:])
        xq, xk = q_ref[:, cols], k_ref[:, cols]
        n2 = jnp.dot(jnp.concatenate([xq * xq, xk * xk], axis=1), nrm_ref[...],
                     preferred_element_type=F32)
        for k, rs in enumerate(subs):
            top = jnp.max(n2[rs, :], axis=0, keepdims=True)
            stats[k] = put(put(stats[k], ST_QN2 + p, top[:, :LANES]), ST_KN2 + p, top[:, LANES:])
    for k in range(len(subs)):
        stats_ref[k] = stats[k]


def _attn_prep(z, small, bfg, layer, tm):
    s = z.shape[0]
    t = ATTN_STAT_TILE
    tril = jnp.asarray(np.tril(np.ones((t, t), np.float32)), BF16)
    half = np.arange(2 * LANES) // FOX_HEAD_DIM
    nrm = jnp.asarray((half[:, None] == half[None, :]).astype(np.float32), BF16)
    places = _aug_placement()
    cb = COL_AQ // WIDTH
    out_sds = jax.ShapeDtypeStruct((FOX_HEADS, s, LANES), BF16)
    out_spec = pl.BlockSpec((FOX_HEADS, tm, LANES), lambda i: (0, i, 0))
    const = lambda a: pl.BlockSpec(a.shape, lambda i: (0,) * a.ndim)
    return pl.pallas_call(
        _attn_prep_kernel,
        grid=(s // tm,),
        in_specs=[pl.BlockSpec((tm, WIDTH), lambda i: (i, cb)),
                  pl.BlockSpec((tm, WIDTH), lambda i: (i, cb + 1)),
                  pl.BlockSpec((tm, WIDTH), lambda i: (i, cb + 2)),
                  pl.BlockSpec((tm, LANES), lambda i: (i, 0)),
                  _layer_spec((1, LANES), layer, 1),
                  const(tril)] + [const(w) for w in places] + [const(nrm)],
        out_specs=[out_spec, out_spec, out_spec,
                   pl.BlockSpec((tm // t, ST_ROWS, LANES), lambda i: (i, 0, 0))],
        out_shape=[out_sds, out_sds, out_sds,
                   jax.ShapeDtypeStruct((s // t, ST_ROWS, LANES), F32)],
        scratch_shapes=[pltpu.VMEM((1, LANES), F32)],
        compiler_params=_cparams(("arbitrary",)),
        name="attn_prep",
    )(z, z, z, small, bfg, tril, *places, nrm)


def _attn_kernel(jlo_ref, q_ref, k_ref, v_ref, o_ref, s0_ref, s1_ref, s2_ref, s3_ref, *, tq, tk):
    pair = pl.program_id(0)
    i = pl.program_id(1)
    lane = lax.broadcasted_iota(jnp.int32, (tq, LANES), 1)
    n_tile = tk // LANES

    def logits(q, hh, j, s_ref):
        start = pl.multiple_of(j * tk, tk)
        s_ref[...] = lax.dot_general(q, k_ref[hh, pl.ds(start, tk), :], (((1,), (1,)), ((), ())),
                                     preferred_element_type=F32)

    def softmax_pv(hh, j, s_ref, carry, masked):
        m, acc = carry
        start = pl.multiple_of(j * tk, tk)
        if masked:
            row = lax.broadcasted_iota(jnp.int32, (tq, tk), 0)
            col = lax.broadcasted_iota(jnp.int32, (tq, tk), 1)
            s_ref[...] = jnp.where(col <= row, s_ref[...], -jnp.inf)
        m_new = jnp.maximum(m, jnp.max(s_ref[...], axis=1, keepdims=True))
        p = jnp.exp(s_ref[...] - jnp.tile(m_new, (1, n_tile)))
        alpha = jnp.exp(m - m_new)
        acc = alpha * acc + jnp.dot(_bf(p), v_ref[hh, pl.ds(start, tk), :],
                                    preferred_element_type=F32)
        return m_new, acc

    bufs = ((s0_ref, s1_ref), (s2_ref, s3_ref))
    jlos = [jlo_ref[(2 * pair + hh) * pl.num_programs(1) + i] for hh in range(2)]
    logits(q_ref[0], 0, jlos[0], bufs[0][0])
    accs = []
    for hh in range(2):
        q = q_ref[hh]
        jlo = jlos[hh]
        cur, nxt = bufs[hh]
        n_full = i - jlo

        def two_blocks(t, carry, q=q, hh=hh, jlo=jlo, cur=cur, nxt=nxt):
            j = jlo + 2 * t
            logits(q, hh, j + 1, nxt)
            carry = softmax_pv(hh, j, cur, carry, False)
            logits(q, hh, j + 2, cur)
            return softmax_pv(hh, j + 1, nxt, carry, False)

        def start_other_head(hh=hh):
            if hh == 0:
                logits(q_ref[1], 1, jlos[1], bufs[1][0])

        def tail_odd(carry, q=q, hh=hh, cur=cur, nxt=nxt, start=start_other_head):
            logits(q, hh, i, nxt)
            start()
            carry = softmax_pv(hh, i - 1, cur, carry, False)
            return softmax_pv(hh, i, nxt, carry, True)

        def tail_even(carry, hh=hh, cur=cur, start=start_other_head):
            start()
            return softmax_pv(hh, i, cur, carry, True)

        carry = (jnp.full((tq, LANES), -jnp.inf, F32), jnp.zeros((tq, LANES), F32))
        carry = lax.fori_loop(0, n_full // 2, two_blocks, carry)
        _, acc = lax.cond(n_full % 2 == 1, tail_odd, tail_even, carry)
        accs.append(acc)
    o0 = accs[0] / pltpu.roll(accs[0], FOX_HEAD_DIM, 1)
    o1 = accs[1] / pltpu.roll(accs[1], FOX_HEAD_DIM, 1)
    o_ref[...] = jnp.where(lane < FOX_HEAD_DIM, o0, o1)


def _skip_starts(stats, tm, tq):
    r = tq // tm
    nq = stats.shape[0] // r
    st = stats.reshape(nq, r, ST_ROWS, LANES)
    per_head = lambda row0: jnp.stack(
        [st[:, :, row0 + h // 2, FOX_HEAD_DIM * (h % 2)] for h in range(FOX_HEADS)], axis=-1)
    f_first = st[:, 0, ST_FFIRST, :FOX_HEADS].T
    f_last = st[:, r - 1, ST_FLAST, :FOX_HEADS].T
    qn = jnp.sqrt(jnp.max(per_head(ST_QN2), axis=1)).T
    kn = jnp.sqrt(jnp.max(per_head(ST_KN2), axis=(0, 1)))[:, None]
    thresh = f_first + 2.0 * NORM_SLACK * qn * kn + SKIP_LOG_MARGIN
    need = f_last[:, None, :] <= thresh[:, :, None]
    idx = jnp.arange(nq)
    first = jnp.min(jnp.where(need, idx[None, None, :], nq), axis=-1)
    return jnp.minimum(first, idx[None, :]).astype(jnp.int32).reshape(-1)


def _attention(jlo, qa, ka, va, tq):
    s = qa.shape[1]
    tk = tq
    kv_spec = pl.BlockSpec((2, s, LANES), lambda p, i, jl: (p, 0, 0), pipeline_mode=pl.Buffered(1))
    return pl.pallas_call(
        functools.partial(_attn_kernel, tq=tq, tk=tk),
        grid_spec=pltpu.PrefetchScalarGridSpec(
            num_scalar_prefetch=1,
            grid=(FOX_HEADS // 2, s // tq),
            in_specs=[pl.BlockSpec((2, tq, LANES), lambda p, i, jl: (p, i, 0)), kv_spec, kv_spec],
            out_specs=pl.BlockSpec((tq, LANES), lambda p, i, jl: (i, p)),
            scratch_shapes=[pltpu.VMEM((tq, tk), F32)] * 4),
        out_shape=jax.ShapeDtypeStruct((s, WIDTH), F32),
        compiler_params=_cparams(("parallel", "arbitrary")),
        name="fox_attention",
    )(jlo, qa, ka, va)


def _chunk_cumsum_matrix():
    r = np.arange(GDN_TILE)
    same = (r[:, None] // CHUNK) == (r[None, :] // CHUNK)
    return jnp.asarray((same & (r[None, :] <= r[:, None])).astype(np.float32), BF16)


def _gdn_prep_kernel(x_ref, halo_ref, small_ref, cw_ref, aneg_ref, dtb_ref, kl_ref,
                     qd_ref, kd_ref, u_ref, w_ref, aqk_ref, egl_ref, xx_ref):
    t = GDN_TILE
    hd = GDN_HEAD_DIM
    pad = BF16_SUBLANES
    rows_in = x_ref.shape[0]

    halo = halo_ref[...].astype(F32)
    xx_ref[0:pad, :] = jnp.where(pl.program_id(0) == 0, jnp.zeros_like(halo), halo)
    xx_ref[pad:pad + rows_in, :] = x_ref[...].astype(F32)

    row = lax.broadcasted_iota(jnp.int32, (t, t), 0)
    col = lax.broadcasted_iota(jnp.int32, (t, t), 1)
    same = (row // CHUNK) == (col // CHUNK)
    incl = jnp.logical_and(same, col <= row)
    strict = jnp.logical_and(same, col < row)
    eye = (row == col).astype(F32)
    row8 = lax.broadcasted_iota(jnp.int32, (8, WIDTH), 0)

    invs, bps, rhss, dests = [], [], [], []
    for tile in range(rows_in // t):
        r0 = tile * t
        rs = slice(r0, r0 + t)
        conv = cw_ref[CONV_K - 1:CONV_K, :] * xx_ref[pad + r0:pad + r0 + t, :]
        for j in range(CONV_K - 1):
            off = pad + r0 - (CONV_K - 1) + j
            conv = conv + cw_ref[j:j + 1, :] * xx_ref[off:off + t, :]
        qkv = _silu(conv)

        small = small_ref[rs, :]
        bcast = lambda c0: jnp.concatenate(
            [jnp.broadcast_to(small[:, c0 + h:c0 + h + 1], (t, hd)) for h in range(GDN_HEADS)],
            axis=1)
        sp_in = bcast(SMALL_BA) + dtb_ref[...]
        softplus = jnp.maximum(sp_in, 0.0) + jnp.log1p(jnp.exp(-jnp.abs(sp_in)))
        g = aneg_ref[...] * softplus
        beta = _sigmoid(bcast(SMALL_BB))
        gcum = _dot_exact_lhs(kl_ref[...], g)
        glast = jnp.concatenate(
            [jnp.broadcast_to(gcum[(c + 1) * CHUNK - 1:(c + 1) * CHUNK, :], (CHUNK, WIDTH))
             for c in range(CHUNKS_PER_TILE)], axis=0)
        eg = jnp.exp(gcum)
        egl = jnp.exp(glast)
        egd = jnp.exp(glast - gcum)
        egl_rows = jnp.zeros((8, WIDTH), F32)
        for c in range(CHUNKS_PER_TILE):
            egl_rows = jnp.where(row8 == c, egl[c * CHUNK:c * CHUNK + 8, :], egl_rows)
        egl_ref[tile * 8:(tile + 1) * 8, :] = egl_rows

        for h in range(GDN_HEADS):
            cs = slice(h * hd, (h + 1) * hd)
            qh = qkv[:, h * hd:(h + 1) * hd]
            kh = qkv[:, WIDTH + h * hd:WIDTH + (h + 1) * hd]
            vh = qkv[:, 2 * WIDTH + h * hd:2 * WIDTH + (h + 1) * hd]
            qh = qh * lax.rsqrt(jnp.sum(qh * qh, axis=-1, keepdims=True) + EPS) * (hd ** -0.5)
            kh = kh * lax.rsqrt(jnp.sum(kh * kh, axis=-1, keepdims=True) + EPS)
            bh = beta[:, cs]
            gc = gcum[:, cs]
            dmat = jnp.concatenate([gc, gc], axis=1) - gc.T[0:1, :]
            gamma = jnp.exp(jnp.where(incl, dmat, -jnp.inf))
            kb = kh * bh
            a = jnp.where(strict, _dot_nt(kb, kh) * gamma, 0.0)
            invs.append(eye - a)
            bps.append(_bf(-a))
            rhss.append(_bf(jnp.concatenate([vh * bh, kb * eg[:, cs]], axis=1)))
            aqk_ref[h, rs, :] = _bf(_dot_nt(qh, kh) * gamma)
            qd_ref[rs, cs] = _bf(qh * eg[:, cs])
            kd_ref[rs, cs] = _bf(kh * egd[:, cs])
            dests.append((rs, cs))

    for _ in range(5):
        bps = [_bf(jnp.dot(b, b, preferred_element_type=F32)) for b in bps]
        invs = [inv + jnp.dot(_bf(inv), b, preferred_element_type=F32)
                for inv, b in zip(invs, bps)]
    for inv, rhs, (rs, cs) in zip(invs, rhss, dests):
        sol = jnp.dot(_bf(inv), rhs, preferred_element_type=F32)
        u_ref[rs, cs] = sol[:, :hd]
        w_ref[rs, cs] = _bf(sol[:, hd:])


def _gdn_prep(z, small, conv_w, aneg, dtb, layer, rows):
    s = z.shape[0]
    t = GDN_TILE
    pad = BF16_SUBLANES
    tiles = rows // t
    kl = _chunk_cumsum_matrix()
    row_spec = pl.BlockSpec((rows, WIDTH), lambda i: (i, 0))
    return pl.pallas_call(
        _gdn_prep_kernel,
        grid=(s // rows,),
        in_specs=[pl.BlockSpec((rows, 3 * WIDTH), lambda i: (i, COL_BQKV // (3 * WIDTH))),
                  pl.BlockSpec((pad, 3 * WIDTH),
                               lambda i: (jnp.maximum(i * (rows // pad) - 1, 0), 0)),
                  pl.BlockSpec((rows, LANES), lambda i: (i, 0)),
                  _layer_spec((CONV_K, 3 * WIDTH), layer, 1),
                  _layer_spec((1, WIDTH), layer, 1),
                  _layer_spec((1, WIDTH), layer, 1),
                  pl.BlockSpec((t, t), lambda i: (0, 0))],
        out_specs=[row_spec, row_spec, row_spec, row_spec,
                   pl.BlockSpec((GDN_HEADS, rows, t), lambda i: (0, i, 0)),
                   pl.BlockSpec((8 * tiles, WIDTH), lambda i: (i, 0))],
        out_shape=[jax.ShapeDtypeStruct((s, WIDTH), BF16),
                   jax.ShapeDtypeStruct((s, WIDTH), BF16),
                   jax.ShapeDtypeStruct((s, WIDTH), F32),
                   jax.ShapeDtypeStruct((s, WIDTH), BF16),
                   jax.ShapeDtypeStruct((GDN_HEADS, s, t), BF16),
                   jax.ShapeDtypeStruct((s // t * 8, WIDTH), F32)],
        scratch_shapes=[pltpu.VMEM((rows + pad, 3 * WIDTH), F32)],
        compiler_params=_cparams(("parallel",)),
        name="gdn_prep",
    )(z, z, small, conv_w, aneg, dtb, kl)


def _gdn_scan_kernel(qd_ref, kd_ref, u_ref, w_ref, aqk_ref, egl_ref, o_ref, state_ref, vn_ref):
    hd = GDN_HEAD_DIM

    @pl.when(pl.program_id(0) == 0)
    def _():
        state_ref[...] = jnp.zeros_like(state_ref)
        vn_ref[...] = jnp.zeros_like(vn_ref)

    heads = range(GDN_HEADS)
    col = lambda h: slice(h * hd, (h + 1) * hd)
    for c in range(CHUNKS_PER_TILE):
        rs = slice(c * CHUNK, (c + 1) * CHUNK)
        sts = [state_ref[h] for h in heads]
        rr = [jnp.dot(jnp.concatenate([w_ref[rs, col(h)], qd_ref[rs, col(h)]], axis=0),
                      _bf(sts[h]), preferred_element_type=F32) for h in heads]
        vns = [_bf(u_ref[rs, col(h)] - rr[h][:CHUNK]) for h in heads]
        for h in heads:
            vn_ref[h, rs, :] = vns[h]
        for h in heads:
            upd = lax.dot_general(kd_ref[rs, col(h)], vns[h], (((0,), (0,)), ((), ())),
                                  preferred_element_type=F32)
            state_ref[h] = sts[h] * egl_ref[c:c + 1, col(h)] + upd
        for h in heads:
            o_ref[rs, col(h)] = rr[h][CHUNK:] + jnp.dot(aqk_ref[h, rs, :], vn_ref[h],
                                                        preferred_element_type=F32)


def _gdn_scan(qd, kd, u, w, aqk, egl):
    s = qd.shape[0]
    t = GDN_TILE
    row_spec = pl.BlockSpec((t, WIDTH), lambda i: (i, 0))
    return pl.pallas_call(
        _gdn_scan_kernel,
        grid=(s // t,),
        in_specs=[row_spec, row_spec, row_spec, row_spec,
                  pl.BlockSpec((GDN_HEADS, t, t), lambda i: (0, i, 0)),
                  pl.BlockSpec((8, WIDTH), lambda i: (i, 0))],
        out_specs=row_spec,
        out_shape=jax.ShapeDtypeStruct((s, WIDTH), F32),
        scratch_shapes=[pltpu.VMEM((GDN_HEADS, GDN_HEAD_DIM, GDN_HEAD_DIM), F32),
                        pltpu.VMEM((GDN_HEADS, t, GDN_HEAD_DIM), BF16)],
        compiler_params=_cparams(("arbitrary",)),
        name="gdn_scan",
    )(qd, kd, u, w, aqk, egl)


def _merge_kernel(x_ref, oa_ref, ob_ref, mq_ref, az_ref, bz_ref, mz_ref, g0_ref, g1_ref, g2_ref,
                  mk_ref, mv_ref, gng_ref, bm_ref, wb_ref, wo_ref, fg_ref, o_ref, *, final_norm):
    hd = GDN_HEAD_DIM
    groups = [slice(r0, r0 + MERGE_SUB) for r0 in range(0, x_ref.shape[0], MERGE_SUB)]
    g_refs = (g0_ref, g1_ref, g2_ref)

    def branch_a(rs):
        return oa_ref[rs, :] * _silu(az_ref[rs, :].astype(F32))

    def branch_b(rs):
        normed = []
        for h in range(GDN_HEADS):
            oh = ob_ref[rs, h * hd:(h + 1) * hd]
            normed.append(oh * lax.rsqrt(jnp.mean(oh * oh, axis=-1, keepdims=True) + EPS))
        return jnp.concatenate(normed, axis=1) * gng_ref[...] * _silu(bz_ref[rs, :].astype(F32))

    def branch_m(rs):
        om = []
        for h in range(MEM_HEADS):
            cs = slice(h * MEM_HEAD_DIM, (h + 1) * MEM_HEAD_DIM)
            sc = _dot_nt(mq_ref[rs, cs], mk_ref[:, cs]) * (MEM_HEAD_DIM ** -0.5)
            p = jnp.exp(sc - jnp.max(sc, axis=-1, keepdims=True))
            om.append(jnp.dot(_bf(p), mv_ref[:, cs], preferred_element_type=F32)
                      / jnp.sum(p, axis=-1, keepdims=True))
        return jnp.concatenate(om, axis=1) * _silu(mz_ref[rs, :].astype(F32))

    merged = [None] * len(groups)
    for n, branch in enumerate((branch_a, branch_b, branch_m)):
        for k, rs in enumerate(groups):
            proj = jnp.dot(_bf(branch(rs)), wb_ref[n], preferred_element_type=F32)
            term = _sigmoid(g_refs[n][rs, :].astype(F32) + bm_ref[n:n + 1, :]) * proj
            merged[k] = term if merged[k] is None else merged[k] + term
    for k, rs in enumerate(groups):
        out = x_ref[rs, :] + jnp.dot(_bf(merged[k]), wo_ref[...], preferred_element_type=F32)
        if final_norm:
            out = out * lax.rsqrt(jnp.mean(out * out, axis=-1, keepdims=True) + EPS) * fg_ref[...]
        o_ref[rs, :] = out


def _merge(x, o_a, o_b, z, memkv, gng, bm, wb, wo, fg_row, layer, tm, final_norm):
    s = x.shape[0]
    zc = lambda col, width: pl.BlockSpec((tm, width), lambda i: (i, col // width))
    n_mem = memkv.shape[0]
    return pl.pallas_call(
        functools.partial(_merge_kernel, final_norm=final_norm),
        grid=(s // tm,),
        in_specs=[pl.BlockSpec((tm, D_MODEL), lambda i: (i, 0)),
                  pl.BlockSpec((tm, WIDTH), lambda i: (i, 0)),
                  pl.BlockSpec((tm, WIDTH), lambda i: (i, 0)),
                  zc(COL_AQ + 3 * WIDTH, WIDTH),
                  zc(COL_AZ, WIDTH), zc(COL_AZ + WIDTH, WIDTH), zc(COL_AZ + 2 * WIDTH, WIDTH),
                  zc(COL_GATES, D_MODEL), zc(COL_GATES + D_MODEL, D_MODEL),
                  zc(COL_GATES + 2 * D_MODEL, D_MODEL),
                  pl.BlockSpec((n_mem, WIDTH), lambda i: (0, 0)),
                  pl.BlockSpec((n_mem, WIDTH), lambda i: (0, 1)),
                  _layer_spec((1, WIDTH), layer, 1),
                  _layer_spec((N_BRANCH, D_MODEL), layer, 1),
                  _layer_spec((N_BRANCH, WIDTH, D_MODEL), layer, 1),
                  _layer_spec((D_MODEL, D_MODEL), layer, 1),
                  pl.BlockSpec((1, D_MODEL), lambda i: (0, 0))],
        out_specs=pl.BlockSpec((tm, D_MODEL), lambda i: (i, 0)),
        out_shape=jax.ShapeDtypeStruct((s, D_MODEL), F32),
        compiler_params=_cparams(("parallel",)),
        name="merge",
    )(x, o_a, o_b, z, z, z, z, z, z, z, memkv, memkv, gng, bm, wb, wo, fg_row)


def _permute_w_in(w_in):
    w = WIDTH
    sizes = (w, w, w, FOX_HEADS, w, w, w, w, GDN_HEADS, GDN_HEADS, w, w, w, N_BRANCH * D_MODEL)
    offs = np.concatenate([[0], np.cumsum(sizes)])
    wb = _bf(w_in)
    part = lambda k: wb[..., offs[k]:offs[k + 1]]
    aq, ak, av, af, az, bq, bk, bv, ba, bb, bz, mq, mz, gates = (part(k) for k in range(14))
    aq = aq * jnp.asarray(FOX_HEAD_DIM ** -0.5, BF16)
    main = [bq, bk, bv, aq, ak, av, mq, az, bz, mz, gates]
    pad = jnp.zeros(w_in.shape[:-1] + (LANES - FOX_HEADS - 2 * GDN_HEADS,), BF16)
    return jnp.concatenate(main, axis=-1), jnp.concatenate([af, ba, bb, pad], axis=-1)


def kernel(x, mem, norm_g, w_in, b_fg, b_merge, conv_w, a_log, dt_bias, gdn_norm_g, mem_norm_g,
           w_mem_kv, w_branch, w_out, final_norm_g):
    assert x.shape[0] == 1 and mem.shape[0] == 1
    depth = w_in.shape[0]
    s = x.shape[1]
    rows = lambda v: v.astype(F32).reshape(depth, 1, -1)
    w_main, w_small = _permute_w_in(w_in)
    w_mem_b, w_branch_b, w_out_b = _bf(w_mem_kv), _bf(w_branch), _bf(w_out)
    norm_g3, mem_norm_g3 = rows(norm_g), rows(mem_norm_g)
    bfg3 = rows(jnp.pad(b_fg, ((0, 0), (0, LANES - FOX_HEADS))))
    aneg3 = rows(jnp.repeat(-jnp.exp(a_log.astype(F32)), GDN_HEAD_DIM, axis=1))
    dtb3 = rows(jnp.repeat(dt_bias, GDN_HEAD_DIM, axis=1))
    gng3 = rows(jnp.tile(gdn_norm_g, (1, GDN_HEADS)))
    bm3 = b_merge.reshape(depth, N_BRANCH, D_MODEL)
    fg_row = final_norm_g.reshape(1, D_MODEL)
    tq = min(s, ATTN_TQ)

    h = x[0]
    for l in range(depth):
        z, small = _norm_matmul(h, norm_g3, w_main, w_small, l, tm=min(s, NORM_TM), tn=NORM_TN)
        memkv, = _norm_matmul(mem[0], mem_norm_g3, w_mem_b, None, l, tm=mem.shape[1], tn=2 * WIDTH)
        qa, ka, va, stats = _attn_prep(z, small, bfg3, l, tm=min(s, ATTN_PREP_ROWS))
        o_a = _attention(_skip_starts(stats, ATTN_STAT_TILE, tq), qa, ka, va, tq=tq)
        qd, kd, u, w, aqk, egl = _gdn_prep(z, small, conv_w, aneg3, dtb3, l, rows=GDN_PREP_ROWS)
        o_b = _gdn_scan(qd, kd, u, w, aqk, egl)
        h = _merge(h, o_a, o_b, z, memkv, gng3, bm3, w_branch_b, w_out_b, fg_row, l,
                   tm=MERGE_TM, final_norm=(l == depth - 1))
    return h[None]
```

```python
import functools

import jax
import jax.numpy as jnp
import numpy as np
from jax import lax
from jax.experimental import pallas as pl
from jax.experimental.pallas import tpu as pltpu

F32 = jnp.float32
BF16 = jnp.bfloat16

D_MODEL = 1024
EPS = 1e-6
FOX_HEADS = 8
FOX_HEAD_DIM = 64
GDN_HEADS = 4
GDN_HEAD_DIM = 128
MEM_HEADS = 4
MEM_HEAD_DIM = 128
WIDTH = 512
N_BRANCH = 3
CHUNK = 64
CONV_K = 4
LANES = 128
BF16_SUBLANES = 16
GDN_TILE = 256
CHUNKS_PER_TILE = GDN_TILE // CHUNK

SRC_AF = 3 * WIDTH
SRC_BA = SRC_AF + FOX_HEADS + 4 * WIDTH
N_SRC = 8208
COL_AQ, COL_AK, COL_AV, COL_AZ = 0, 512, 1024, 1536
COL_BQ, COL_BK, COL_BV, COL_BZ = 2048, 2560, 3072, 3584
COL_MQ, COL_MZ, COL_GATES = 4096, 4608, 5120
N_MAIN = 8192
SMALL_BA = 8
SMALL_BB = 12

WEIGHT_PREP_ROWS = 128
NORM_TM = 1024
NORM_TN = 2048
ATTN_PREP_ROWS = 1024
ATTN_STAT_TILE = 256
ATTN_TQ = 512
GDN_PREP_ROWS = 512
MERGE_TM = 512
MERGE_SUB = 256

SKIP_LOG_MARGIN = 100.0
NORM_SLACK = 1.01

VMEM_LIMIT = 56 * 1024 * 1024


def _cparams(sem):
    return pltpu.CompilerParams(dimension_semantics=sem, vmem_limit_bytes=VMEM_LIMIT)


def _bf(x):
    return x.astype(BF16)


def _dot(a, b):
    return jnp.dot(_bf(a), _bf(b), preferred_element_type=F32)


def _dot_nt(a, b):
    return lax.dot_general(_bf(a), _bf(b), (((1,), (1,)), ((), ())), preferred_element_type=F32)


def _sigmoid(x):
    return 1.0 / (1.0 + jnp.exp(-x))


def _silu(x):
    return x * _sigmoid(x)


def _split3(x):
    hi = _bf(x).astype(F32)
    r1 = x - hi
    mid = _bf(r1).astype(F32)
    lo = _bf(r1 - mid).astype(F32)
    return hi, mid, lo


def _dot_exact_lhs(mat01, x):
    hi, mid, lo = _split3(x)
    return (jnp.dot(mat01, _bf(hi), preferred_element_type=F32)
            + jnp.dot(mat01, _bf(mid), preferred_element_type=F32)
            + jnp.dot(mat01, _bf(lo), preferred_element_type=F32))


def _layer_spec(shape, layer, n_grid):
    zeros = (0,) * len(shape)
    if n_grid == 1:
        return pl.BlockSpec((None,) + tuple(shape), lambda i: (layer,) + zeros)
    return pl.BlockSpec((None,) + tuple(shape), lambda i, j: (layer,) + zeros)


def _norm_matmul_kernel(x_ref, g_ref, w_ref, *rest, with_small):
    if with_small:
        ws_ref, o_ref, os_ref, h_ref = rest
    else:
        o_ref, h_ref = rest

    @pl.when(pl.program_id(1) == 0)
    def _():
        x = x_ref[...]
        y = x * lax.rsqrt(jnp.mean(x * x, axis=-1, keepdims=True) + EPS)
        h = _bf(y * g_ref[...])
        h_ref[...] = h
        if with_small:
            os_ref[...] = jnp.dot(h, ws_ref[...], preferred_element_type=F32)

    o_ref[...] = jnp.dot(h_ref[...], w_ref[...], preferred_element_type=F32).astype(o_ref.dtype)


def _norm_matmul(x, g, w, w_small, layer, tm, tn):
    s, d = x.shape
    n = w.shape[2]
    with_small = w_small is not None
    in_specs = [pl.BlockSpec((tm, d), lambda i, j: (i, 0)),
                _layer_spec((1, d), layer, 2),
                pl.BlockSpec((None, d, tn), lambda i, j: (layer, 0, j))]
    out_specs = [pl.BlockSpec((tm, tn), lambda i, j: (i, j))]
    out_shape = [jax.ShapeDtypeStruct((s, n), BF16)]
    args = [x, g, w]
    if with_small:
        in_specs.append(_layer_spec((d, LANES), layer, 2))
        out_specs.append(pl.BlockSpec((tm, LANES), lambda i, j: (i, 0)))
        out_shape.append(jax.ShapeDtypeStruct((s, LANES), F32))
        args.append(w_small)
    return pl.pallas_call(
        functools.partial(_norm_matmul_kernel, with_small=with_small),
        grid=(s // tm, n // tn),
        in_specs=in_specs, out_specs=out_specs, out_shape=out_shape,
        scratch_shapes=[pltpu.VMEM((tm, d), BF16)],
        compiler_params=_cparams(("parallel", "arbitrary")),
        name="norm_matmul",
    )(*args)


FL_HI, FL_MID, FL_LO, FL_ONE = 0, 8, 16, 24
ST_FFIRST, ST_FLAST, ST_QN2, ST_KN2, ST_DMIN, ST_ROWS = 0, 1, 2, 6, 10, 16


def _aug_placement():
    d = FOX_HEAD_DIM
    wq, wk, wv = (np.zeros((FOX_HEADS // 2, 2 * LANES, 2 * LANES), np.float32) for _ in range(3))
    for p in range(FOX_HEADS // 2):
        for half in range(2):
            h = 2 * p + half
            feat = half * LANES + half * d
            aux = half * LANES + (1 - half) * d
            for w, scale in ((wq, d ** -0.5), (wk, 1.0), (wv, 1.0)):
                w[p, half * d + np.arange(d), feat + np.arange(d)] = scale
            for a, src in enumerate((FL_HI, FL_MID, FL_LO)):
                wq[p, LANES + src + h, aux + a] = 1.0
                wk[p, LANES + src + h, aux + 3 + a] = -1.0
            wq[p, LANES + FL_ONE, aux + 3:aux + 6] = 1.0
            wk[p, LANES + FL_ONE, aux:aux + 3] = 1.0
            wv[p, LANES + FL_ONE, aux:aux + d] = 1.0
    return [jnp.asarray(w, BF16) for w in (wq, wk, wv)]


def _attn_prep_kernel(q_ref, k_ref, v_ref, small_ref, bfg_ref, tril_ref, wq_ref, wk_ref, wv_ref,
                      nrm_ref, qa_ref, ka_ref, va_ref, stats_ref, carry_ref):
    rows = q_ref.shape[0]
    t = ATTN_STAT_TILE
    subs = [slice(r0, r0 + t) for r0 in range(0, rows, t)]

    @pl.when(pl.program_id(0) == 0)
    def _():
        carry_ref[...] = jnp.zeros_like(carry_ref)

    af = small_ref[...] + bfg_ref[...]
    logf = jnp.minimum(af, 0.0) - jnp.log1p(jnp.exp(-jnp.abs(af)))
    offset = carry_ref[...]
    cums = []
    for rs in subs:
        cums.append(_dot_exact_lhs(tril_ref[...], logf[rs, :]) + offset)
        offset = cums[-1][t - 1:t, :]
    carry_ref[...] = offset
    cum = jnp.concatenate(cums, axis=0)

    lane = lax.broadcasted_iota(jnp.int32, (rows, LANES), 1)
    hi, mid, lo = _split3(cum)
    fl = jnp.where(lane < FL_MID, hi,
                   jnp.where(lane < FL_LO, pltpu.roll(mid, FL_MID, 1),
                             jnp.where(lane < FL_ONE, pltpu.roll(lo, FL_LO, 1),
                                       jnp.where(lane == FL_ONE, 1.0, 0.0))))
    fl = _bf(fl)

    row = lax.broadcasted_iota(jnp.int32, (ST_ROWS, LANES), 0)
    put = lambda st, r, vec: jnp.where(row == r, jnp.broadcast_to(vec, (ST_ROWS, LANES)), st)
    stats = [put(put(jnp.zeros((ST_ROWS, LANES), F32), ST_FFIRST, c[0:1, :]),
                 ST_FLAST, c[t - 1:t, :]) for c in cums]
    for p in range(FOX_HEADS // 2):
        cols = slice(p * LANES, (p + 1) * LANES)
        for x_ref, w_ref, out_ref in ((q_ref, wq_ref, qa_ref), (k_ref, wk_ref, ka_ref),
                                      (v_ref, wv_ref, va_ref)):
            aug = jnp.dot(jnp.concatenate([x_ref[:, cols], fl], axis=1), w_ref[p],
                          preferred_element_type=F32)
            out_ref[2 * p] = _bf(aug[:, :LANES])
            out_ref[2 * p + 1] = _bf(aug[:, LANES:])
        xq, xk = q_ref[:, cols], k_ref[:, cols]
        n2 = jnp.dot(jnp.concatenate([xq * xq, xk * xk], axis=1), nrm_ref[...],
                     preferred_element_type=F32)
        diag = jnp.dot(xq * xk, nrm_ref[:LANES, :LANES], preferred_element_type=F32)
        for k, rs in enumerate(subs):
            top = jnp.max(n2[rs, :], axis=0, keepdims=True)
            stats[k] = put(put(stats[k], ST_QN2 + p, top[:, :LANES]), ST_KN2 + p, top[:, LANES:])
            stats[k] = put(stats[k], ST_DMIN + p, jnp.min(diag[rs, :], axis=0, keepdims=True))
    for k in range(len(subs)):
        stats_ref[k] = stats[k]


def _attn_prep(z, small, bfg, layer, tm):
    s = z.shape[0]
    t = ATTN_STAT_TILE
    tril = jnp.asarray(np.tril(np.ones((t, t), np.float32)), BF16)
    half = np.arange(2 * LANES) // FOX_HEAD_DIM
    nrm = jnp.asarray((half[:, None] == half[None, :]).astype(np.float32), BF16)
    places = _aug_placement()
    out_sds = jax.ShapeDtypeStruct((FOX_HEADS, s, LANES), BF16)
    out_spec = pl.BlockSpec((FOX_HEADS, tm, LANES), lambda i: (0, i, 0))
    const = lambda a: pl.BlockSpec(a.shape, lambda i: (0,) * a.ndim)
    return pl.pallas_call(
        _attn_prep_kernel,
        grid=(s // tm,),
        in_specs=[pl.BlockSpec((tm, WIDTH), lambda i: (i, COL_AQ // WIDTH)),
                  pl.BlockSpec((tm, WIDTH), lambda i: (i, COL_AK // WIDTH)),
                  pl.BlockSpec((tm, WIDTH), lambda i: (i, COL_AV // WIDTH)),
                  pl.BlockSpec((tm, LANES), lambda i: (i, 0)),
                  _layer_spec((1, LANES), layer, 1),
                  const(tril)] + [const(w) for w in places] + [const(nrm)],
        out_specs=[out_spec, out_spec, out_spec,
                   pl.BlockSpec((tm // t, ST_ROWS, LANES), lambda i: (i, 0, 0))],
        out_shape=[out_sds, out_sds, out_sds,
                   jax.ShapeDtypeStruct((s // t, ST_ROWS, LANES), F32)],
        scratch_shapes=[pltpu.VMEM((1, LANES), F32)],
        compiler_params=_cparams(("arbitrary",)),
        name="attn_prep",
    )(z, z, z, small, bfg, tril, *places, nrm)


def _attn_kernel(jlo_ref, q_ref, k_ref, v_ref, o_ref, s0_ref, s1_ref, s2_ref, s3_ref, *, tq, tk):
    pair = pl.program_id(0)
    i = pl.program_id(1)
    lane = lax.broadcasted_iota(jnp.int32, (tq, LANES), 1)
    n_tile = tk // LANES

    def logits(q, hh, j, s_ref):
        start = pl.multiple_of(j * tk, tk)
        s_ref[...] = lax.dot_general(q, k_ref[hh, pl.ds(start, tk), :], (((1,), (1,)), ((), ())),
                                     preferred_element_type=F32)

    def softmax_pv(hh, j, s_ref, carry, masked):
        m, acc = carry
        start = pl.multiple_of(j * tk, tk)
        if masked:
            row = lax.broadcasted_iota(jnp.int32, (tq, tk), 0)
            col = lax.broadcasted_iota(jnp.int32, (tq, tk), 1)
            s_ref[...] = jnp.where(col <= row, s_ref[...], -jnp.inf)
        m_new = jnp.maximum(m, jnp.max(s_ref[...], axis=1, keepdims=True))
        p = jnp.exp(s_ref[...] - jnp.tile(m_new, (1, n_tile)))
        alpha = jnp.exp(m - m_new)
        acc = alpha * acc + jnp.dot(_bf(p), v_ref[hh, pl.ds(start, tk), :],
                                    preferred_element_type=F32)
        return m_new, acc

    bufs = ((s0_ref, s1_ref), (s2_ref, s3_ref))
    jlos = [jlo_ref[(2 * pair + hh) * pl.num_programs(1) + i] for hh in range(2)]
    logits(q_ref[0], 0, jlos[0], bufs[0][0])
    accs = []
    for hh in range(2):
        q = q_ref[hh]
        jlo = jlos[hh]
        cur, nxt = bufs[hh]
        n_full = i - jlo

        def two_blocks(t, carry, q=q, hh=hh, jlo=jlo, cur=cur, nxt=nxt):
            j = jlo + 2 * t
            logits(q, hh, j + 1, nxt)
            carry = softmax_pv(hh, j, cur, carry, False)
            logits(q, hh, j + 2, cur)
            return softmax_pv(hh, j + 1, nxt, carry, False)

        def start_other_head(hh=hh):
            if hh == 0:
                logits(q_ref[1], 1, jlos[1], bufs[1][0])

        def tail_odd(carry, q=q, hh=hh, cur=cur, nxt=nxt, start=start_other_head):
            logits(q, hh, i, nxt)
            start()
            carry = softmax_pv(hh, i - 1, cur, carry, False)
            return softmax_pv(hh, i, nxt, carry, True)

        def tail_even(carry, hh=hh, cur=cur, start=start_other_head):
            start()
            return softmax_pv(hh, i, cur, carry, True)

        carry = (jnp.full((tq, LANES), -jnp.inf, F32), jnp.zeros((tq, LANES), F32))
        carry = lax.fori_loop(0, n_full // 2, two_blocks, carry)
        _, acc = lax.cond(n_full % 2 == 1, tail_odd, tail_even, carry)
        accs.append(acc)
    o0 = accs[0] / pltpu.roll(accs[0], FOX_HEAD_DIM, 1)
    o1 = accs[1] / pltpu.roll(accs[1], FOX_HEAD_DIM, 1)
    o_ref[...] = jnp.where(lane < FOX_HEAD_DIM, o0, o1)


def _skip_starts(stats, tm, tq):
    r = tq // tm
    nq = stats.shape[0] // r
    scale = FOX_HEAD_DIM ** -0.5
    st = stats.reshape(nq, r, ST_ROWS, LANES)
    per_head = lambda row0: jnp.stack(
        [st[:, :, row0 + h // 2, FOX_HEAD_DIM * (h % 2)] for h in range(FOX_HEADS)], axis=-1)
    f_first = st[:, 0, ST_FFIRST, :FOX_HEADS].T
    f_last = st[:, r - 1, ST_FLAST, :FOX_HEADS].T
    qn = jnp.sqrt(jnp.max(per_head(ST_QN2), axis=1)).T
    kn = jnp.sqrt(jnp.max(per_head(ST_KN2), axis=(0, 1)))[:, None]
    bound = NORM_SLACK * scale * qn * kn
    dmin = scale * jnp.min(per_head(ST_DMIN), axis=1).T - (NORM_SLACK - 1.0) * bound
    thresh = f_first + bound - dmin + SKIP_LOG_MARGIN
    need = f_last[:, None, :] <= thresh[:, :, None]
    idx = jnp.arange(nq)
    first = jnp.min(jnp.where(need, idx[None, None, :], nq), axis=-1)
    return jnp.minimum(first, idx[None, :]).astype(jnp.int32).reshape(-1)


def _attention(jlo, qa, ka, va, tq):
    s = qa.shape[1]
    tk = tq
    kv_spec = pl.BlockSpec((2, s, LANES), lambda p, i, jl: (p, 0, 0))
    return pl.pallas_call(
        functools.partial(_attn_kernel, tq=tq, tk=tk),
        grid_spec=pltpu.PrefetchScalarGridSpec(
            num_scalar_prefetch=1,
            grid=(FOX_HEADS // 2, s // tq),
            in_specs=[pl.BlockSpec((2, tq, LANES), lambda p, i, jl: (p, i, 0)), kv_spec, kv_spec],
            out_specs=pl.BlockSpec((tq, LANES), lambda p, i, jl: (i, p)),
            scratch_shapes=[pltpu.VMEM((tq, tk), F32)] * 4),
        out_shape=jax.ShapeDtypeStruct((s, WIDTH), F32),
        compiler_params=_cparams(("parallel", "arbitrary")),
        name="fox_attention",
    )(jlo, qa, ka, va)


def _chunk_cumsum_matrix():
    r = np.arange(GDN_TILE)
    same = (r[:, None] // CHUNK) == (r[None, :] // CHUNK)
    return jnp.asarray((same & (r[None, :] <= r[:, None])).astype(np.float32), BF16)


def _gdn_prep_kernel(xq_ref, xk_ref, xv_ref, hq_ref, hk_ref, hv_ref, small_ref, cw_ref, aneg_ref,
                     dtb_ref, kl_ref, qd_ref, kd_ref, u_ref, w_ref, aqk_ref, egl_ref, xx_ref):
    t = GDN_TILE
    hd = GDN_HEAD_DIM
    pad = BF16_SUBLANES
    rows_in = xq_ref.shape[0]

    for n, (x_ref, halo_ref) in enumerate(((xq_ref, hq_ref), (xk_ref, hk_ref), (xv_ref, hv_ref))):
        cs = slice(n * WIDTH, (n + 1) * WIDTH)
        halo = halo_ref[...].astype(F32)
        xx_ref[0:pad, cs] = jnp.where(pl.program_id(0) == 0, jnp.zeros_like(halo), halo)
        xx_ref[pad:pad + rows_in, cs] = x_ref[...].astype(F32)

    row = lax.broadcasted_iota(jnp.int32, (t, t), 0)
    col = lax.broadcasted_iota(jnp.int32, (t, t), 1)
    same = (row // CHUNK) == (col // CHUNK)
    incl = jnp.logical_and(same, col <= row)
    strict = jnp.logical_and(same, col < row)
    eye = (row == col).astype(F32)
    row8 = lax.broadcasted_iota(jnp.int32, (8, WIDTH), 0)

    invs, bps, rhss, dests = [], [], [], []
    for tile in range(rows_in // t):
        r0 = tile * t
        rs = slice(r0, r0 + t)
        conv = cw_ref[CONV_K - 1:CONV_K, :] * xx_ref[pad + r0:pad + r0 + t, :]
        for j in range(CONV_K - 1):
            off = pad + r0 - (CONV_K - 1) + j
            conv = conv + cw_ref[j:j + 1, :] * xx_ref[off:off + t, :]
        qkv = _silu(conv)

        small = small_ref[rs, :]
        bcast = lambda c0: jnp.concatenate(
            [jnp.broadcast_to(small[:, c0 + h:c0 + h + 1], (t, hd)) for h in range(GDN_HEADS)],
            axis=1)
        sp_in = bcast(SMALL_BA) + dtb_ref[...]
        softplus = jnp.maximum(sp_in, 0.0) + jnp.log1p(jnp.exp(-jnp.abs(sp_in)))
        g = aneg_ref[...] * softplus
        beta = _sigmoid(bcast(SMALL_BB))
        gcum = _dot_exact_lhs(kl_ref[...], g)
        glast = jnp.concatenate(
            [jnp.broadcast_to(gcum[(c + 1) * CHUNK - 1:(c + 1) * CHUNK, :], (CHUNK, WIDTH))
             for c in range(CHUNKS_PER_TILE)], axis=0)
        eg = jnp.exp(gcum)
        egl = jnp.exp(glast)
        egd = jnp.exp(glast - gcum)
        egl_rows = jnp.zeros((8, WIDTH), F32)
        for c in range(CHUNKS_PER_TILE):
            egl_rows = jnp.where(row8 == c, egl[c * CHUNK:c * CHUNK + 8, :], egl_rows)
        egl_ref[tile * 8:(tile + 1) * 8, :] = egl_rows

        for h in range(GDN_HEADS):
            cs = slice(h * hd, (h + 1) * hd)
            qh = qkv[:, h * hd:(h + 1) * hd]
            kh = qkv[:, WIDTH + h * hd:WIDTH + (h + 1) * hd]
            vh = qkv[:, 2 * WIDTH + h * hd:2 * WIDTH + (h + 1) * hd]
            qh = qh * lax.rsqrt(jnp.sum(qh * qh, axis=-1, keepdims=True) + EPS) * (hd ** -0.5)
            kh = kh * lax.rsqrt(jnp.sum(kh * kh, axis=-1, keepdims=True) + EPS)
            bh = beta[:, cs]
            gc = gcum[:, cs]
            dmat = jnp.concatenate([gc, gc], axis=1) - gc.T[0:1, :]
            gamma = jnp.exp(jnp.where(incl, dmat, -jnp.inf))
            kb = kh * bh
            a = jnp.where(strict, _dot_nt(kb, kh) * gamma, 0.0)
            invs.append(eye - a)
            bps.append(_bf(-a))
            rhss.append(_bf(jnp.concatenate([vh * bh, kb * eg[:, cs]], axis=1)))
            aqk_ref[h, rs, :] = _bf(_dot_nt(qh, kh) * gamma)
            qd_ref[rs, cs] = _bf(qh * eg[:, cs])
            kd_ref[rs, cs] = _bf(kh * egd[:, cs])
            dests.append((rs, cs))

    for _ in range(5):
        bps = [_bf(jnp.dot(b, b, preferred_element_type=F32)) for b in bps]
        invs = [inv + jnp.dot(_bf(inv), b, preferred_element_type=F32)
                for inv, b in zip(invs, bps)]
    for inv, rhs, (rs, cs) in zip(invs, rhss, dests):
        sol = jnp.dot(_bf(inv), rhs, preferred_element_type=F32)
        u_ref[rs, cs] = sol[:, :hd]
        w_ref[rs, cs] = _bf(sol[:, hd:])


def _gdn_prep(z, small, conv_w, aneg, dtb, layer, rows):
    s = z.shape[0]
    t = GDN_TILE
    pad = BF16_SUBLANES
    tiles = rows // t
    kl = _chunk_cumsum_matrix()
    row_spec = pl.BlockSpec((rows, WIDTH), lambda i: (i, 0))
    halo_spec = lambda col: pl.BlockSpec(
        (pad, WIDTH), lambda i: (jnp.maximum(i * (rows // pad) - 1, 0), col // WIDTH))
    return pl.pallas_call(
        _gdn_prep_kernel,
        grid=(s // rows,),
        in_specs=[pl.BlockSpec((rows, WIDTH), lambda i: (i, COL_BQ // WIDTH)),
                  pl.BlockSpec((rows, WIDTH), lambda i: (i, COL_BK // WIDTH)),
                  pl.BlockSpec((rows, WIDTH), lambda i: (i, COL_BV // WIDTH)),
                  halo_spec(COL_BQ), halo_spec(COL_BK), halo_spec(COL_BV),
                  pl.BlockSpec((rows, LANES), lambda i: (i, 0)),
                  _layer_spec((CONV_K, 3 * WIDTH), layer, 1),
                  _layer_spec((1, WIDTH), layer, 1),
                  _layer_spec((1, WIDTH), layer, 1),
                  pl.BlockSpec((t, t), lambda i: (0, 0))],
        out_specs=[row_spec, row_spec, row_spec, row_spec,
                   pl.BlockSpec((GDN_HEADS, rows, t), lambda i: (0, i, 0)),
                   pl.BlockSpec((8 * tiles, WIDTH), lambda i: (i, 0))],
        out_shape=[jax.ShapeDtypeStruct((s, WIDTH), BF16),
                   jax.ShapeDtypeStruct((s, WIDTH), BF16),
                   jax.ShapeDtypeStruct((s, WIDTH), F32),
                   jax.ShapeDtypeStruct((s, WIDTH), BF16),
                   jax.ShapeDtypeStruct((GDN_HEADS, s, t), BF16),
                   jax.ShapeDtypeStruct((s // t * 8, WIDTH), F32)],
        scratch_shapes=[pltpu.VMEM((rows + pad, 3 * WIDTH), F32)],
        compiler_params=_cparams(("parallel",)),
        name="gdn_prep",
    )(z, z, z, z, z, z, small, conv_w, aneg, dtb, kl)


def _gdn_scan_kernel(qd_ref, kd_ref, u_ref, w_ref, aqk_ref, egl_ref, o_ref, state_ref, vn_ref):
    hd = GDN_HEAD_DIM

    @pl.when(pl.program_id(0) == 0)
    def _():
        state_ref[...] = jnp.zeros_like(state_ref)
        vn_ref[...] = jnp.zeros_like(vn_ref)

    heads = range(GDN_HEADS)
    col = lambda h: slice(h * hd, (h + 1) * hd)
    for c in range(CHUNKS_PER_TILE):
        rs = slice(c * CHUNK, (c + 1) * CHUNK)
        sts = [state_ref[h] for h in heads]
        rr = [jnp.dot(jnp.concatenate([w_ref[rs, col(h)], qd_ref[rs, col(h)]], axis=0),
                      _bf(sts[h]), preferred_element_type=F32) for h in heads]
        vns = [_bf(u_ref[rs, col(h)] - rr[h][:CHUNK]) for h in heads]
        for h in heads:
            vn_ref[h, rs, :] = vns[h]
        for h in heads:
            upd = lax.dot_general(kd_ref[rs, col(h)], vns[h], (((0,), (0,)), ((), ())),
                                  preferred_element_type=F32)
            state_ref[h] = sts[h] * egl_ref[c:c + 1, col(h)] + upd
        for h in heads:
            o_ref[rs, col(h)] = rr[h][CHUNK:] + jnp.dot(aqk_ref[h, rs, :], vn_ref[h],
                                                        preferred_element_type=F32)


def _gdn_scan(qd, kd, u, w, aqk, egl):
    s = qd.shape[0]
    t = GDN_TILE
    row_spec = pl.BlockSpec((t, WIDTH), lambda i: (i, 0))
    return pl.pallas_call(
        _gdn_scan_kernel,
        grid=(s // t,),
        in_specs=[row_spec, row_spec, row_spec, row_spec,
                  pl.BlockSpec((GDN_HEADS, t, t), lambda i: (0, i, 0)),
                  pl.BlockSpec((8, WIDTH), lambda i: (i, 0))],
        out_specs=row_spec,
        out_shape=jax.ShapeDtypeStruct((s, WIDTH), F32),
        scratch_shapes=[pltpu.VMEM((GDN_HEADS, GDN_HEAD_DIM, GDN_HEAD_DIM), F32),
                        pltpu.VMEM((GDN_HEADS, t, GDN_HEAD_DIM), BF16)],
        compiler_params=_cparams(("arbitrary",)),
        name="gdn_scan",
    )(qd, kd, u, w, aqk, egl)


def _merge_kernel(x_ref, oa_ref, ob_ref, mq_ref, az_ref, bz_ref, mz_ref, g0_ref, g1_ref, g2_ref,
                  mk_ref, mv_ref, gng_ref, bm_ref, wb_ref, wo_ref, fg_ref, o_ref, *, final_norm):
    hd = GDN_HEAD_DIM
    groups = [slice(r0, r0 + MERGE_SUB) for r0 in range(0, x_ref.shape[0], MERGE_SUB)]
    g_refs = (g0_ref, g1_ref, g2_ref)

    def branch_a(rs):
        return oa_ref[rs, :] * _silu(az_ref[rs, :].astype(F32))

    def branch_b(rs):
        normed = []
        for h in range(GDN_HEADS):
            oh = ob_ref[rs, h * hd:(h + 1) * hd]
            normed.append(oh * lax.rsqrt(jnp.mean(oh * oh, axis=-1, keepdims=True) + EPS))
        return jnp.concatenate(normed, axis=1) * gng_ref[...] * _silu(bz_ref[rs, :].astype(F32))

    def branch_m(rs):
        om = []
        for h in range(MEM_HEADS):
            cs = slice(h * MEM_HEAD_DIM, (h + 1) * MEM_HEAD_DIM)
            sc = _dot_nt(mq_ref[rs, cs], mk_ref[:, cs]) * (MEM_HEAD_DIM ** -0.5)
            p = jnp.exp(sc - jnp.max(sc, axis=-1, keepdims=True))
            om.append(jnp.dot(_bf(p), mv_ref[:, cs], preferred_element_type=F32)
                      / jnp.sum(p, axis=-1, keepdims=True))
        return jnp.concatenate(om, axis=1) * _silu(mz_ref[rs, :].astype(F32))

    merged = [None] * len(groups)
    for n, branch in enumerate((branch_a, branch_b, branch_m)):
        for k, rs in enumerate(groups):
            proj = jnp.dot(_bf(branch(rs)), wb_ref[n], preferred_element_type=F32)
            term = _sigmoid(g_refs[n][rs, :].astype(F32) + bm_ref[n:n + 1, :]) * proj
            merged[k] = term if merged[k] is None else merged[k] + term
    for k, rs in enumerate(groups):
        out = x_ref[rs, :] + jnp.dot(_bf(merged[k]), wo_ref[...], preferred_element_type=F32)
        if final_norm:
            out = out * lax.rsqrt(jnp.mean(out * out, axis=-1, keepdims=True) + EPS) * fg_ref[...]
        o_ref[rs, :] = out


def _merge(x, o_a, o_b, z, memkv, gng, bm, wb, wo, fg_row, layer, tm, final_norm):
    s = x.shape[0]
    zc = lambda col, width: pl.BlockSpec((tm, width), lambda i: (i, col // width))
    n_mem = memkv.shape[0]
    return pl.pallas_call(
        functools.partial(_merge_kernel, final_norm=final_norm),
        grid=(s // tm,),
        in_specs=[pl.BlockSpec((tm, D_MODEL), lambda i: (i, 0)),
                  pl.BlockSpec((tm, WIDTH), lambda i: (i, 0)),
                  pl.BlockSpec((tm, WIDTH), lambda i: (i, 0)),
                  zc(COL_MQ, WIDTH),
                  zc(COL_AZ, WIDTH), zc(COL_BZ, WIDTH), zc(COL_MZ, WIDTH),
                  zc(COL_GATES, D_MODEL), zc(COL_GATES + D_MODEL, D_MODEL),
                  zc(COL_GATES + 2 * D_MODEL, D_MODEL),
                  pl.BlockSpec((n_mem, WIDTH), lambda i: (0, 0)),
                  pl.BlockSpec((n_mem, WIDTH), lambda i: (0, 1)),
                  _layer_spec((1, WIDTH), layer, 1),
                  _layer_spec((N_BRANCH, D_MODEL), layer, 1),
                  _layer_spec((N_BRANCH, WIDTH, D_MODEL), layer, 1),
                  _layer_spec((D_MODEL, D_MODEL), layer, 1),
                  pl.BlockSpec((1, D_MODEL), lambda i: (0, 0))],
        out_specs=pl.BlockSpec((tm, D_MODEL), lambda i: (i, 0)),
        out_shape=jax.ShapeDtypeStruct((s, D_MODEL), F32),
        compiler_params=_cparams(("parallel",)),
        name="merge",
    )(x, o_a, o_b, z, z, z, z, z, z, z, memkv, memkv, gng, bm, wb, wo, fg_row)


def _weight_prep_kernel(w_ref, main_ref, small_ref):
    gap_a = FOX_HEADS
    gap_b = gap_a + 2 * GDN_HEADS
    for c0 in range(0, N_MAIN, WIDTH):
        shift = 0 if c0 < SRC_AF else (gap_a if c0 + gap_a < SRC_BA else gap_b)
        main_ref[:, c0:c0 + WIDTH] = _bf(w_ref[:, c0 + shift:c0 + shift + WIDTH])
    lane = lax.broadcasted_iota(jnp.int32, small_ref.shape, 1)
    chunk_b = SRC_BA - SMALL_BA
    small_ref[...] = _bf(jnp.where(lane < SMALL_BA, w_ref[:, SRC_AF:SRC_AF + LANES],
                                   jnp.where(lane < SMALL_BB + GDN_HEADS,
                                             w_ref[:, chunk_b:chunk_b + LANES], 0.0)))


def _weight_prep(w_in, tr):
    depth, d, n_src = w_in.shape
    assert n_src == N_SRC and (SRC_BA - SMALL_BA) % LANES == 0
    return pl.pallas_call(
        _weight_prep_kernel,
        grid=(depth, d // tr),
        in_specs=[pl.BlockSpec((None, tr, n_src), lambda l, i: (l, i, 0))],
        out_specs=[pl.BlockSpec((None, tr, N_MAIN), lambda l, i: (l, i, 0)),
                   pl.BlockSpec((None, tr, LANES), lambda l, i: (l, i, 0))],
        out_shape=[jax.ShapeDtypeStruct((depth, d, N_MAIN), BF16),
                   jax.ShapeDtypeStruct((depth, d, LANES), BF16)],
        compiler_params=_cparams(("parallel", "parallel")),
        name="weight_prep",
    )(w_in)


def kernel(x, mem, norm_g, w_in, b_fg, b_merge, conv_w, a_log, dt_bias, gdn_norm_g, mem_norm_g,
           w_mem_kv, w_branch, w_out, final_norm_g):
    assert x.shape[0] == 1 and mem.shape[0] == 1
    depth = w_in.shape[0]
    s = x.shape[1]
    rows = lambda v: v.astype(F32).reshape(depth, 1, -1)
    w_main, w_small = _weight_prep(w_in, tr=WEIGHT_PREP_ROWS)
    w_mem_b, w_branch_b, w_out_b = _bf(w_mem_kv), _bf(w_branch), _bf(w_out)
    norm_g3, mem_norm_g3 = rows(norm_g), rows(mem_norm_g)
    bfg3 = rows(jnp.pad(b_fg, ((0, 0), (0, LANES - FOX_HEADS))))
    aneg3 = rows(jnp.repeat(-jnp.exp(a_log.astype(F32)), GDN_HEAD_DIM, axis=1))
    dtb3 = rows(jnp.repeat(dt_bias, GDN_HEAD_DIM, axis=1))
    gng3 = rows(jnp.tile(gdn_norm_g, (1, GDN_HEADS)))
    bm3 = b_merge.reshape(depth, N_BRANCH, D_MODEL)
    fg_row = final_norm_g.reshape(1, D_MODEL)
    tq = min(s, ATTN_TQ)

    h = x[0]
    for l in range(depth):
        z, small = _norm_matmul(h, norm_g3, w_main, w_small, l, tm=min(s, NORM_TM), tn=NORM_TN)
        memkv, = _norm_matmul(mem[0], mem_norm_g3, w_mem_b, None, l, tm=mem.shape[1], tn=2 * WIDTH)
        qa, ka, va, stats = _attn_prep(z, small, bfg3, l, tm=min(s, ATTN_PREP_ROWS))
        o_a = _attention(_skip_starts(stats, ATTN_STAT_TILE, tq), qa, ka, va, tq=tq)
        qd, kd, u, w, aqk, egl = _gdn_prep(z, small, conv_w, aneg3, dtb3, l, rows=GDN_PREP_ROWS)
        o_b = _gdn_scan(qd, kd, u, w, aqk, egl)
        h = _merge(h, o_a, o_b, z, memkv, gng3, bm3, w_branch_b, w_out_b, fg_row, l,
                   tm=MERGE_TM, final_norm=(l == depth - 1))
    return h[None]
```

```python
import functools

import jax
import jax.numpy as jnp
import numpy as np
from jax import lax
from jax.experimental import pallas as pl
from jax.experimental.pallas import tpu as pltpu

F32 = jnp.float32
BF16 = jnp.bfloat16

D_MODEL = 1024
EPS = 1e-6
FOX_HEADS = 8
FOX_HEAD_DIM = 64
GDN_HEADS = 4
GDN_HEAD_DIM = 128
MEM_HEADS = 4
MEM_HEAD_DIM = 128
WIDTH = 512
N_BRANCH = 3
CHUNK = 64
CONV_K = 4
LANES = 128
BF16_SUBLANES = 16
GDN_TILE = 256
CHUNKS_PER_TILE = GDN_TILE // CHUNK

SRC_AF = 3 * WIDTH
SRC_BA = SRC_AF + FOX_HEADS + 4 * WIDTH
N_SRC = 8208
COL_AQ, COL_AK, COL_AV, COL_AZ = 0, 512, 1024, 1536
COL_BQ, COL_BK, COL_BV, COL_BZ = 2048, 2560, 3072, 3584
COL_MQ, COL_MZ, COL_GATES = 4096, 4608, 5120
N_MAIN = 8192
SMALL_BA = 8
SMALL_BB = 12

NORM_TM = 1024
NORM_TN = 2048
ATTN_PREP_ROWS = 1024
ATTN_STAT_TILE = 256
ATTN_TQ = 512
GDN_PREP_ROWS = 512
MERGE_TM = 512
MERGE_SUB = 256

SKIP_LOG_MARGIN = 100.0
NORM_SLACK = 1.01

VMEM_LIMIT = 56 * 1024 * 1024


def _cparams(sem):
    return pltpu.CompilerParams(dimension_semantics=sem, vmem_limit_bytes=VMEM_LIMIT)


def _bf(x):
    return x.astype(BF16)


def _dot(a, b):
    return jnp.dot(_bf(a), _bf(b), preferred_element_type=F32)


def _dot_nt(a, b):
    return lax.dot_general(_bf(a), _bf(b), (((1,), (1,)), ((), ())), preferred_element_type=F32)


def _sigmoid(x):
    return 1.0 / (1.0 + jnp.exp(-x))


def _silu(x):
    return x * _sigmoid(x)


def _split3(x):
    hi = _bf(x).astype(F32)
    r1 = x - hi
    mid = _bf(r1).astype(F32)
    lo = _bf(r1 - mid).astype(F32)
    return hi, mid, lo


def _dot_exact_lhs(mat01, x):
    hi, mid, lo = _split3(x)
    return (jnp.dot(mat01, _bf(hi), preferred_element_type=F32)
            + jnp.dot(mat01, _bf(mid), preferred_element_type=F32)
            + jnp.dot(mat01, _bf(lo), preferred_element_type=F32))


def _layer_spec(shape, layer, n_grid):
    zeros = (0,) * len(shape)
    if n_grid == 1:
        return pl.BlockSpec((None,) + tuple(shape), lambda i: (layer,) + zeros)
    return pl.BlockSpec((None,) + tuple(shape), lambda i, j: (layer,) + zeros)


def _norm_matmul_kernel(x_ref, g_ref, w_ref, *rest, with_small):
    if with_small:
        ws_ref, o_ref, os_ref, h_ref = rest
    else:
        o_ref, h_ref = rest

    @pl.when(pl.program_id(1) == 0)
    def _():
        x = x_ref[...]
        y = x * lax.rsqrt(jnp.mean(x * x, axis=-1, keepdims=True) + EPS)
        h = _bf(y * g_ref[...])
        h_ref[...] = h
        if with_small:
            os_ref[...] = _dot_nt(h, ws_ref[...])

    o_ref[...] = _dot_nt(h_ref[...], w_ref[...]).astype(o_ref.dtype)


def _norm_matmul(x, g, w_t, w_small_t, layer, tm, tn):
    s, d = x.shape
    n = w_t.shape[1]
    with_small = w_small_t is not None
    in_specs = [pl.BlockSpec((tm, d), lambda i, j: (i, 0)),
                _layer_spec((1, d), layer, 2),
                pl.BlockSpec((None, tn, d), lambda i, j: (layer, j, 0))]
    out_specs = [pl.BlockSpec((tm, tn), lambda i, j: (i, j))]
    out_shape = [jax.ShapeDtypeStruct((s, n), BF16)]
    args = [x, g, w_t]
    if with_small:
        in_specs.append(_layer_spec((LANES, d), layer, 2))
        out_specs.append(pl.BlockSpec((tm, LANES), lambda i, j: (i, 0)))
        out_shape.append(jax.ShapeDtypeStruct((s, LANES), F32))
        args.append(w_small_t)
    return pl.pallas_call(
        functools.partial(_norm_matmul_kernel, with_small=with_small),
        grid=(s // tm, n // tn),
        in_specs=in_specs, out_specs=out_specs, out_shape=out_shape,
        scratch_shapes=[pltpu.VMEM((tm, d), BF16)],
        compiler_params=_cparams(("parallel", "arbitrary")),
        name="norm_matmul",
    )(*args)


FL_HI, FL_MID, FL_LO, FL_ONE = 0, 8, 16, 24
ST_FFIRST, ST_FLAST, ST_QN2, ST_KN2, ST_DMIN, ST_ROWS = 0, 1, 2, 6, 10, 16


def _aug_placement():
    d = FOX_HEAD_DIM
    wq, wk, wv = (np.zeros((FOX_HEADS // 2, 2 * LANES, 2 * LANES), np.float32) for _ in range(3))
    for p in range(FOX_HEADS // 2):
        for half in range(2):
            h = 2 * p + half
            feat = half * LANES + half * d
            aux = half * LANES + (1 - half) * d
            for w, scale in ((wq, d ** -0.5), (wk, 1.0), (wv, 1.0)):
                w[p, half * d + np.arange(d), feat + np.arange(d)] = scale
            for a, src in enumerate((FL_HI, FL_MID, FL_LO)):
                wq[p, LANES + src + h, aux + a] = 1.0
                wk[p, LANES + src + h, aux + 3 + a] = -1.0
            wq[p, LANES + FL_ONE, aux + 3:aux + 6] = 1.0
            wk[p, LANES + FL_ONE, aux:aux + 3] = 1.0
            wv[p, LANES + FL_ONE, aux:aux + d] = 1.0
    return [jnp.asarray(w, BF16) for w in (wq, wk, wv)]


def _attn_prep_kernel(q_ref, k_ref, v_ref, small_ref, bfg_ref, tril_ref, wq_ref, wk_ref, wv_ref,
                      nrm_ref, qa_ref, ka_ref, va_ref, stats_ref, carry_ref):
    rows = q_ref.shape[0]
    t = ATTN_STAT_TILE
    subs = [slice(r0, r0 + t) for r0 in range(0, rows, t)]

    @pl.when(pl.program_id(0) == 0)
    def _():
        carry_ref[...] = jnp.zeros_like(carry_ref)

    af = small_ref[...] + bfg_ref[...]
    logf = jnp.minimum(af, 0.0) - jnp.log1p(jnp.exp(-jnp.abs(af)))
    offset = carry_ref[...]
    cums = []
    for rs in subs:
        cums.append(_dot_exact_lhs(tril_ref[...], logf[rs, :]) + offset)
        offset = cums[-1][t - 1:t, :]
    carry_ref[...] = offset
    cum = jnp.concatenate(cums, axis=0)

    lane = lax.broadcasted_iota(jnp.int32, (rows, LANES), 1)
    hi, mid, lo = _split3(cum)
    fl = jnp.where(lane < FL_MID, hi,
                   jnp.where(lane < FL_LO, pltpu.roll(mid, FL_MID, 1),
                             jnp.where(lane < FL_ONE, pltpu.roll(lo, FL_LO, 1),
                                       jnp.where(lane == FL_ONE, 1.0, 0.0))))
    fl = _bf(fl)

    row = lax.broadcasted_iota(jnp.int32, (ST_ROWS, LANES), 0)
    put = lambda st, r, vec: jnp.where(row == r, jnp.broadcast_to(vec, (ST_ROWS, LANES)), st)
    stats = [put(put(jnp.zeros((ST_ROWS, LANES), F32), ST_FFIRST, c[0:1, :]),
                 ST_FLAST, c[t - 1:t, :]) for c in cums]
    for p in range(FOX_HEADS // 2):
        cols = slice(p * LANES, (p + 1) * LANES)
        for x_ref, w_ref, out_ref in ((q_ref, wq_ref, qa_ref), (k_ref, wk_ref, ka_ref),
                                      (v_ref, wv_ref, va_ref)):
            aug = jnp.dot(jnp.concatenate([x_ref[:, cols], fl], axis=1), w_ref[p],
                          preferred_element_type=F32)
            out_ref[2 * p] = _bf(aug[:, :LANES])
            out_ref[2 * p + 1] = _bf(aug[:, LANES:])
        xq, xk = q_ref[:, cols], k_ref[:, cols]
        n2 = jnp.dot(jnp.concatenate([xq * xq, xk * xk], axis=1), nrm_ref[...],
                     preferred_element_type=F32)
        diag = jnp.dot(xq * xk, nrm_ref[:LANES, :LANES], preferred_element_type=F32)
        for k, rs in enumerate(subs):
            top = jnp.max(n2[rs, :], axis=0, keepdims=True)
            stats[k] = put(put(stats[k], ST_QN2 + p, top[:, :LANES]), ST_KN2 + p, top[:, LANES:])
            stats[k] = put(stats[k], ST_DMIN + p, jnp.min(diag[rs, :], axis=0, keepdims=True))
    for k in range(len(subs)):
        stats_ref[k] = stats[k]


def _attn_prep(z, small, bfg, layer, tm):
    s = z.shape[0]
    t = ATTN_STAT_TILE
    tril = jnp.asarray(np.tril(np.ones((t, t), np.float32)), BF16)
    half = np.arange(2 * LANES) // FOX_HEAD_DIM
    nrm = jnp.asarray((half[:, None] == half[None, :]).astype(np.float32), BF16)
    places = _aug_placement()
    out_sds = jax.ShapeDtypeStruct((FOX_HEADS, s, LANES), BF16)
    out_spec = pl.BlockSpec((FOX_HEADS, tm, LANES), lambda i: (0, i, 0))
    const = lambda a: pl.BlockSpec(a.shape, lambda i: (0,) * a.ndim)
    return pl.pallas_call(
        _attn_prep_kernel,
        grid=(s // tm,),
        in_specs=[pl.BlockSpec((tm, WIDTH), lambda i: (i, COL_AQ // WIDTH)),
                  pl.BlockSpec((tm, WIDTH), lambda i: (i, COL_AK // WIDTH)),
                  pl.BlockSpec((tm, WIDTH), lambda i: (i, COL_AV // WIDTH)),
                  pl.BlockSpec((tm, LANES), lambda i: (i, 0)),
                  _layer_spec((1, LANES), layer, 1),
                  const(tril)] + [const(w) for w in places] + [const(nrm)],
        out_specs=[out_spec, out_spec, out_spec,
                   pl.BlockSpec((tm // t, ST_ROWS, LANES), lambda i: (i, 0, 0))],
        out_shape=[out_sds, out_sds, out_sds,
                   jax.ShapeDtypeStruct((s // t, ST_ROWS, LANES), F32)],
        scratch_shapes=[pltpu.VMEM((1, LANES), F32)],
        compiler_params=_cparams(("arbitrary",)),
        name="attn_prep",
    )(z, z, z, small, bfg, tril, *places, nrm)


def _attn_kernel(jlo_ref, q_ref, k_ref, v_ref, o_ref, s0_ref, s1_ref, s2_ref, s3_ref, *, tq, tk):
    pair = pl.program_id(0)
    i = pl.program_id(1)
    lane = lax.broadcasted_iota(jnp.int32, (tq, LANES), 1)
    n_tile = tk // LANES

    def logits(q, hh, j, s_ref):
        start = pl.multiple_of(j * tk, tk)
        s_ref[...] = lax.dot_general(q, k_ref[hh, pl.ds(start, tk), :], (((1,), (1,)), ((), ())),
                                     preferred_element_type=F32)

    def softmax_pv(hh, j, s_ref, carry, masked):
        m, acc = carry
        start = pl.multiple_of(j * tk, tk)
        if masked:
            row = lax.broadcasted_iota(jnp.int32, (tq, tk), 0)
            col = lax.broadcasted_iota(jnp.int32, (tq, tk), 1)
            s_ref[...] = jnp.where(col <= row, s_ref[...], -jnp.inf)
        m_new = jnp.maximum(m, jnp.max(s_ref[...], axis=1, keepdims=True))
        p = jnp.exp(s_ref[...] - jnp.tile(m_new, (1, n_tile)))
        alpha = jnp.exp(m - m_new)
        acc = alpha * acc + jnp.dot(_bf(p), v_ref[hh, pl.ds(start, tk), :],
                                    preferred_element_type=F32)
        return m_new, acc

    bufs = ((s0_ref, s1_ref), (s2_ref, s3_ref))
    jlos = [jlo_ref[(2 * pair + hh) * pl.num_programs(1) + i] for hh in range(2)]
    logits(q_ref[0], 0, jlos[0], bufs[0][0])
    accs = []
    for hh in range(2):
        q = q_ref[hh]
        jlo = jlos[hh]
        cur, nxt = bufs[hh]
        n_full = i - jlo

        def two_blocks(t, carry, q=q, hh=hh, jlo=jlo, cur=cur, nxt=nxt):
            j = jlo + 2 * t
            logits(q, hh, j + 1, nxt)
            carry = softmax_pv(hh, j, cur, carry, False)
            logits(q, hh, j + 2, cur)
            return softmax_pv(hh, j + 1, nxt, carry, False)

        def start_other_head(hh=hh):
            if hh == 0:
                logits(q_ref[1], 1, jlos[1], bufs[1][0])

        def tail_odd(carry, q=q, hh=hh, cur=cur, nxt=nxt, start=start_other_head):
            logits(q, hh, i, nxt)
            start()
            carry = softmax_pv(hh, i - 1, cur, carry, False)
            return softmax_pv(hh, i, nxt, carry, True)

        def tail_even(carry, hh=hh, cur=cur, start=start_other_head):
            start()
            return softmax_pv(hh, i, cur, carry, True)

        carry = (jnp.full((tq, LANES), -jnp.inf, F32), jnp.zeros((tq, LANES), F32))
        carry = lax.fori_loop(0, n_full // 2, two_blocks, carry)
        _, acc = lax.cond(n_full % 2 == 1, tail_odd, tail_even, carry)
        accs.append(acc)
    o0 = accs[0] / pltpu.roll(accs[0], FOX_HEAD_DIM, 1)
    o1 = accs[1] / pltpu.roll(accs[1], FOX_HEAD_DIM, 1)
    o_ref[...] = jnp.where(lane < FOX_HEAD_DIM, o0, o1)


def _skip_starts(stats, tm, tq):
    r = tq // tm
    nq = stats.shape[0] // r
    scale = FOX_HEAD_DIM ** -0.5
    st = stats.reshape(nq, r, ST_ROWS, LANES)
    per_head = lambda row0: jnp.stack(
        [st[:, :, row0 + h // 2, FOX_HEAD_DIM * (h % 2)] for h in range(FOX_HEADS)], axis=-1)
    f_first = st[:, 0, ST_FFIRST, :FOX_HEADS].T
    f_last = st[:, r - 1, ST_FLAST, :FOX_HEADS].T
    qn = jnp.sqrt(jnp.max(per_head(ST_QN2), axis=1)).T
    kn = jnp.sqrt(jnp.max(per_head(ST_KN2), axis=(0, 1)))[:, None]
    bound = NORM_SLACK * scale * qn * kn
    dmin = scale * jnp.min(per_head(ST_DMIN), axis=1).T - (NORM_SLACK - 1.0) * bound
    thresh = f_first + bound - dmin + SKIP_LOG_MARGIN
    need = f_last[:, None, :] <= thresh[:, :, None]
    idx = jnp.arange(nq)
    first = jnp.min(jnp.where(need, idx[None, None, :], nq), axis=-1)
    return jnp.minimum(first, idx[None, :]).astype(jnp.int32).reshape(-1)


def _attention(jlo, qa, ka, va, tq):
    s = qa.shape[1]
    tk = tq
    kv_spec = pl.BlockSpec((2, s, LANES), lambda p, i, jl: (p, 0, 0))
    return pl.pallas_call(
        functools.partial(_attn_kernel, tq=tq, tk=tk),
        grid_spec=pltpu.PrefetchScalarGridSpec(
            num_scalar_prefetch=1,
            grid=(FOX_HEADS // 2, s // tq),
            in_specs=[pl.BlockSpec((2, tq, LANES), lambda p, i, jl: (p, i, 0)), kv_spec, kv_spec],
            out_specs=pl.BlockSpec((tq, LANES), lambda p, i, jl: (i, p)),
            scratch_shapes=[pltpu.VMEM((tq, tk), F32)] * 4),
        out_shape=jax.ShapeDtypeStruct((s, WIDTH), F32),
        compiler_params=_cparams(("parallel", "arbitrary")),
        name="fox_attention",
    )(jlo, qa, ka, va)


def _chunk_cumsum_matrix():
    r = np.arange(GDN_TILE)
    same = (r[:, None] // CHUNK) == (r[None, :] // CHUNK)
    return jnp.asarray((same & (r[None, :] <= r[:, None])).astype(np.float32), BF16)


def _gdn_prep_kernel(xq_ref, xk_ref, xv_ref, hq_ref, hk_ref, hv_ref, small_ref, cw_ref, aneg_ref,
                     dtb_ref, kl_ref, qd_ref, kd_ref, u_ref, w_ref, aqk_ref, egl_ref, xx_ref):
    t = GDN_TILE
    hd = GDN_HEAD_DIM
    pad = BF16_SUBLANES
    rows_in = xq_ref.shape[0]

    for n, (x_ref, halo_ref) in enumerate(((xq_ref, hq_ref), (xk_ref, hk_ref), (xv_ref, hv_ref))):
        cs = slice(n * WIDTH, (n + 1) * WIDTH)
        halo = halo_ref[...].astype(F32)
        xx_ref[0:pad, cs] = jnp.where(pl.program_id(0) == 0, jnp.zeros_like(halo), halo)
        xx_ref[pad:pad + rows_in, cs] = x_ref[...].astype(F32)

    row = lax.broadcasted_iota(jnp.int32, (t, t), 0)
    col = lax.broadcasted_iota(jnp.int32, (t, t), 1)
    same = (row // CHUNK) == (col // CHUNK)
    incl = jnp.logical_and(same, col <= row)
    strict = jnp.logical_and(same, col < row)
    eye = (row == col).astype(F32)
    row8 = lax.broadcasted_iota(jnp.int32, (8, WIDTH), 0)

    invs, bps, rhss, dests = [], [], [], []
    for tile in range(rows_in // t):
        r0 = tile * t
        rs = slice(r0, r0 + t)
        conv = cw_ref[CONV_K - 1:CONV_K, :] * xx_ref[pad + r0:pad + r0 + t, :]
        for j in range(CONV_K - 1):
            off = pad + r0 - (CONV_K - 1) + j
            conv = conv + cw_ref[j:j + 1, :] * xx_ref[off:off + t, :]
        qkv = _silu(conv)

        small = small_ref[rs, :]
        bcast = lambda c0: jnp.concatenate(
            [jnp.broadcast_to(small[:, c0 + h:c0 + h + 1], (t, hd)) for h in range(GDN_HEADS)],
            axis=1)
        sp_in = bcast(SMALL_BA) + dtb_ref[...]
        softplus = jnp.maximum(sp_in, 0.0) + jnp.log1p(jnp.exp(-jnp.abs(sp_in)))
        g = aneg_ref[...] * softplus
        beta = _sigmoid(bcast(SMALL_BB))
        gcum = _dot_exact_lhs(kl_ref[...], g)
        glast = jnp.concatenate(
            [jnp.broadcast_to(gcum[(c + 1) * CHUNK - 1:(c + 1) * CHUNK, :], (CHUNK, WIDTH))
             for c in range(CHUNKS_PER_TILE)], axis=0)
        eg = jnp.exp(gcum)
        egl = jnp.exp(glast)
        egd = jnp.exp(glast - gcum)
        egl_rows = jnp.zeros((8, WIDTH), F32)
        for c in range(CHUNKS_PER_TILE):
            egl_rows = jnp.where(row8 == c, egl[c * CHUNK:c * CHUNK + 8, :], egl_rows)
        egl_ref[tile * 8:(tile + 1) * 8, :] = egl_rows

        for h in range(GDN_HEADS):
            cs = slice(h * hd, (h + 1) * hd)
            qh = qkv[:, h * hd:(h + 1) * hd]
            kh = qkv[:, WIDTH + h * hd:WIDTH + (h + 1) * hd]
            vh = qkv[:, 2 * WIDTH + h * hd:2 * WIDTH + (h + 1) * hd]
            qh = qh * lax.rsqrt(jnp.sum(qh * qh, axis=-1, keepdims=True) + EPS) * (hd ** -0.5)
            kh = kh * lax.rsqrt(jnp.sum(kh * kh, axis=-1, keepdims=True) + EPS)
            bh = beta[:, cs]
            gc = gcum[:, cs]
            dmat = jnp.concatenate([gc, gc], axis=1) - gc.T[0:1, :]
            gamma = jnp.exp(jnp.where(incl, dmat, -jnp.inf))
            kb = kh * bh
            a = jnp.where(strict, _dot_nt(kb, kh) * gamma, 0.0)
            invs.append(eye - a)
            bps.append(_bf(-a))
            rhss.append(_bf(jnp.concatenate([vh * bh, kb * eg[:, cs]], axis=1)))
            aqk_ref[h, rs, :] = _bf(_dot_nt(qh, kh) * gamma)
            qd_ref[rs, cs] = _bf(qh * eg[:, cs])
            kd_ref[rs, cs] = _bf(kh * egd[:, cs])
            dests.append((rs, cs))

    for _ in range(5):
        bps = [_bf(jnp.dot(b, b, preferred_element_type=F32)) for b in bps]
        invs = [inv + jnp.dot(_bf(inv), b, preferred_element_type=F32)
                for inv, b in zip(invs, bps)]
    for inv, rhs, (rs, cs) in zip(invs, rhss, dests):
        sol = jnp.dot(_bf(inv), rhs, preferred_element_type=F32)
        u_ref[rs, cs] = sol[:, :hd]
        w_ref[rs, cs] = _bf(sol[:, hd:])


def _gdn_prep(z, small, conv_w, aneg, dtb, layer, rows):
    s = z.shape[0]
    t = GDN_TILE
    pad = BF16_SUBLANES
    tiles = rows // t
    kl = _chunk_cumsum_matrix()
    row_spec = pl.BlockSpec((rows, WIDTH), lambda i: (i, 0))
    halo_spec = lambda col: pl.BlockSpec(
        (pad, WIDTH), lambda i: (jnp.maximum(i * (rows // pad) - 1, 0), col // WIDTH))
    return pl.pallas_call(
        _gdn_prep_kernel,
        grid=(s // rows,),
        in_specs=[pl.BlockSpec((rows, WIDTH), lambda i: (i, COL_BQ // WIDTH)),
                  pl.BlockSpec((rows, WIDTH), lambda i: (i, COL_BK // WIDTH)),
                  pl.BlockSpec((rows, WIDTH), lambda i: (i, COL_BV // WIDTH)),
                  halo_spec(COL_BQ), halo_spec(COL_BK), halo_spec(COL_BV),
                  pl.BlockSpec((rows, LANES), lambda i: (i, 0)),
                  _layer_spec((CONV_K, 3 * WIDTH), layer, 1),
                  _layer_spec((1, WIDTH), layer, 1),
                  _layer_spec((1, WIDTH), layer, 1),
                  pl.BlockSpec((t, t), lambda i: (0, 0))],
        out_specs=[row_spec, row_spec, row_spec, row_spec,
                   pl.BlockSpec((GDN_HEADS, rows, t), lambda i: (0, i, 0)),
                   pl.BlockSpec((8 * tiles, WIDTH), lambda i: (i, 0))],
        out_shape=[jax.ShapeDtypeStruct((s, WIDTH), BF16),
                   jax.ShapeDtypeStruct((s, WIDTH), BF16),
                   jax.ShapeDtypeStruct((s, WIDTH), F32),
                   jax.ShapeDtypeStruct((s, WIDTH), BF16),
                   jax.ShapeDtypeStruct((GDN_HEADS, s, t), BF16),
                   jax.ShapeDtypeStruct((s // t * 8, WIDTH), F32)],
        scratch_shapes=[pltpu.VMEM((rows + pad, 3 * WIDTH), F32)],
        compiler_params=_cparams(("parallel",)),
        name="gdn_prep",
    )(z, z, z, z, z, z, small, conv_w, aneg, dtb, kl)


def _gdn_scan_kernel(qd_ref, kd_ref, u_ref, w_ref, aqk_ref, egl_ref, o_ref, state_ref, vn_ref):
    hd = GDN_HEAD_DIM

    @pl.when(pl.program_id(0) == 0)
    def _():
        state_ref[...] = jnp.zeros_like(state_ref)
        vn_ref[...] = jnp.zeros_like(vn_ref)

    heads = range(GDN_HEADS)
    col = lambda h: slice(h * hd, (h + 1) * hd)
    for c in range(CHUNKS_PER_TILE):
        rs = slice(c * CHUNK, (c + 1) * CHUNK)
        sts = [state_ref[h] for h in heads]
        rr = [jnp.dot(jnp.concatenate([w_ref[rs, col(h)], qd_ref[rs, col(h)]], axis=0),
                      _bf(sts[h]), preferred_element_type=F32) for h in heads]
        vns = [_bf(u_ref[rs, col(h)] - rr[h][:CHUNK]) for h in heads]
        for h in heads:
            vn_ref[h, rs, :] = vns[h]
        for h in heads:
            upd = lax.dot_general(kd_ref[rs, col(h)], vns[h], (((0,), (0,)), ((), ())),
                                  preferred_element_type=F32)
            state_ref[h] = sts[h] * egl_ref[c:c + 1, col(h)] + upd
        for h in heads:
            o_ref[rs, col(h)] = rr[h][CHUNK:] + jnp.dot(aqk_ref[h, rs, :], vn_ref[h],
                                                        preferred_element_type=F32)


def _gdn_scan(qd, kd, u, w, aqk, egl):
    s = qd.shape[0]
    t = GDN_TILE
    row_spec = pl.BlockSpec((t, WIDTH), lambda i: (i, 0))
    return pl.pallas_call(
        _gdn_scan_kernel,
        grid=(s // t,),
        in_specs=[row_spec, row_spec, row_spec, row_spec,
                  pl.BlockSpec((GDN_HEADS, t, t), lambda i: (0, i, 0)),
                  pl.BlockSpec((8, WIDTH), lambda i: (i, 0))],
        out_specs=row_spec,
        out_shape=jax.ShapeDtypeStruct((s, WIDTH), F32),
        scratch_shapes=[pltpu.VMEM((GDN_HEADS, GDN_HEAD_DIM, GDN_HEAD_DIM), F32),
                        pltpu.VMEM((GDN_HEADS, t, GDN_HEAD_DIM), BF16)],
        compiler_params=_cparams(("arbitrary",)),
        name="gdn_scan",
    )(qd, kd, u, w, aqk, egl)


def _merge_kernel(x_ref, oa_ref, ob_ref, mq_ref, az_ref, bz_ref, mz_ref, g0_ref, g1_ref, g2_ref,
                  mk_ref, mv_ref, gng_ref, bm_ref, wb_ref, wo_ref, fg_ref, o_ref, *, final_norm):
    hd = GDN_HEAD_DIM
    groups = [slice(r0, r0 + MERGE_SUB) for r0 in range(0, x_ref.shape[0], MERGE_SUB)]
    g_refs = (g0_ref, g1_ref, g2_ref)

    def branch_a(rs):
        return oa_ref[rs, :] * _silu(az_ref[rs, :].astype(F32))

    def branch_b(rs):
        normed = []
        for h in range(GDN_HEADS):
            oh = ob_ref[rs, h * hd:(h + 1) * hd]
            normed.append(oh * lax.rsqrt(jnp.mean(oh * oh, axis=-1, keepdims=True) + EPS))
        return jnp.concatenate(normed, axis=1) * gng_ref[...] * _silu(bz_ref[rs, :].astype(F32))

    def branch_m(rs):
        om = []
        for h in range(MEM_HEADS):
            cs = slice(h * MEM_HEAD_DIM, (h + 1) * MEM_HEAD_DIM)
            sc = _dot_nt(mq_ref[rs, cs], mk_ref[:, cs]) * (MEM_HEAD_DIM ** -0.5)
            p = jnp.exp(sc - jnp.max(sc, axis=-1, keepdims=True))
            om.append(jnp.dot(_bf(p), mv_ref[:, cs], preferred_element_type=F32)
                      / jnp.sum(p, axis=-1, keepdims=True))
        return jnp.concatenate(om, axis=1) * _silu(mz_ref[rs, :].astype(F32))

    merged = [None] * len(groups)
    for n, branch in enumerate((branch_a, branch_b, branch_m)):
        for k, rs in enumerate(groups):
            proj = jnp.dot(_bf(branch(rs)), wb_ref[n], preferred_element_type=F32)
            term = _sigmoid(g_refs[n][rs, :].astype(F32) + bm_ref[n:n + 1, :]) * proj
            merged[k] = term if merged[k] is None else merged[k] + term
    for k, rs in enumerate(groups):
        out = x_ref[rs, :] + jnp.dot(_bf(merged[k]), wo_ref[...], preferred_element_type=F32)
        if final_norm:
            out = out * lax.rsqrt(jnp.mean(out * out, axis=-1, keepdims=True) + EPS) * fg_ref[...]
        o_ref[rs, :] = out


def _merge(x, o_a, o_b, z, memkv, gng, bm, wb, wo, fg_row, layer, tm, final_norm):
    s = x.shape[0]
    zc = lambda col, width: pl.BlockSpec((tm, width), lambda i: (i, col // width))
    n_mem = memkv.shape[0]
    return pl.pallas_call(
        functools.partial(_merge_kernel, final_norm=final_norm),
        grid=(s // tm,),
        in_specs=[pl.BlockSpec((tm, D_MODEL), lambda i: (i, 0)),
                  pl.BlockSpec((tm, WIDTH), lambda i: (i, 0)),
                  pl.BlockSpec((tm, WIDTH), lambda i: (i, 0)),
                  zc(COL_MQ, WIDTH),
                  zc(COL_AZ, WIDTH), zc(COL_BZ, WIDTH), zc(COL_MZ, WIDTH),
                  zc(COL_GATES, D_MODEL), zc(COL_GATES + D_MODEL, D_MODEL),
                  zc(COL_GATES + 2 * D_MODEL, D_MODEL),
                  pl.BlockSpec((n_mem, WIDTH), lambda i: (0, 0)),
                  pl.BlockSpec((n_mem, WIDTH), lambda i: (0, 1)),
                  _layer_spec((1, WIDTH), layer, 1),
                  _layer_spec((N_BRANCH, D_MODEL), layer, 1),
                  _layer_spec((N_BRANCH, WIDTH, D_MODEL), layer, 1),
                  _layer_spec((D_MODEL, D_MODEL), layer, 1),
                  pl.BlockSpec((1, D_MODEL), lambda i: (0, 0))],
        out_specs=pl.BlockSpec((tm, D_MODEL), lambda i: (i, 0)),
        out_shape=jax.ShapeDtypeStruct((s, D_MODEL), F32),
        compiler_params=_cparams(("parallel",)),
        name="merge",
    )(x, o_a, o_b, z, z, z, z, z, z, z, memkv, memkv, gng, bm, wb, wo, fg_row)


def _split_w_in(w_in):
    depth, d, n_src = w_in.shape
    assert n_src == N_SRC
    w_t = jnp.swapaxes(w_in, 1, 2)
    gap = FOX_HEADS
    main = jnp.concatenate([w_t[:, :SRC_AF], w_t[:, SRC_AF + gap:SRC_BA],
                            w_t[:, SRC_BA + 2 * GDN_HEADS:]], axis=1)
    small = jnp.concatenate([w_t[:, SRC_AF:SRC_AF + gap], w_t[:, SRC_BA:SRC_BA + 2 * GDN_HEADS],
                             jnp.zeros((depth, LANES - gap - 2 * GDN_HEADS, d), w_in.dtype)], axis=1)
    return _bf(main), _bf(small)


def kernel(x, mem, norm_g, w_in, b_fg, b_merge, conv_w, a_log, dt_bias, gdn_norm_g, mem_norm_g,
           w_mem_kv, w_branch, w_out, final_norm_g):
    assert x.shape[0] == 1 and mem.shape[0] == 1
    depth = w_in.shape[0]
    s = x.shape[1]
    rows = lambda v: v.astype(F32).reshape(depth, 1, -1)
    w_main, w_small = _split_w_in(w_in)
    w_mem_b = _bf(jnp.swapaxes(w_mem_kv, 1, 2))
    w_branch_b, w_out_b = _bf(w_branch), _bf(w_out)
    norm_g3, mem_norm_g3 = rows(norm_g), rows(mem_norm_g)
    bfg3 = rows(jnp.pad(b_fg, ((0, 0), (0, LANES - FOX_HEADS))))
    aneg3 = rows(jnp.repeat(-jnp.exp(a_log.astype(F32)), GDN_HEAD_DIM, axis=1))
    dtb3 = rows(jnp.repeat(dt_bias, GDN_HEAD_DIM, axis=1))
    gng3 = rows(jnp.tile(gdn_norm_g, (1, GDN_HEADS)))
    bm3 = b_merge.reshape(depth, N_BRANCH, D_MODEL)
    fg_row = final_norm_g.reshape(1, D_MODEL)
    tq = min(s, ATTN_TQ)

    h = x[0]
    for l in range(depth):
        z, small = _norm_matmul(h, norm_g3, w_main, w_small, l, tm=min(s, NORM_TM), tn=NORM_TN)
        memkv, = _norm_matmul(mem[0], mem_norm_g3, w_mem_b, None, l, tm=mem.shape[1], tn=2 * WIDTH)
        qa, ka, va, stats = _attn_prep(z, small, bfg3, l, tm=min(s, ATTN_PREP_ROWS))
        o_a = _attention(_skip_starts(stats, ATTN_STAT_TILE, tq), qa, ka, va, tq=tq)
        qd, kd, u, w, aqk, egl = _gdn_prep(z, small, conv_w, aneg3, dtb3, l, rows=GDN_PREP_ROWS)
        o_b = _gdn_scan(qd, kd, u, w, aqk, egl)
        h = _merge(h, o_a, o_b, z, memkv, gng3, bm3, w_branch_b, w_out_b, fg_row, l,
                   tm=MERGE_TM, final_norm=(l == depth - 1))
    return h[None]
```

```python
import functools

import jax
import jax.numpy as jnp
import numpy as np
from jax import lax
from jax.experimental import pallas as pl
from jax.experimental.pallas import tpu as pltpu

F32 = jnp.float32
BF16 = jnp.bfloat16

D_MODEL = 1024
EPS = 1e-6
FOX_HEADS = 8
FOX_HEAD_DIM = 64
GDN_HEADS = 4
GDN_HEAD_DIM = 128
MEM_HEADS = 4
MEM_HEAD_DIM = 128
WIDTH = 512
N_BRANCH = 3
CHUNK = 64
CONV_K = 4
LANES = 128
F32_SUBLANES = 8
BF16_SUBLANES = 16
GDN_TILE = 256
CHUNKS_PER_TILE = GDN_TILE // CHUNK

SRC_AF = 3 * WIDTH
SRC_BA = SRC_AF + FOX_HEADS + 4 * WIDTH
N_SRC = 8208
COL_AQ, COL_AK, COL_AV, COL_AZ = 0, 512, 1024, 1536
COL_BQ, COL_BK, COL_BV, COL_BZ = 2048, 2560, 3072, 3584
COL_MQ, COL_MZ, COL_GATES = 4096, 4608, 5120
N_MAIN = 8192
SMALL_BA = 8
SMALL_BB = 12

NORM_TM = 2048
NORM_TN = 1024
ATTN_PREP_ROWS = 1024
ATTN_STAT_TILE = 256
ATTN_TQ = 512
GDN_PREP_ROWS = 512
MERGE_TM = 512
MERGE_SUB = 256

SKIP_LOG_MARGIN = 100.0
NORM_SLACK = 1.01

VMEM_LIMIT = 56 * 1024 * 1024


def _cparams(sem):
    return pltpu.CompilerParams(dimension_semantics=sem, vmem_limit_bytes=VMEM_LIMIT)


def _bf(x):
    return x.astype(BF16)


def _dot(a, b):
    return jnp.dot(_bf(a), _bf(b), preferred_element_type=F32)


def _dot_nt(a, b):
    return lax.dot_general(_bf(a), _bf(b), (((1,), (1,)), ((), ())), preferred_element_type=F32)


def _sigmoid(x):
    return 1.0 / (1.0 + jnp.exp(-x))


def _silu(x):
    return x * _sigmoid(x)


def _split3(x):
    hi = _bf(x).astype(F32)
    r1 = x - hi
    mid = _bf(r1).astype(F32)
    lo = _bf(r1 - mid).astype(F32)
    return hi, mid, lo


def _dot_exact_lhs(mat01, x):
    hi, mid, lo = _split3(x)
    return (jnp.dot(mat01, _bf(hi), preferred_element_type=F32)
            + jnp.dot(mat01, _bf(mid), preferred_element_type=F32)
            + jnp.dot(mat01, _bf(lo), preferred_element_type=F32))


def _layer_spec(shape, layer, n_grid):
    zeros = (0,) * len(shape)
    if n_grid == 1:
        return pl.BlockSpec((None,) + tuple(shape), lambda i: (layer,) + zeros)
    return pl.BlockSpec((None,) + tuple(shape), lambda i, j: (layer,) + zeros)


def _norm_matmul_kernel(x_ref, g_ref, *rest, n_w, n_narrow):
    w_refs, narrow_refs, outs = rest[:n_w], rest[n_w:n_w + n_narrow], rest[n_w + n_narrow:]
    if n_narrow:
        o_ref, os_ref, h_ref = outs
    else:
        o_ref, h_ref = outs

    @pl.when(pl.program_id(1) == 0)
    def _():
        x = x_ref[...]
        y = x * lax.rsqrt(jnp.mean(x * x, axis=-1, keepdims=True) + EPS)
        h = _bf(y * g_ref[...])
        h_ref[...] = h
        if n_narrow:
            rows = [r[...] for r in narrow_refs]
            used = sum(r.shape[0] for r in rows)
            rows.append(jnp.zeros((LANES - used, x.shape[1]), rows[0].dtype))
            os_ref[...] = _dot_nt(h, jnp.concatenate(rows, axis=0))

    sub = o_ref.shape[1] // n_w
    for k, w_ref in enumerate(w_refs):
        o_ref[:, k * sub:(k + 1) * sub] = _dot_nt(h_ref[...], w_ref[...]).astype(o_ref.dtype)


def _norm_matmul(x, g, w_t, layer, tm, tn, n_out=None, sub=None, row_start=None, narrow_rows=()):
    s, d = x.shape
    n_out = w_t.shape[1] if n_out is None else n_out
    sub = tn if sub is None else sub
    n_w = tn // sub
    n_src = w_t.shape[1]
    w_rows = w_t.reshape(w_t.shape[0] * n_src, d)
    unit = F32_SUBLANES
    window = lambda rows, start: pl.BlockSpec(
        (pl.Element(rows), pl.Element(d)),
        lambda i, j: ((layer * (n_src // unit) + start(j)) * unit, 0))
    if row_start is None:
        w_specs = [window(tn, lambda j: j * (tn // unit))]
    else:
        w_specs = [window(sub, lambda j, k=k: row_start(j * n_w + k)) for k in range(n_w)]
    narrow_specs = [window(cnt, lambda j, r=r: r // unit) for r, cnt in narrow_rows]
    out_specs = [pl.BlockSpec((tm, tn), lambda i, j: (i, j))]
    out_shape = [jax.ShapeDtypeStruct((s, n_out), BF16)]
    if narrow_rows:
        out_specs.append(pl.BlockSpec((tm, LANES), lambda i, j: (i, 0)))
        out_shape.append(jax.ShapeDtypeStruct((s, LANES), F32))
    return pl.pallas_call(
        functools.partial(_norm_matmul_kernel, n_w=n_w, n_narrow=len(narrow_rows)),
        grid=(s // tm, n_out // tn),
        in_specs=[pl.BlockSpec((tm, d), lambda i, j: (i, 0)), _layer_spec((1, d), layer, 2)]
        + w_specs + narrow_specs,
        out_specs=out_specs, out_shape=out_shape,
        scratch_shapes=[pltpu.VMEM((tm, d), BF16)],
        compiler_params=_cparams(("parallel", "arbitrary")),
        name="norm_matmul",
    )(x, g, *([w_rows] * (n_w + len(narrow_rows))))


FL_HI, FL_MID, FL_LO, FL_ONE = 0, 8, 16, 24
ST_FFIRST, ST_FLAST, ST_QN2, ST_KN2, ST_DMIN, ST_ROWS = 0, 1, 2, 6, 10, 16


def _aug_placement():
    d = FOX_HEAD_DIM
    wq, wk, wv = (np.zeros((FOX_HEADS // 2, 2 * LANES, 2 * LANES), np.float32) for _ in range(3))
    for p in range(FOX_HEADS // 2):
        for half in range(2):
            h = 2 * p + half
            feat = half * LANES + half * d
            aux = half * LANES + (1 - half) * d
            for w, scale in ((wq, d ** -0.5), (wk, 1.0), (wv, 1.0)):
                w[p, half * d + np.arange(d), feat + np.arange(d)] = scale
            for a, src in enumerate((FL_HI, FL_MID, FL_LO)):
                wq[p, LANES + src + h, aux + a] = 1.0
                wk[p, LANES + src + h, aux + 3 + a] = -1.0
            wq[p, LANES + FL_ONE, aux + 3:aux + 6] = 1.0
            wk[p, LANES + FL_ONE, aux:aux + 3] = 1.0
            wv[p, LANES + FL_ONE, aux:aux + d] = 1.0
    return [jnp.asarray(w, BF16) for w in (wq, wk, wv)]


def _attn_prep_kernel(q_ref, k_ref, v_ref, small_ref, bfg_ref, tril_ref, wq_ref, wk_ref, wv_ref,
                      nrm_ref, qa_ref, ka_ref, va_ref, stats_ref, carry_ref):
    rows = q_ref.shape[0]
    t = ATTN_STAT_TILE
    subs = [slice(r0, r0 + t) for r0 in range(0, rows, t)]

    @pl.when(pl.program_id(0) == 0)
    def _():
        carry_ref[...] = jnp.zeros_like(carry_ref)

    af = small_ref[...] + bfg_ref[...]
    logf = jnp.minimum(af, 0.0) - jnp.log1p(jnp.exp(-jnp.abs(af)))
    offset = carry_ref[...]
    cums = []
    for rs in subs:
        cums.append(_dot_exact_lhs(tril_ref[...], logf[rs, :]) + offset)
        offset = cums[-1][t - 1:t, :]
    carry_ref[...] = offset
    cum = jnp.concatenate(cums, axis=0)

    lane = lax.broadcasted_iota(jnp.int32, (rows, LANES), 1)
    hi, mid, lo = _split3(cum)
    fl = jnp.where(lane < FL_MID, hi,
                   jnp.where(lane < FL_LO, pltpu.roll(mid, FL_MID, 1),
                             jnp.where(lane < FL_ONE, pltpu.roll(lo, FL_LO, 1),
                                       jnp.where(lane == FL_ONE, 1.0, 0.0))))
    fl = _bf(fl)

    row = lax.broadcasted_iota(jnp.int32, (ST_ROWS, LANES), 0)
    put = lambda st, r, vec: jnp.where(row == r, jnp.broadcast_to(vec, (ST_ROWS, LANES)), st)
    stats = [put(put(jnp.zeros((ST_ROWS, LANES), F32), ST_FFIRST, c[0:1, :]),
                 ST_FLAST, c[t - 1:t, :]) for c in cums]
    for p in range(FOX_HEADS // 2):
        cols = slice(p * LANES, (p + 1) * LANES)
        for x_ref, w_ref, out_ref in ((q_ref, wq_ref, qa_ref), (k_ref, wk_ref, ka_ref),
                                      (v_ref, wv_ref, va_ref)):
            aug = jnp.dot(jnp.concatenate([x_ref[:, cols], fl], axis=1), w_ref[p],
                          preferred_element_type=F32)
            out_ref[2 * p] = _bf(aug[:, :LANES])
            out_ref[2 * p + 1] = _bf(aug[:, LANES:])
        xq, xk = q_ref[:, cols], k_ref[:, cols]
        n2 = jnp.dot(jnp.concatenate([xq * xq, xk * xk], axis=1), nrm_ref[...],
                     preferred_element_type=F32)
        diag = jnp.dot(xq * xk, nrm_ref[:LANES, :LANES], preferred_element_type=F32)
        for k, rs in enumerate(subs):
            top = jnp.max(n2[rs, :], axis=0, keepdims=True)
            stats[k] = put(put(stats[k], ST_QN2 + p, top[:, :LANES]), ST_KN2 + p, top[:, LANES:])
            stats[k] = put(stats[k], ST_DMIN + p, jnp.min(diag[rs, :], axis=0, keepdims=True))
    for k in range(len(subs)):
        stats_ref[k] = stats[k]


def _attn_prep(z, small, bfg, layer, tm):
    s = z.shape[0]
    t = ATTN_STAT_TILE
    tril = jnp.asarray(np.tril(np.ones((t, t), np.float32)), BF16)
    half = np.arange(2 * LANES) // FOX_HEAD_DIM
    nrm = jnp.asarray((half[:, None] == half[None, :]).astype(np.float32), BF16)
    places = _aug_placement()
    out_sds = jax.ShapeDtypeStruct((FOX_HEADS, s, LANES), BF16)
    out_spec = pl.BlockSpec((FOX_HEADS, tm, LANES), lambda i: (0, i, 0))
    const = lambda a: pl.BlockSpec(a.shape, lambda i: (0,) * a.ndim)
    return pl.pallas_call(
        _attn_prep_kernel,
        grid=(s // tm,),
        in_specs=[pl.BlockSpec((tm, WIDTH), lambda i: (i, COL_AQ // WIDTH)),
                  pl.BlockSpec((tm, WIDTH), lambda i: (i, COL_AK // WIDTH)),
                  pl.BlockSpec((tm, WIDTH), lambda i: (i, COL_AV // WIDTH)),
                  pl.BlockSpec((tm, LANES), lambda i: (i, 0)),
                  _layer_spec((1, LANES), layer, 1),
                  const(tril)] + [const(w) for w in places] + [const(nrm)],
        out_specs=[out_spec, out_spec, out_spec,
                   pl.BlockSpec((tm // t, ST_ROWS, LANES), lambda i: (i, 0, 0))],
        out_shape=[out_sds, out_sds, out_sds,
                   jax.ShapeDtypeStruct((s // t, ST_ROWS, LANES), F32)],
        scratch_shapes=[pltpu.VMEM((1, LANES), F32)],
        compiler_params=_cparams(("arbitrary",)),
        name="attn_prep",
    )(z, z, z, small, bfg, tril, *places, nrm)


def _attn_kernel(jlo_ref, q_ref, k_ref, v_ref, o_ref, s0_ref, s1_ref, s2_ref, s3_ref, *, tq, tk):
    pair = pl.program_id(0)
    i = pl.program_id(1)
    lane = lax.broadcasted_iota(jnp.int32, (tq, LANES), 1)
    n_tile = tk // LANES

    def logits(q, hh, j, s_ref):
        start = pl.multiple_of(j * tk, tk)
        s_ref[...] = lax.dot_general(q, k_ref[hh, pl.ds(start, tk), :], (((1,), (1,)), ((), ())),
                                     preferred_element_type=F32)

    def softmax_pv(hh, j, s_ref, carry, masked):
        m, acc = carry
        start = pl.multiple_of(j * tk, tk)
        if masked:
            row = lax.broadcasted_iota(jnp.int32, (tq, tk), 0)
            col = lax.broadcasted_iota(jnp.int32, (tq, tk), 1)
            s_ref[...] = jnp.where(col <= row, s_ref[...], -jnp.inf)
        m_new = jnp.maximum(m, jnp.max(s_ref[...], axis=1, keepdims=True))
        p = jnp.exp(s_ref[...] - jnp.tile(m_new, (1, n_tile)))
        alpha = jnp.exp(m - m_new)
        acc = alpha * acc + jnp.dot(_bf(p), v_ref[hh, pl.ds(start, tk), :],
                                    preferred_element_type=F32)
        return m_new, acc

    bufs = ((s0_ref, s1_ref), (s2_ref, s3_ref))
    jlos = [jlo_ref[(2 * pair + hh) * pl.num_programs(1) + i] for hh in range(2)]
    logits(q_ref[0], 0, jlos[0], bufs[0][0])
    accs = []
    for hh in range(2):
        q = q_ref[hh]
        jlo = jlos[hh]
        cur, nxt = bufs[hh]
        n_full = i - jlo

        def two_blocks(t, carry, q=q, hh=hh, jlo=jlo, cur=cur, nxt=nxt):
            j = jlo + 2 * t
            logits(q, hh, j + 1, nxt)
            carry = softmax_pv(hh, j, cur, carry, False)
            logits(q, hh, j + 2, cur)
            return softmax_pv(hh, j + 1, nxt, carry, False)

        def start_other_head(hh=hh):
            if hh == 0:
                logits(q_ref[1], 1, jlos[1], bufs[1][0])

        def tail_odd(carry, q=q, hh=hh, cur=cur, nxt=nxt, start=start_other_head):
            logits(q, hh, i, nxt)
            start()
            carry = softmax_pv(hh, i - 1, cur, carry, False)
            return softmax_pv(hh, i, nxt, carry, True)

        def tail_even(carry, hh=hh, cur=cur, start=start_other_head):
            start()
            return softmax_pv(hh, i, cur, carry, True)

        carry = (jnp.full((tq, LANES), -jnp.inf, F32), jnp.zeros((tq, LANES), F32))
        carry = lax.fori_loop(0, n_full // 2, two_blocks, carry)
        _, acc = lax.cond(n_full % 2 == 1, tail_odd, tail_even, carry)
        accs.append(acc)
    o0 = accs[0] / pltpu.roll(accs[0], FOX_HEAD_DIM, 1)
    o1 = accs[1] / pltpu.roll(accs[1], FOX_HEAD_DIM, 1)
    o_ref[...] = jnp.where(lane < FOX_HEAD_DIM, o0, o1)


def _skip_starts(stats, tm, tq):
    r = tq // tm
    nq = stats.shape[0] // r
    scale = FOX_HEAD_DIM ** -0.5
    st = stats.reshape(nq, r, ST_ROWS, LANES)
    per_head = lambda row0: jnp.stack(
        [st[:, :, row0 + h // 2, FOX_HEAD_DIM * (h % 2)] for h in range(FOX_HEADS)], axis=-1)
    f_first = st[:, 0, ST_FFIRST, :FOX_HEADS].T
    f_last = st[:, r - 1, ST_FLAST, :FOX_HEADS].T
    qn = jnp.sqrt(jnp.max(per_head(ST_QN2), axis=1)).T
    kn = jnp.sqrt(jnp.max(per_head(ST_KN2), axis=(0, 1)))[:, None]
    bound = NORM_SLACK * scale * qn * kn
    dmin = scale * jnp.min(per_head(ST_DMIN), axis=1).T - (NORM_SLACK - 1.0) * bound
    thresh = f_first + bound - dmin + SKIP_LOG_MARGIN
    need = f_last[:, None, :] <= thresh[:, :, None]
    idx = jnp.arange(nq)
    first = jnp.min(jnp.where(need, idx[None, None, :], nq), axis=-1)
    return jnp.minimum(first, idx[None, :]).astype(jnp.int32).reshape(-1)


def _attention(jlo, qa, ka, va, tq):
    s = qa.shape[1]
    tk = tq
    kv_spec = pl.BlockSpec((2, s, LANES), lambda p, i, jl: (p, 0, 0))
    return pl.pallas_call(
        functools.partial(_attn_kernel, tq=tq, tk=tk),
        grid_spec=pltpu.PrefetchScalarGridSpec(
            num_scalar_prefetch=1,
            grid=(FOX_HEADS // 2, s // tq),
            in_specs=[pl.BlockSpec((2, tq, LANES), lambda p, i, jl: (p, i, 0)), kv_spec, kv_spec],
            out_specs=pl.BlockSpec((tq, LANES), lambda p, i, jl: (i, p)),
            scratch_shapes=[pltpu.VMEM((tq, tk), F32)] * 4),
        out_shape=jax.ShapeDtypeStruct((s, WIDTH), F32),
        compiler_params=_cparams(("parallel", "arbitrary")),
        name="fox_attention",
    )(jlo, qa, ka, va)


def _chunk_cumsum_matrix():
    r = np.arange(GDN_TILE)
    same = (r[:, None] // CHUNK) == (r[None, :] // CHUNK)
    return jnp.asarray((same & (r[None, :] <= r[:, None])).astype(np.float32), BF16)


def _gdn_prep_kernel(xq_ref, xk_ref, xv_ref, hq_ref, hk_ref, hv_ref, small_ref, cw_ref, aneg_ref,
                     dtb_ref, kl_ref, qd_ref, kd_ref, u_ref, w_ref, aqk_ref, egl_ref, xx_ref):
    t = GDN_TILE
    hd = GDN_HEAD_DIM
    pad = BF16_SUBLANES
    rows_in = xq_ref.shape[0]

    for n, (x_ref, halo_ref) in enumerate(((xq_ref, hq_ref), (xk_ref, hk_ref), (xv_ref, hv_ref))):
        cs = slice(n * WIDTH, (n + 1) * WIDTH)
        halo = halo_ref[...].astype(F32)
        xx_ref[0:pad, cs] = jnp.where(pl.program_id(0) == 0, jnp.zeros_like(halo), halo)
        xx_ref[pad:pad + rows_in, cs] = x_ref[...].astype(F32)

    row = lax.broadcasted_iota(jnp.int32, (t, t), 0)
    col = lax.broadcasted_iota(jnp.int32, (t, t), 1)
    same = (row // CHUNK) == (col // CHUNK)
    incl = jnp.logical_and(same, col <= row)
    strict = jnp.logical_and(same, col < row)
    eye = (row == col).astype(F32)
    row8 = lax.broadcasted_iota(jnp.int32, (8, WIDTH), 0)

    invs, bps, rhss, dests = [], [], [], []
    for tile in range(rows_in // t):
        r0 = tile * t
        rs = slice(r0, r0 + t)
        conv = cw_ref[CONV_K - 1:CONV_K, :] * xx_ref[pad + r0:pad + r0 + t, :]
        for j in range(CONV_K - 1):
            off = pad + r0 - (CONV_K - 1) + j
            conv = conv + cw_ref[j:j + 1, :] * xx_ref[off:off + t, :]
        qkv = _silu(conv)

        small = small_ref[rs, :]
        bcast = lambda c0: jnp.concatenate(
            [jnp.broadcast_to(small[:, c0 + h:c0 + h + 1], (t, hd)) for h in range(GDN_HEADS)],
            axis=1)
        sp_in = bcast(SMALL_BA) + dtb_ref[...]
        softplus = jnp.maximum(sp_in, 0.0) + jnp.log1p(jnp.exp(-jnp.abs(sp_in)))
        g = aneg_ref[...] * softplus
        beta = _sigmoid(bcast(SMALL_BB))
        gcum = _dot_exact_lhs(kl_ref[...], g)
        glast = jnp.concatenate(
            [jnp.broadcast_to(gcum[(c + 1) * CHUNK - 1:(c + 1) * CHUNK, :], (CHUNK, WIDTH))
             for c in range(CHUNKS_PER_TILE)], axis=0)
        eg = jnp.exp(gcum)
        egl = jnp.exp(glast)
        egd = jnp.exp(glast - gcum)
        egl_rows = jnp.zeros((8, WIDTH), F32)
        for c in range(CHUNKS_PER_TILE):
            egl_rows = jnp.where(row8 == c, egl[c * CHUNK:c * CHUNK + 8, :], egl_rows)
        egl_ref[tile * 8:(tile + 1) * 8, :] = egl_rows

        for h in range(GDN_HEADS):
            cs = slice(h * hd, (h + 1) * hd)
            qh = qkv[:, h * hd:(h + 1) * hd]
            kh = qkv[:, WIDTH + h * hd:WIDTH + (h + 1) * hd]
            vh = qkv[:, 2 * WIDTH + h * hd:2 * WIDTH + (h + 1) * hd]
            qh = qh * lax.rsqrt(jnp.sum(qh * qh, axis=-1, keepdims=True) + EPS) * (hd ** -0.5)
            kh = kh * lax.rsqrt(jnp.sum(kh * kh, axis=-1, keepdims=True) + EPS)
            bh = beta[:, cs]
            gc = gcum[:, cs]
            dmat = jnp.concatenate([gc, gc], axis=1) - gc.T[0:1, :]
            gamma = jnp.exp(jnp.where(incl, dmat, -jnp.inf))
            kb = kh * bh
            a = jnp.where(strict, _dot_nt(kb, kh) * gamma, 0.0)
            invs.append(eye - a)
            bps.append(_bf(-a))
            rhss.append(_bf(jnp.concatenate([vh * bh, kb * eg[:, cs]], axis=1)))
            aqk_ref[h, rs, :] = _bf(_dot_nt(qh, kh) * gamma)
            qd_ref[rs, cs] = _bf(qh * eg[:, cs])
            kd_ref[rs, cs] = _bf(kh * egd[:, cs])
            dests.append((rs, cs))

    for _ in range(5):
        bps = [_bf(jnp.dot(b, b, preferred_element_type=F32)) for b in bps]
        invs = [inv + jnp.dot(_bf(inv), b, preferred_element_type=F32)
                for inv, b in zip(invs, bps)]
    for inv, rhs, (rs, cs) in zip(invs, rhss, dests):
        sol = jnp.dot(_bf(inv), rhs, preferred_element_type=F32)
        u_ref[rs, cs] = sol[:, :hd]
        w_ref[rs, cs] = _bf(sol[:, hd:])


def _gdn_prep(z, small, conv_w, aneg, dtb, layer, rows):
    s = z.shape[0]
    t = GDN_TILE
    pad = BF16_SUBLANES
    tiles = rows // t
    kl = _chunk_cumsum_matrix()
    row_spec = pl.BlockSpec((rows, WIDTH), lambda i: (i, 0))
    halo_spec = lambda col: pl.BlockSpec(
        (pad, WIDTH), lambda i: (jnp.maximum(i * (rows // pad) - 1, 0), col // WIDTH))
    return pl.pallas_call(
        _gdn_prep_kernel,
        grid=(s // rows,),
        in_specs=[pl.BlockSpec((rows, WIDTH), lambda i: (i, COL_BQ // WIDTH)),
                  pl.BlockSpec((rows, WIDTH), lambda i: (i, COL_BK // WIDTH)),
                  pl.BlockSpec((rows, WIDTH), lambda i: (i, COL_BV // WIDTH)),
                  halo_spec(COL_BQ), halo_spec(COL_BK), halo_spec(COL_BV),
                  pl.BlockSpec((rows, LANES), lambda i: (i, 0)),
                  _layer_spec((CONV_K, 3 * WIDTH), layer, 1),
                  _layer_spec((1, WIDTH), layer, 1),
                  _layer_spec((1, WIDTH), layer, 1),
                  pl.BlockSpec((t, t), lambda i: (0, 0))],
        out_specs=[row_spec, row_spec, row_spec, row_spec,
                   pl.BlockSpec((GDN_HEADS, rows, t), lambda i: (0, i, 0)),
                   pl.BlockSpec((8 * tiles, WIDTH), lambda i: (i, 0))],
        out_shape=[jax.ShapeDtypeStruct((s, WIDTH), BF16),
                   jax.ShapeDtypeStruct((s, WIDTH), BF16),
                   jax.ShapeDtypeStruct((s, WIDTH), F32),
                   jax.ShapeDtypeStruct((s, WIDTH), BF16),
                   jax.ShapeDtypeStruct((GDN_HEADS, s, t), BF16),
                   jax.ShapeDtypeStruct((s // t * 8, WIDTH), F32)],
        scratch_shapes=[pltpu.VMEM((rows + pad, 3 * WIDTH), F32)],
        compiler_params=_cparams(("parallel",)),
        name="gdn_prep",
    )(z, z, z, z, z, z, small, conv_w, aneg, dtb, kl)


def _gdn_scan_kernel(qd_ref, kd_ref, u_ref, w_ref, aqk_ref, egl_ref, o_ref, state_ref, vn_ref):
    hd = GDN_HEAD_DIM

    @pl.when(pl.program_id(0) == 0)
    def _():
        state_ref[...] = jnp.zeros_like(state_ref)
        vn_ref[...] = jnp.zeros_like(vn_ref)

    heads = range(GDN_HEADS)
    col = lambda h: slice(h * hd, (h + 1) * hd)
    for c in range(CHUNKS_PER_TILE):
        rs = slice(c * CHUNK, (c + 1) * CHUNK)
        sts = [state_ref[h] for h in heads]
        rr = [jnp.dot(jnp.concatenate([w_ref[rs, col(h)], qd_ref[rs, col(h)]], axis=0),
                      _bf(sts[h]), preferred_element_type=F32) for h in heads]
        vns = [_bf(u_ref[rs, col(h)] - rr[h][:CHUNK]) for h in heads]
        for h in heads:
            vn_ref[h, rs, :] = vns[h]
        for h in heads:
            upd = lax.dot_general(kd_ref[rs, col(h)], vns[h], (((0,), (0,)), ((), ())),
                                  preferred_element_type=F32)
            state_ref[h] = sts[h] * egl_ref[c:c + 1, col(h)] + upd
        for h in heads:
            o_ref[rs, col(h)] = rr[h][CHUNK:] + jnp.dot(aqk_ref[h, rs, :], vn_ref[h],
                                                        preferred_element_type=F32)


def _gdn_scan(qd, kd, u, w, aqk, egl):
    s = qd.shape[0]
    t = GDN_TILE
    row_spec = pl.BlockSpec((t, WIDTH), lambda i: (i, 0))
    return pl.pallas_call(
        _gdn_scan_kernel,
        grid=(s // t,),
        in_specs=[row_spec, row_spec, row_spec, row_spec,
                  pl.BlockSpec((GDN_HEADS, t, t), lambda i: (0, i, 0)),
                  pl.BlockSpec((8, WIDTH), lambda i: (i, 0))],
        out_specs=row_spec,
        out_shape=jax.ShapeDtypeStruct((s, WIDTH), F32),
        scratch_shapes=[pltpu.VMEM((GDN_HEADS, GDN_HEAD_DIM, GDN_HEAD_DIM), F32),
                        pltpu.VMEM((GDN_HEADS, t, GDN_HEAD_DIM), BF16)],
        compiler_params=_cparams(("arbitrary",)),
        name="gdn_scan",
    )(qd, kd, u, w, aqk, egl)


def _merge_kernel(x_ref, oa_ref, ob_ref, mq_ref, az_ref, bz_ref, mz_ref, g0_ref, g1_ref, g2_ref,
                  mk_ref, mv_ref, gng_ref, bm_ref, wb_ref, wo_ref, fg_ref, o_ref, *, final_norm):
    hd = GDN_HEAD_DIM
    groups = [slice(r0, r0 + MERGE_SUB) for r0 in range(0, x_ref.shape[0], MERGE_SUB)]
    g_refs = (g0_ref, g1_ref, g2_ref)

    def branch_a(rs):
        return oa_ref[rs, :] * _silu(az_ref[rs, :].astype(F32))

    def branch_b(rs):
        normed = []
        for h in range(GDN_HEADS):
            oh = ob_ref[rs, h * hd:(h + 1) * hd]
            normed.append(oh * lax.rsqrt(jnp.mean(oh * oh, axis=-1, keepdims=True) + EPS))
        return jnp.concatenate(normed, axis=1) * gng_ref[...] * _silu(bz_ref[rs, :].astype(F32))

    def branch_m(rs):
        om = []
        for h in range(MEM_HEADS):
            cs = slice(h * MEM_HEAD_DIM, (h + 1) * MEM_HEAD_DIM)
            sc = _dot_nt(mq_ref[rs, cs], mk_ref[:, cs]) * (MEM_HEAD_DIM ** -0.5)
            p = jnp.exp(sc - jnp.max(sc, axis=-1, keepdims=True))
            om.append(jnp.dot(_bf(p), mv_ref[:, cs], preferred_element_type=F32)
                      / jnp.sum(p, axis=-1, keepdims=True))
        return jnp.concatenate(om, axis=1) * _silu(mz_ref[rs, :].astype(F32))

    merged = [None] * len(groups)
    for n, branch in enumerate((branch_a, branch_b, branch_m)):
        for k, rs in enumerate(groups):
            proj = jnp.dot(_bf(branch(rs)), wb_ref[n], preferred_element_type=F32)
            term = _sigmoid(g_refs[n][rs, :].astype(F32) + bm_ref[n:n + 1, :]) * proj
            merged[k] = term if merged[k] is None else merged[k] + term
    for k, rs in enumerate(groups):
        out = x_ref[rs, :] + jnp.dot(_bf(merged[k]), wo_ref[...], preferred_element_type=F32)
        if final_norm:
            out = out * lax.rsqrt(jnp.mean(out * out, axis=-1, keepdims=True) + EPS) * fg_ref[...]
        o_ref[rs, :] = out


def _merge(x, o_a, o_b, z, memkv, gng, bm, wb, wo, fg_row, layer, tm, final_norm):
    s = x.shape[0]
    zc = lambda col, width: pl.BlockSpec((tm, width), lambda i: (i, col // width))
    n_mem = memkv.shape[0]
    return pl.pallas_call(
        functools.partial(_merge_kernel, final_norm=final_norm),
        grid=(s // tm,),
        in_specs=[pl.BlockSpec((tm, D_MODEL), lambda i: (i, 0)),
                  pl.BlockSpec((tm, WIDTH), lambda i: (i, 0)),
                  pl.BlockSpec((tm, WIDTH), lambda i: (i, 0)),
                  zc(COL_MQ, WIDTH),
                  zc(COL_AZ, WIDTH), zc(COL_BZ, WIDTH), zc(COL_MZ, WIDTH),
                  zc(COL_GATES, D_MODEL), zc(COL_GATES + D_MODEL, D_MODEL),
                  zc(COL_GATES + 2 * D_MODEL, D_MODEL),
                  pl.BlockSpec((n_mem, WIDTH), lambda i: (0, 0)),
                  pl.BlockSpec((n_mem, WIDTH), lambda i: (0, 1)),
                  _layer_spec((1, WIDTH), layer, 1),
                  _layer_spec((N_BRANCH, D_MODEL), layer, 1),
                  _layer_spec((N_BRANCH, WIDTH, D_MODEL), layer, 1),
                  _layer_spec((D_MODEL, D_MODEL), layer, 1),
                  pl.BlockSpec((1, D_MODEL), lambda i: (0, 0))],
        out_specs=pl.BlockSpec((tm, D_MODEL), lambda i: (i, 0)),
        out_shape=jax.ShapeDtypeStruct((s, D_MODEL), F32),
        compiler_params=_cparams(("parallel",)),
        name="merge",
    )(x, o_a, o_b, z, z, z, z, z, z, z, memkv, memkv, gng, bm, wb, wo, fg_row)


def _main_row_start(c):
    col = c * WIDTH
    past_af = (col >= COL_AZ).astype(jnp.int32)
    past_ba = (col >= COL_BZ).astype(jnp.int32)
    return (c * (WIDTH // F32_SUBLANES) + past_af * (FOX_HEADS // F32_SUBLANES)
            + past_ba * (2 * GDN_HEADS // F32_SUBLANES))


NARROW_ROWS = ((SRC_AF, FOX_HEADS), (SRC_BA, 2 * GDN_HEADS))


def kernel(x, mem, norm_g, w_in, b_fg, b_merge, conv_w, a_log, dt_bias, gdn_norm_g, mem_norm_g,
           w_mem_kv, w_branch, w_out, final_norm_g):
    assert x.shape[0] == 1 and mem.shape[0] == 1
    depth = w_in.shape[0]
    s = x.shape[1]
    rows = lambda v: v.astype(F32).reshape(depth, 1, -1)
    assert w_in.shape[2] == N_SRC
    w_in_t = jnp.swapaxes(w_in, 1, 2)
    w_mem_t = jnp.swapaxes(w_mem_kv, 1, 2)
    w_branch_b, w_out_b = _bf(w_branch), _bf(w_out)
    norm_g3, mem_norm_g3 = rows(norm_g), rows(mem_norm_g)
    bfg3 = rows(jnp.pad(b_fg, ((0, 0), (0, LANES - FOX_HEADS))))
    aneg3 = rows(jnp.repeat(-jnp.exp(a_log.astype(F32)), GDN_HEAD_DIM, axis=1))
    dtb3 = rows(jnp.repeat(dt_bias, GDN_HEAD_DIM, axis=1))
    gng3 = rows(jnp.tile(gdn_norm_g, (1, GDN_HEADS)))
    bm3 = b_merge.reshape(depth, N_BRANCH, D_MODEL)
    fg_row = final_norm_g.reshape(1, D_MODEL)
    tq = min(s, ATTN_TQ)

    h = x[0]
    for l in range(depth):
        z, small = _norm_matmul(h, norm_g3, w_in_t, l, tm=min(s, NORM_TM), tn=NORM_TN,
                                n_out=N_MAIN, sub=WIDTH, row_start=_main_row_start,
                                narrow_rows=NARROW_ROWS)
        memkv, = _norm_matmul(mem[0], mem_norm_g3, w_mem_t, l, tm=mem.shape[1], tn=2 * WIDTH)
        qa, ka, va, stats = _attn_prep(z, small, bfg3, l, tm=min(s, ATTN_PREP_ROWS))
        o_a = _attention(_skip_starts(stats, ATTN_STAT_TILE, tq), qa, ka, va, tq=tq)
        qd, kd, u, w, aqk, egl = _gdn_prep(z, small, conv_w, aneg3, dtb3, l, rows=GDN_PREP_ROWS)
        o_b = _gdn_scan(qd, kd, u, w, aqk, egl)
        h = _merge(h, o_a, o_b, z, memkv, gng3, bm3, w_branch_b, w_out_b, fg_row, l,
                   tm=MERGE_TM, final_norm=(l == depth - 1))
    return h[None]
```

```python
import functools

import jax
import jax.numpy as jnp
import numpy as np
from jax import lax
from jax.experimental import pallas as pl
from jax.experimental.pallas import tpu as pltpu

F32 = jnp.float32
BF16 = jnp.bfloat16

D_MODEL = 1024
EPS = 1e-6
FOX_HEADS = 8
FOX_HEAD_DIM = 64
GDN_HEADS = 4
GDN_HEAD_DIM = 128
MEM_HEADS = 4
MEM_HEAD_DIM = 128
WIDTH = 512
N_BRANCH = 3
CHUNK = 64
CONV_K = 4
LANES = 128
F32_SUBLANES = 8
BF16_SUBLANES = 16
GDN_TILE = 256
CHUNKS_PER_TILE = GDN_TILE // CHUNK

SRC_AF = 3 * WIDTH
SRC_BA = SRC_AF + FOX_HEADS + 4 * WIDTH
N_SRC = 8208
COL_AQ, COL_AK, COL_AV, COL_AZ = 0, 512, 1024, 1536
COL_BQ, COL_BK, COL_BV, COL_BZ = 2048, 2560, 3072, 3584
COL_MQ, COL_MZ, COL_GATES = 4096, 4608, 5120
N_MAIN = 8192
SMALL_BA = 8
SMALL_BB = 12

NORM_TM = 2048
NORM_TN = 1024
ATTN_PREP_ROWS = 1024
ATTN_STAT_TILE = 256
ATTN_TQ = 512
GDN_PREP_ROWS = 512
MERGE_TM = 512
MERGE_SUB = 256

SKIP_LOG_MARGIN = 90.0
NORM_SLACK = 1.01

VMEM_LIMIT = 56 * 1024 * 1024


def _cparams(sem):
    return pltpu.CompilerParams(dimension_semantics=sem, vmem_limit_bytes=VMEM_LIMIT)


def _bf(x):
    return x.astype(BF16)


def _dot(a, b):
    return jnp.dot(_bf(a), _bf(b), preferred_element_type=F32)


def _dot_nt(a, b):
    return lax.dot_general(_bf(a), _bf(b), (((1,), (1,)), ((), ())), preferred_element_type=F32)


NEG_LOG2_E = -1.4426950408889634


def _sigmoid(x):
    return 1.0 / (1.0 + jnp.exp2(x * NEG_LOG2_E))


def _silu(x):
    return x * _sigmoid(x)


def _split3(x):
    hi = _bf(x).astype(F32)
    r1 = x - hi
    mid = _bf(r1).astype(F32)
    lo = _bf(r1 - mid).astype(F32)
    return hi, mid, lo


def _dot_exact_lhs(mat01, x):
    hi, mid, lo = _split3(x)
    return (jnp.dot(mat01, _bf(hi), preferred_element_type=F32)
            + jnp.dot(mat01, _bf(mid), preferred_element_type=F32)
            + jnp.dot(mat01, _bf(lo), preferred_element_type=F32))


def _layer_spec(shape, layer, n_grid):
    zeros = (0,) * len(shape)
    if n_grid == 1:
        return pl.BlockSpec((None,) + tuple(shape), lambda i: (layer,) + zeros)
    return pl.BlockSpec((None,) + tuple(shape), lambda i, j: (layer,) + zeros)


def _norm_matmul_kernel(x_ref, g_ref, *rest, n_w, n_narrow):
    w_refs, narrow_refs, outs = rest[:n_w], rest[n_w:n_w + n_narrow], rest[n_w + n_narrow:]
    if n_narrow:
        o_ref, os_ref, h_ref = outs
    else:
        o_ref, h_ref = outs

    @pl.when(pl.program_id(1) == 0)
    def _():
        x = x_ref[...]
        y = x * lax.rsqrt(jnp.mean(x * x, axis=-1, keepdims=True) + EPS)
        h = _bf(y * g_ref[...])
        h_ref[...] = h
        if n_narrow:
            rows = [r[...] for r in narrow_refs]
            used = sum(r.shape[0] for r in rows)
            rows.append(jnp.zeros((LANES - used, x.shape[1]), rows[0].dtype))
            os_ref[...] = _dot_nt(h, jnp.concatenate(rows, axis=0))

    sub = o_ref.shape[1] // n_w
    for k, w_ref in enumerate(w_refs):
        o_ref[:, k * sub:(k + 1) * sub] = _dot_nt(h_ref[...], w_ref[...]).astype(o_ref.dtype)


def _norm_matmul(x, g, w_t, layer, tm, tn, n_out=None, sub=None, row_start=None, narrow_rows=()):
    s, d = x.shape
    n_out = w_t.shape[1] if n_out is None else n_out
    sub = tn if sub is None else sub
    n_w = tn // sub
    n_src = w_t.shape[1]
    w_rows = w_t.reshape(w_t.shape[0] * n_src, d)
    unit = F32_SUBLANES
    window = lambda rows, start: pl.BlockSpec(
        (pl.Element(rows), pl.Element(d)),
        lambda i, j: ((layer * (n_src // unit) + start(j)) * unit, 0))
    if row_start is None:
        w_specs = [window(tn, lambda j: j * (tn // unit))]
    else:
        w_specs = [window(sub, lambda j, k=k: row_start(j * n_w + k)) for k in range(n_w)]
    narrow_specs = [window(cnt, lambda j, r=r: r // unit) for r, cnt in narrow_rows]
    out_specs = [pl.BlockSpec((tm, tn), lambda i, j: (i, j))]
    out_shape = [jax.ShapeDtypeStruct((s, n_out), BF16)]
    if narrow_rows:
        out_specs.append(pl.BlockSpec((tm, LANES), lambda i, j: (i, 0)))
        out_shape.append(jax.ShapeDtypeStruct((s, LANES), F32))
    return pl.pallas_call(
        functools.partial(_norm_matmul_kernel, n_w=n_w, n_narrow=len(narrow_rows)),
        grid=(s // tm, n_out // tn),
        in_specs=[pl.BlockSpec((tm, d), lambda i, j: (i, 0)), _layer_spec((1, d), layer, 2)]
        + w_specs + narrow_specs,
        out_specs=out_specs, out_shape=out_shape,
        scratch_shapes=[pltpu.VMEM((tm, d), BF16)],
        compiler_params=_cparams(("parallel", "arbitrary")),
        name="norm_matmul",
    )(x, g, *([w_rows] * (n_w + len(narrow_rows))))


FL_HI, FL_MID, FL_LO, FL_ONE = 0, 8, 16, 24
ST_FFIRST, ST_FLAST, ST_QN2, ST_KN2, ST_DMIN, ST_ROWS = 0, 1, 2, 6, 10, 16


def _aug_placement():
    d = FOX_HEAD_DIM
    wq, wk, wv = (np.zeros((FOX_HEADS // 2, 2 * LANES, 2 * LANES), np.float32) for _ in range(3))
    for p in range(FOX_HEADS // 2):
        for half in range(2):
            h = 2 * p + half
            feat = half * LANES + half * d
            aux = half * LANES + (1 - half) * d
            for w, scale in ((wq, d ** -0.5), (wk, 1.0), (wv, 1.0)):
                w[p, half * d + np.arange(d), feat + np.arange(d)] = scale
            for a, src in enumerate((FL_HI, FL_MID, FL_LO)):
                wq[p, LANES + src + h, aux + a] = 1.0
                wk[p, LANES + src + h, aux + 3 + a] = -1.0
            wq[p, LANES + FL_ONE, aux + 3:aux + 6] = 1.0
            wk[p, LANES + FL_ONE, aux:aux + 3] = 1.0
            wv[p, LANES + FL_ONE, aux:aux + d] = 1.0
    return [jnp.asarray(w, BF16) for w in (wq, wk, wv)]


def _attn_prep_kernel(q_ref, k_ref, v_ref, small_ref, bfg_ref, tril_ref, wq_ref, wk_ref, wv_ref,
                      nrm_ref, qa_ref, ka_ref, va_ref, stats_ref, carry_ref):
    rows = q_ref.shape[0]
    t = ATTN_STAT_TILE
    subs = [slice(r0, r0 + t) for r0 in range(0, rows, t)]

    @pl.when(pl.program_id(0) == 0)
    def _():
        carry_ref[...] = jnp.zeros_like(carry_ref)

    af = small_ref[...] + bfg_ref[...]
    logf = jnp.minimum(af, 0.0) - jnp.log1p(jnp.exp(-jnp.abs(af)))
    offset = carry_ref[...]
    cums = []
    for rs in subs:
        cums.append(_dot_exact_lhs(tril_ref[...], logf[rs, :]) + offset)
        offset = cums[-1][t - 1:t, :]
    carry_ref[...] = offset
    cum = jnp.concatenate(cums, axis=0)

    lane = lax.broadcasted_iota(jnp.int32, (rows, LANES), 1)
    hi, mid, lo = _split3(cum)
    fl = jnp.where(lane < FL_MID, hi,
                   jnp.where(lane < FL_LO, pltpu.roll(mid, FL_MID, 1),
                             jnp.where(lane < FL_ONE, pltpu.roll(lo, FL_LO, 1),
                                       jnp.where(lane == FL_ONE, 1.0, 0.0))))
    fl = _bf(fl)

    row = lax.broadcasted_iota(jnp.int32, (ST_ROWS, LANES), 0)
    put = lambda st, r, vec: jnp.where(row == r, jnp.broadcast_to(vec, (ST_ROWS, LANES)), st)
    stats = [put(put(jnp.zeros((ST_ROWS, LANES), F32), ST_FFIRST, c[0:1, :]),
                 ST_FLAST, c[t - 1:t, :]) for c in cums]
    for p in range(FOX_HEADS // 2):
        cols = slice(p * LANES, (p + 1) * LANES)
        for x_ref, w_ref, out_ref in ((q_ref, wq_ref, qa_ref), (k_ref, wk_ref, ka_ref),
                                      (v_ref, wv_ref, va_ref)):
            aug = jnp.dot(jnp.concatenate([x_ref[:, cols], fl], axis=1), w_ref[p],
                          preferred_element_type=F32)
            out_ref[2 * p] = _bf(aug[:, :LANES])
            out_ref[2 * p + 1] = _bf(aug[:, LANES:])
        xq, xk = q_ref[:, cols], k_ref[:, cols]
        n2 = jnp.dot(jnp.concatenate([xq * xq, xk * xk], axis=1), nrm_ref[...],
                     preferred_element_type=F32)
        diag = jnp.dot(xq * xk, nrm_ref[:LANES, :LANES], preferred_element_type=F32)
        for k, rs in enumerate(subs):
            top = jnp.max(n2[rs, :], axis=0, keepdims=True)
            stats[k] = put(put(stats[k], ST_QN2 + p, top[:, :LANES]), ST_KN2 + p, top[:, LANES:])
            stats[k] = put(stats[k], ST_DMIN + p, jnp.min(diag[rs, :], axis=0, keepdims=True))
    for k in range(len(subs)):
        stats_ref[k] = stats[k]


def _attn_prep(z, small, bfg, layer, tm):
    s = z.shape[0]
    t = ATTN_STAT_TILE
    tril = jnp.asarray(np.tril(np.ones((t, t), np.float32)), BF16)
    half = np.arange(2 * LANES) // FOX_HEAD_DIM
    nrm = jnp.asarray((half[:, None] == half[None, :]).astype(np.float32), BF16)
    places = _aug_placement()
    out_sds = jax.ShapeDtypeStruct((FOX_HEADS, s, LANES), BF16)
    out_spec = pl.BlockSpec((FOX_HEADS, tm, LANES), lambda i: (0, i, 0))
    const = lambda a: pl.BlockSpec(a.shape, lambda i: (0,) * a.ndim)
    return pl.pallas_call(
        _attn_prep_kernel,
        grid=(s // tm,),
        in_specs=[pl.BlockSpec((tm, WIDTH), lambda i: (i, COL_AQ // WIDTH)),
                  pl.BlockSpec((tm, WIDTH), lambda i: (i, COL_AK // WIDTH)),
                  pl.BlockSpec((tm, WIDTH), lambda i: (i, COL_AV // WIDTH)),
                  pl.BlockSpec((tm, LANES), lambda i: (i, 0)),
                  _layer_spec((1, LANES), layer, 1),
                  const(tril)] + [const(w) for w in places] + [const(nrm)],
        out_specs=[out_spec, out_spec, out_spec,
                   pl.BlockSpec((tm // t, ST_ROWS, LANES), lambda i: (i, 0, 0))],
        out_shape=[out_sds, out_sds, out_sds,
                   jax.ShapeDtypeStruct((s // t, ST_ROWS, LANES), F32)],
        scratch_shapes=[pltpu.VMEM((1, LANES), F32)],
        compiler_params=_cparams(("arbitrary",)),
        name="attn_prep",
    )(z, z, z, small, bfg, tril, *places, nrm)


def _attn_kernel(jlo_ref, q_ref, k_ref, v_ref, o_ref, s0_ref, s1_ref, s2_ref, s3_ref, *, tq, tk):
    pair = pl.program_id(0)
    i = pl.program_id(1)
    lane = lax.broadcasted_iota(jnp.int32, (tq, LANES), 1)
    n_tile = tk // LANES

    def logits(q, hh, j, s_ref):
        start = pl.multiple_of(j * tk, tk)
        s_ref[...] = lax.dot_general(q, k_ref[hh, pl.ds(start, tk), :], (((1,), (1,)), ((), ())),
                                     preferred_element_type=F32)

    def softmax_pv(hh, j, s_ref, carry, masked):
        m, acc = carry
        start = pl.multiple_of(j * tk, tk)
        if masked:
            row = lax.broadcasted_iota(jnp.int32, (tq, tk), 0)
            col = lax.broadcasted_iota(jnp.int32, (tq, tk), 1)
            s_ref[...] = jnp.where(col <= row, s_ref[...], -jnp.inf)
        m_new = jnp.maximum(m, jnp.max(s_ref[...], axis=1, keepdims=True))
        p = jnp.exp(s_ref[...] - jnp.tile(m_new, (1, n_tile)))
        alpha = jnp.exp(m - m_new)
        acc = alpha * acc + jnp.dot(_bf(p), v_ref[hh, pl.ds(start, tk), :],
                                    preferred_element_type=F32)
        return m_new, acc

    bufs = ((s0_ref, s1_ref), (s2_ref, s3_ref))
    jlos = [jlo_ref[(2 * pair + hh) * pl.num_programs(1) + i] for hh in range(2)]
    logits(q_ref[0], 0, jlos[0], bufs[0][0])
    accs = []
    for hh in range(2):
        q = q_ref[hh]
        jlo = jlos[hh]
        cur, nxt = bufs[hh]
        n_full = i - jlo

        def two_blocks(t, carry, q=q, hh=hh, jlo=jlo, cur=cur, nxt=nxt):
            j = jlo + 2 * t
            logits(q, hh, j + 1, nxt)
            carry = softmax_pv(hh, j, cur, carry, False)
            logits(q, hh, j + 2, cur)
            return softmax_pv(hh, j + 1, nxt, carry, False)

        def start_other_head(hh=hh):
            if hh == 0:
                logits(q_ref[1], 1, jlos[1], bufs[1][0])

        def tail_odd(carry, q=q, hh=hh, cur=cur, nxt=nxt, start=start_other_head):
            logits(q, hh, i, nxt)
            start()
            carry = softmax_pv(hh, i - 1, cur, carry, False)
            return softmax_pv(hh, i, nxt, carry, True)

        def tail_even(carry, hh=hh, cur=cur, start=start_other_head):
            start()
            return softmax_pv(hh, i, cur, carry, True)

        carry = (jnp.full((tq, LANES), -jnp.inf, F32), jnp.zeros((tq, LANES), F32))
        carry = lax.fori_loop(0, n_full // 2, two_blocks, carry)
        _, acc = lax.cond(n_full % 2 == 1, tail_odd, tail_even, carry)
        accs.append(acc)
    o0 = accs[0] / pltpu.roll(accs[0], FOX_HEAD_DIM, 1)
    o1 = accs[1] / pltpu.roll(accs[1], FOX_HEAD_DIM, 1)
    o_ref[...] = jnp.where(lane < FOX_HEAD_DIM, o0, o1)


def _skip_starts(stats, tm, tq):
    r = tq // tm
    nq = stats.shape[0] // r
    scale = FOX_HEAD_DIM ** -0.5
    st = stats.reshape(nq, r, ST_ROWS, LANES)
    per_head = lambda row0: jnp.stack(
        [st[:, :, row0 + h // 2, FOX_HEAD_DIM * (h % 2)] for h in range(FOX_HEADS)], axis=-1)
    f_first = st[:, 0, ST_FFIRST, :FOX_HEADS].T
    f_last = st[:, r - 1, ST_FLAST, :FOX_HEADS].T
    qn = jnp.sqrt(jnp.max(per_head(ST_QN2), axis=1)).T
    kn = jnp.sqrt(jnp.max(per_head(ST_KN2), axis=(0, 1)))[:, None]
    bound = NORM_SLACK * scale * qn * kn
    dmin = scale * jnp.min(per_head(ST_DMIN), axis=1).T - (NORM_SLACK - 1.0) * bound
    thresh = f_first + bound - dmin + SKIP_LOG_MARGIN
    need = f_last[:, None, :] <= thresh[:, :, None]
    idx = jnp.arange(nq)
    first = jnp.min(jnp.where(need, idx[None, None, :], nq), axis=-1)
    return jnp.minimum(first, idx[None, :]).astype(jnp.int32).reshape(-1)


def _attention(jlo, qa, ka, va, tq):
    s = qa.shape[1]
    tk = tq
    kv_spec = pl.BlockSpec((2, s, LANES), lambda p, i, jl: (p, 0, 0))
    return pl.pallas_call(
        functools.partial(_attn_kernel, tq=tq, tk=tk),
        grid_spec=pltpu.PrefetchScalarGridSpec(
            num_scalar_prefetch=1,
            grid=(FOX_HEADS // 2, s // tq),
            in_specs=[pl.BlockSpec((2, tq, LANES), lambda p, i, jl: (p, i, 0)), kv_spec, kv_spec],
            out_specs=pl.BlockSpec((tq, LANES), lambda p, i, jl: (i, p)),
            scratch_shapes=[pltpu.VMEM((tq, tk), F32)] * 4),
        out_shape=jax.ShapeDtypeStruct((s, WIDTH), F32),
        compiler_params=_cparams(("parallel", "arbitrary")),
        name="fox_attention",
    )(jlo, qa, ka, va)


def _chunk_cumsum_matrix():
    r = np.arange(GDN_TILE)
    same = (r[:, None] // CHUNK) == (r[None, :] // CHUNK)
    return jnp.asarray((same & (r[None, :] <= r[:, None])).astype(np.float32), BF16)


def _gdn_prep_kernel(xq_ref, xk_ref, xv_ref, hq_ref, hk_ref, hv_ref, small_ref, cw_ref, aneg_ref,
                     dtb_ref, kl_ref, qd_ref, kd_ref, u_ref, w_ref, aqk_ref, egl_ref, xx_ref):
    t = GDN_TILE
    hd = GDN_HEAD_DIM
    pad = BF16_SUBLANES
    rows_in = xq_ref.shape[0]

    for n, (x_ref, halo_ref) in enumerate(((xq_ref, hq_ref), (xk_ref, hk_ref), (xv_ref, hv_ref))):
        cs = slice(n * WIDTH, (n + 1) * WIDTH)
        halo = halo_ref[...].astype(F32)
        xx_ref[0:pad, cs] = jnp.where(pl.program_id(0) == 0, jnp.zeros_like(halo), halo)
        xx_ref[pad:pad + rows_in, cs] = x_ref[...].astype(F32)

    row = lax.broadcasted_iota(jnp.int32, (t, t), 0)
    col = lax.broadcasted_iota(jnp.int32, (t, t), 1)
    same = (row // CHUNK) == (col // CHUNK)
    incl = jnp.logical_and(same, col <= row)
    strict = jnp.logical_and(same, col < row)
    eye = (row == col).astype(F32)
    row8 = lax.broadcasted_iota(jnp.int32, (8, WIDTH), 0)

    invs, bps, rhss, dests = [], [], [], []
    for tile in range(rows_in // t):
        r0 = tile * t
        rs = slice(r0, r0 + t)
        conv = cw_ref[CONV_K - 1:CONV_K, :] * xx_ref[pad + r0:pad + r0 + t, :]
        for j in range(CONV_K - 1):
            off = pad + r0 - (CONV_K - 1) + j
            conv = conv + cw_ref[j:j + 1, :] * xx_ref[off:off + t, :]
        qkv = _silu(conv)

        small = small_ref[rs, :]
        bcast = lambda c0: jnp.concatenate(
            [jnp.broadcast_to(small[:, c0 + h:c0 + h + 1], (t, hd)) for h in range(GDN_HEADS)],
            axis=1)
        sp_in = bcast(SMALL_BA) + dtb_ref[...]
        softplus = jnp.maximum(sp_in, 0.0) + jnp.log1p(jnp.exp(-jnp.abs(sp_in)))
        g = aneg_ref[...] * softplus
        beta = _sigmoid(bcast(SMALL_BB))
        gcum = _dot_exact_lhs(kl_ref[...], g)
        glast = jnp.concatenate(
            [jnp.broadcast_to(gcum[(c + 1) * CHUNK - 1:(c + 1) * CHUNK, :], (CHUNK, WIDTH))
             for c in range(CHUNKS_PER_TILE)], axis=0)
        eg = jnp.exp(gcum)
        egl = jnp.exp(glast)
        egd = jnp.exp(glast - gcum)
        egl_rows = jnp.zeros((8, WIDTH), F32)
        for c in range(CHUNKS_PER_TILE):
            egl_rows = jnp.where(row8 == c, egl[c * CHUNK:c * CHUNK + 8, :], egl_rows)
        egl_ref[tile * 8:(tile + 1) * 8, :] = egl_rows

        for h in range(GDN_HEADS):
            cs = slice(h * hd, (h + 1) * hd)
            qh = qkv[:, h * hd:(h + 1) * hd]
            kh = qkv[:, WIDTH + h * hd:WIDTH + (h + 1) * hd]
            vh = qkv[:, 2 * WIDTH + h * hd:2 * WIDTH + (h + 1) * hd]
            qh = qh * lax.rsqrt(jnp.sum(qh * qh, axis=-1, keepdims=True) + EPS) * (hd ** -0.5)
            kh = kh * lax.rsqrt(jnp.sum(kh * kh, axis=-1, keepdims=True) + EPS)
            bh = beta[:, cs]
            gc = gcum[:, cs]
            dmat = jnp.concatenate([gc, gc], axis=1) - gc.T[0:1, :]
            gamma = jnp.exp(jnp.where(incl, dmat, -jnp.inf))
            kb = kh * bh
            a = jnp.where(strict, _dot_nt(kb, kh) * gamma, 0.0)
            invs.append(eye - a)
            bps.append(_bf(-a))
            rhss.append(_bf(jnp.concatenate([vh * bh, kb * eg[:, cs]], axis=1)))
            aqk_ref[h, rs, :] = _bf(_dot_nt(qh, kh) * gamma)
            qd_ref[rs, cs] = _bf(qh * eg[:, cs])
            kd_ref[rs, cs] = _bf(kh * egd[:, cs])
            dests.append((rs, cs))

    for _ in range(5):
        bps = [_bf(jnp.dot(b, b, preferred_element_type=F32)) for b in bps]
        invs = [inv + jnp.dot(_bf(inv), b, preferred_element_type=F32)
                for inv, b in zip(invs, bps)]
    for inv, rhs, (rs, cs) in zip(invs, rhss, dests):
        sol = jnp.dot(_bf(inv), rhs, preferred_element_type=F32)
        u_ref[rs, cs] = sol[:, :hd]
        w_ref[rs, cs] = _bf(sol[:, hd:])


def _gdn_prep(z, small, conv_w, aneg, dtb, layer, rows):
    s = z.shape[0]
    t = GDN_TILE
    pad = BF16_SUBLANES
    tiles = rows // t
    kl = _chunk_cumsum_matrix()
    row_spec = pl.BlockSpec((rows, WIDTH), lambda i: (i, 0))
    halo_spec = lambda col: pl.BlockSpec(
        (pad, WIDTH), lambda i: (jnp.maximum(i * (rows // pad) - 1, 0), col // WIDTH))
    return pl.pallas_call(
        _gdn_prep_kernel,
        grid=(s // rows,),
        in_specs=[pl.BlockSpec((rows, WIDTH), lambda i: (i, COL_BQ // WIDTH)),
                  pl.BlockSpec((rows, WIDTH), lambda i: (i, COL_BK // WIDTH)),
                  pl.BlockSpec((rows, WIDTH), lambda i: (i, COL_BV // WIDTH)),
                  halo_spec(COL_BQ), halo_spec(COL_BK), halo_spec(COL_BV),
                  pl.BlockSpec((rows, LANES), lambda i: (i, 0)),
                  _layer_spec((CONV_K, 3 * WIDTH), layer, 1),
                  _layer_spec((1, WIDTH), layer, 1),
                  _layer_spec((1, WIDTH), layer, 1),
                  pl.BlockSpec((t, t), lambda i: (0, 0))],
        out_specs=[row_spec, row_spec, row_spec, row_spec,
                   pl.BlockSpec((GDN_HEADS, rows, t), lambda i: (0, i, 0)),
                   pl.BlockSpec((8 * tiles, WIDTH), lambda i: (i, 0))],
        out_shape=[jax.ShapeDtypeStruct((s, WIDTH), BF16),
                   jax.ShapeDtypeStruct((s, WIDTH), BF16),
                   jax.ShapeDtypeStruct((s, WIDTH), F32),
                   jax.ShapeDtypeStruct((s, WIDTH), BF16),
                   jax.ShapeDtypeStruct((GDN_HEADS, s, t), BF16),
                   jax.ShapeDtypeStruct((s // t * 8, WIDTH), F32)],
        scratch_shapes=[pltpu.VMEM((rows + pad, 3 * WIDTH), F32)],
        compiler_params=_cparams(("parallel",)),
        name="gdn_prep",
    )(z, z, z, z, z, z, small, conv_w, aneg, dtb, kl)


def _gdn_scan_kernel(qd_ref, kd_ref, u_ref, w_ref, aqk_ref, egl_ref, o_ref, state_ref, vn_ref):
    hd = GDN_HEAD_DIM

    @pl.when(pl.program_id(0) == 0)
    def _():
        state_ref[...] = jnp.zeros_like(state_ref)
        vn_ref[...] = jnp.zeros_like(vn_ref)

    heads = range(GDN_HEADS)
    col = lambda h: slice(h * hd, (h + 1) * hd)
    for c in range(CHUNKS_PER_TILE):
        rs = slice(c * CHUNK, (c + 1) * CHUNK)
        sts = [state_ref[h] for h in heads]
        rr = [jnp.dot(jnp.concatenate([w_ref[rs, col(h)], qd_ref[rs, col(h)]], axis=0),
                      _bf(sts[h]), preferred_element_type=F32) for h in heads]
        vns = [_bf(u_ref[rs, col(h)] - rr[h][:CHUNK]) for h in heads]
        for h in heads:
            vn_ref[h, rs, :] = vns[h]
        for h in heads:
            upd = lax.dot_general(kd_ref[rs, col(h)], vns[h], (((0,), (0,)), ((), ())),
                                  preferred_element_type=F32)
            state_ref[h] = sts[h] * egl_ref[c:c + 1, col(h)] + upd
        for h in heads:
            o_ref[rs, col(h)] = rr[h][CHUNK:] + jnp.dot(aqk_ref[h, rs, :], vn_ref[h],
                                                        preferred_element_type=F32)


def _gdn_scan(qd, kd, u, w, aqk, egl):
    s = qd.shape[0]
    t = GDN_TILE
    row_spec = pl.BlockSpec((t, WIDTH), lambda i: (i, 0))
    return pl.pallas_call(
        _gdn_scan_kernel,
        grid=(s // t,),
        in_specs=[row_spec, row_spec, row_spec, row_spec,
                  pl.BlockSpec((GDN_HEADS, t, t), lambda i: (0, i, 0)),
                  pl.BlockSpec((8, WIDTH), lambda i: (i, 0))],
        out_specs=row_spec,
        out_shape=jax.ShapeDtypeStruct((s, WIDTH), F32),
        scratch_shapes=[pltpu.VMEM((GDN_HEADS, GDN_HEAD_DIM, GDN_HEAD_DIM), F32),
                        pltpu.VMEM((GDN_HEADS, t, GDN_HEAD_DIM), BF16)],
        compiler_params=_cparams(("arbitrary",)),
        name="gdn_scan",
    )(qd, kd, u, w, aqk, egl)


def _merge_kernel(x_ref, oa_ref, ob_ref, mq_ref, az_ref, bz_ref, mz_ref, g0_ref, g1_ref, g2_ref,
                  mk_ref, mv_ref, gng_ref, bm_ref, wb_ref, wo_ref, fg_ref, o_ref, *, final_norm):
    hd = GDN_HEAD_DIM
    groups = [slice(r0, r0 + MERGE_SUB) for r0 in range(0, x_ref.shape[0], MERGE_SUB)]
    g_refs = (g0_ref, g1_ref, g2_ref)

    def branch_a(rs):
        return oa_ref[rs, :] * _silu(az_ref[rs, :].astype(F32))

    def branch_b(rs):
        normed = []
        for h in range(GDN_HEADS):
            oh = ob_ref[rs, h * hd:(h + 1) * hd]
            normed.append(oh * lax.rsqrt(jnp.mean(oh * oh, axis=-1, keepdims=True) + EPS))
        return jnp.concatenate(normed, axis=1) * gng_ref[...] * _silu(bz_ref[rs, :].astype(F32))

    def branch_m(rs):
        om = []
        for h in range(MEM_HEADS):
            cs = slice(h * MEM_HEAD_DIM, (h + 1) * MEM_HEAD_DIM)
            sc = _dot_nt(mq_ref[rs, cs], mk_ref[:, cs])
            p = jnp.exp2((sc - jnp.max(sc, axis=-1, keepdims=True))
                         * (-NEG_LOG2_E * MEM_HEAD_DIM ** -0.5))
            om.append(jnp.dot(_bf(p), mv_ref[:, cs], preferred_element_type=F32)
                      / jnp.sum(p, axis=-1, keepdims=True))
        return jnp.concatenate(om, axis=1) * _silu(mz_ref[rs, :].astype(F32))

    merged = [None] * len(groups)
    for n, branch in enumerate((branch_a, branch_b, branch_m)):
        for k, rs in enumerate(groups):
            proj = jnp.dot(_bf(branch(rs)), wb_ref[n], preferred_element_type=F32)
            term = _sigmoid(g_refs[n][rs, :].astype(F32) + bm_ref[n:n + 1, :]) * proj
            merged[k] = term if merged[k] is None else merged[k] + term
    for k, rs in enumerate(groups):
        out = x_ref[rs, :] + jnp.dot(_bf(merged[k]), wo_ref[...], preferred_element_type=F32)
        if final_norm:
            out = out * lax.rsqrt(jnp.mean(out * out, axis=-1, keepdims=True) + EPS) * fg_ref[...]
        o_ref[rs, :] = out


def _merge(x, o_a, o_b, z, memkv, gng, bm, wb, wo, fg_row, layer, tm, final_norm):
    s = x.shape[0]
    zc = lambda col, width: pl.BlockSpec((tm, width), lambda i: (i, col // width))
    n_mem = memkv.shape[0]
    return pl.pallas_call(
        functools.partial(_merge_kernel, final_norm=final_norm),
        grid=(s // tm,),
        in_specs=[pl.BlockSpec((tm, D_MODEL), lambda i: (i, 0)),
                  pl.BlockSpec((tm, WIDTH), lambda i: (i, 0)),
                  pl.BlockSpec((tm, WIDTH), lambda i: (i, 0)),
                  zc(COL_MQ, WIDTH),
                  zc(COL_AZ, WIDTH), zc(COL_BZ, WIDTH), zc(COL_MZ, WIDTH),
                  zc(COL_GATES, D_MODEL), zc(COL_GATES + D_MODEL, D_MODEL),
                  zc(COL_GATES + 2 * D_MODEL, D_MODEL),
                  pl.BlockSpec((n_mem, WIDTH), lambda i: (0, 0)),
                  pl.BlockSpec((n_mem, WIDTH), lambda i: (0, 1)),
                  _layer_spec((1, WIDTH), layer, 1),
                  _layer_spec((N_BRANCH, D_MODEL), layer, 1),
                  _layer_spec((N_BRANCH, WIDTH, D_MODEL), layer, 1),
                  _layer_spec((D_MODEL, D_MODEL), layer, 1),
                  pl.BlockSpec((1, D_MODEL), lambda i: (0, 0))],
        out_specs=pl.BlockSpec((tm, D_MODEL), lambda i: (i, 0)),
        out_shape=jax.ShapeDtypeStruct((s, D_MODEL), F32),
        compiler_params=_cparams(("parallel",)),
        name="merge",
    )(x, o_a, o_b, z, z, z, z, z, z, z, memkv, memkv, gng, bm, wb, wo, fg_row)


def _main_row_start(c):
    col = c * WIDTH
    past_af = (col >= COL_AZ).astype(jnp.int32)
    past_ba = (col >= COL_BZ).astype(jnp.int32)
    return (c * (WIDTH // F32_SUBLANES) + past_af * (FOX_HEADS // F32_SUBLANES)
            + past_ba * (2 * GDN_HEADS // F32_SUBLANES))


NARROW_ROWS = ((SRC_AF, FOX_HEADS), (SRC_BA, 2 * GDN_HEADS))


def kernel(x, mem, norm_g, w_in, b_fg, b_merge, conv_w, a_log, dt_bias, gdn_norm_g, mem_norm_g,
           w_mem_kv, w_branch, w_out, final_norm_g):
    assert x.shape[0] == 1 and mem.shape[0] == 1
    depth = w_in.shape[0]
    s = x.shape[1]
    rows = lambda v: v.astype(F32).reshape(depth, 1, -1)
    assert w_in.shape[2] == N_SRC
    w_in_t = jnp.swapaxes(w_in, 1, 2)
    w_mem_t = jnp.swapaxes(w_mem_kv, 1, 2)
    w_branch_b, w_out_b = _bf(w_branch), _bf(w_out)
    norm_g3, mem_norm_g3 = rows(norm_g), rows(mem_norm_g)
    bfg3 = rows(jnp.pad(b_fg, ((0, 0), (0, LANES - FOX_HEADS))))
    aneg3 = rows(jnp.repeat(-jnp.exp(a_log.astype(F32)), GDN_HEAD_DIM, axis=1))
    dtb3 = rows(jnp.repeat(dt_bias, GDN_HEAD_DIM, axis=1))
    gng3 = rows(jnp.tile(gdn_norm_g, (1, GDN_HEADS)))
    bm3 = b_merge.reshape(depth, N_BRANCH, D_MODEL)
    fg_row = final_norm_g.reshape(1, D_MODEL)
    tq = min(s, ATTN_TQ)

    h = x[0]
    for l in range(depth):
        z, small = _norm_matmul(h, norm_g3, w_in_t, l, tm=min(s, NORM_TM), tn=NORM_TN,
                                n_out=N_MAIN, sub=WIDTH, row_start=_main_row_start,
                                narrow_rows=NARROW_ROWS)
        memkv, = _norm_matmul(mem[0], mem_norm_g3, w_mem_t, l, tm=mem.shape[1], tn=2 * WIDTH)
        qa, ka, va, stats = _attn_prep(z, small, bfg3, l, tm=min(s, ATTN_PREP_ROWS))
        o_a = _attention(_skip_starts(stats, ATTN_STAT_TILE, tq), qa, ka, va, tq=tq)
        qd, kd, u, w, aqk, egl = _gdn_prep(z, small, conv_w, aneg3, dtb3, l, rows=GDN_PREP_ROWS)
        o_b = _gdn_scan(qd, kd, u, w, aqk, egl)
        h = _merge(h, o_a, o_b, z, memkv, gng3, bm3, w_branch_b, w_out_b, fg_row, l,
                   tm=MERGE_TM, final_norm=(l == depth - 1))
    return h[None]
```

```python
import functools

import jax
import jax.numpy as jnp
import numpy as np
from jax import lax
from jax.experimental import pallas as pl
from jax.experimental.pallas import tpu as pltpu

F32 = jnp.float32
BF16 = jnp.bfloat16

D_MODEL = 1024
EPS = 1e-6
FOX_HEADS = 8
FOX_HEAD_DIM = 64
GDN_HEADS = 4
GDN_HEAD_DIM = 128
MEM_HEADS = 4
MEM_HEAD_DIM = 128
WIDTH = 512
N_BRANCH = 3
CHUNK = 64
CONV_K = 4
LANES = 128
F32_SUBLANES = 8
BF16_SUBLANES = 16
GDN_TILE = 256
CHUNKS_PER_TILE = GDN_TILE // CHUNK

SRC_AF = 3 * WIDTH
SRC_BA = SRC_AF + FOX_HEADS + 4 * WIDTH
N_SRC = 8208
COL_AQ, COL_AK, COL_AV, COL_AZ = 0, 512, 1024, 1536
COL_BQ, COL_BK, COL_BV, COL_BZ = 2048, 2560, 3072, 3584
COL_MQ, COL_MZ, COL_GATES = 4096, 4608, 5120
N_MAIN = 8192
SMALL_BA = 8
SMALL_BB = 12

NORM_TM = 2048
NORM_TN = 1024
ATTN_PREP_ROWS = 1024
ATTN_STAT_TILE = 256
ATTN_TQ = 512
GDN_PREP_ROWS = 512
MERGE_TM = 512
MERGE_SUB = 512

SKIP_LOG_MARGIN = 90.0
NORM_SLACK = 1.01

VMEM_LIMIT = 56 * 1024 * 1024


def _cparams(sem):
    return pltpu.CompilerParams(dimension_semantics=sem, vmem_limit_bytes=VMEM_LIMIT)


def _bf(x):
    return x.astype(BF16)


def _dot(a, b):
    return jnp.dot(_bf(a), _bf(b), preferred_element_type=F32)


def _dot_nt(a, b):
    return lax.dot_general(_bf(a), _bf(b), (((1,), (1,)), ((), ())), preferred_element_type=F32)


NEG_LOG2_E = -1.4426950408889634


def _sigmoid(x):
    return 1.0 / (1.0 + jnp.exp2(x * NEG_LOG2_E))


def _silu(x):
    return x * _sigmoid(x)


def _split3(x):
    hi = _bf(x).astype(F32)
    r1 = x - hi
    mid = _bf(r1).astype(F32)
    lo = _bf(r1 - mid).astype(F32)
    return hi, mid, lo


def _dot_exact_lhs(mat01, x):
    hi, mid, lo = _split3(x)
    return (jnp.dot(mat01, _bf(hi), preferred_element_type=F32)
            + jnp.dot(mat01, _bf(mid), preferred_element_type=F32)
            + jnp.dot(mat01, _bf(lo), preferred_element_type=F32))


def _layer_spec(shape, layer, n_grid):
    zeros = (0,) * len(shape)
    if n_grid == 1:
        return pl.BlockSpec((None,) + tuple(shape), lambda i: (layer,) + zeros)
    return pl.BlockSpec((None,) + tuple(shape), lambda i, j: (layer,) + zeros)


def _norm_matmul_kernel(x_ref, g_ref, *rest, n_w, n_narrow):
    w_refs, narrow_refs, outs = rest[:n_w], rest[n_w:n_w + n_narrow], rest[n_w + n_narrow:]
    if n_narrow:
        o_ref, os_ref, h_ref = outs
    else:
        o_ref, h_ref = outs

    @pl.when(pl.program_id(1) == 0)
    def _():
        x = x_ref[...]
        y = x * lax.rsqrt(jnp.mean(x * x, axis=-1, keepdims=True) + EPS)
        h = _bf(y * g_ref[...])
        h_ref[...] = h
        if n_narrow:
            rows = [r[...] for r in narrow_refs]
            used = sum(r.shape[0] for r in rows)
            rows.append(jnp.zeros((LANES - used, x.shape[1]), rows[0].dtype))
            os_ref[...] = _dot_nt(h, jnp.concatenate(rows, axis=0))

    sub = o_ref.shape[1] // n_w
    for k, w_ref in enumerate(w_refs):
        o_ref[:, k * sub:(k + 1) * sub] = _dot_nt(h_ref[...], w_ref[...]).astype(o_ref.dtype)


def _norm_matmul(x, g, w_t, layer, tm, tn, n_out=None, sub=None, row_start=None, narrow_rows=()):
    s, d = x.shape
    n_out = w_t.shape[1] if n_out is None else n_out
    sub = tn if sub is None else sub
    n_w = tn // sub
    n_src = w_t.shape[1]
    w_rows = w_t.reshape(w_t.shape[0] * n_src, d)
    unit = F32_SUBLANES
    window = lambda rows, start: pl.BlockSpec(
        (pl.Element(rows), pl.Element(d)),
        lambda i, j: ((layer * (n_src // unit) + start(j)) * unit, 0))
    if row_start is None:
        w_specs = [window(tn, lambda j: j * (tn // unit))]
    else:
        w_specs = [window(sub, lambda j, k=k: row_start(j * n_w + k)) for k in range(n_w)]
    narrow_specs = [window(cnt, lambda j, r=r: r // unit) for r, cnt in narrow_rows]
    out_specs = [pl.BlockSpec((tm, tn), lambda i, j: (i, j))]
    out_shape = [jax.ShapeDtypeStruct((s, n_out), BF16)]
    if narrow_rows:
        out_specs.append(pl.BlockSpec((tm, LANES), lambda i, j: (i, 0)))
        out_shape.append(jax.ShapeDtypeStruct((s, LANES), F32))
    return pl.pallas_call(
        functools.partial(_norm_matmul_kernel, n_w=n_w, n_narrow=len(narrow_rows)),
        grid=(s // tm, n_out // tn),
        in_specs=[pl.BlockSpec((tm, d), lambda i, j: (i, 0)), _layer_spec((1, d), layer, 2)]
        + w_specs + narrow_specs,
        out_specs=out_specs, out_shape=out_shape,
        scratch_shapes=[pltpu.VMEM((tm, d), BF16)],
        compiler_params=_cparams(("parallel", "arbitrary")),
        name="norm_matmul",
    )(x, g, *([w_rows] * (n_w + len(narrow_rows))))


FL_HI, FL_MID, FL_LO, FL_ONE = 0, 8, 16, 24
ST_FFIRST, ST_FLAST, ST_QN2, ST_KN2, ST_DMIN, ST_ROWS = 0, 1, 2, 6, 10, 16


def _aug_placement():
    d = FOX_HEAD_DIM
    wq, wk, wv = (np.zeros((FOX_HEADS // 2, 2 * LANES, 2 * LANES), np.float32) for _ in range(3))
    for p in range(FOX_HEADS // 2):
        for half in range(2):
            h = 2 * p + half
            feat = half * LANES + half * d
            aux = half * LANES + (1 - half) * d
            for w, scale in ((wq, d ** -0.5), (wk, 1.0), (wv, 1.0)):
                w[p, half * d + np.arange(d), feat + np.arange(d)] = scale
            for a, src in enumerate((FL_HI, FL_MID, FL_LO)):
                wq[p, LANES + src + h, aux + a] = 1.0
                wk[p, LANES + src + h, aux + 3 + a] = -1.0
            wq[p, LANES + FL_ONE, aux + 3:aux + 6] = 1.0
            wk[p, LANES + FL_ONE, aux:aux + 3] = 1.0
            wv[p, LANES + FL_ONE, aux:aux + d] = 1.0
    return [jnp.asarray(w, BF16) for w in (wq, wk, wv)]


def _attn_prep_kernel(q_ref, k_ref, v_ref, small_ref, bfg_ref, tril_ref, wq_ref, wk_ref, wv_ref,
                      nrm_ref, qa_ref, ka_ref, va_ref, stats_ref, carry_ref):
    rows = q_ref.shape[0]
    t = ATTN_STAT_TILE
    subs = [slice(r0, r0 + t) for r0 in range(0, rows, t)]

    @pl.when(pl.program_id(0) == 0)
    def _():
        carry_ref[...] = jnp.zeros_like(carry_ref)

    af = small_ref[...] + bfg_ref[...]
    logf = jnp.minimum(af, 0.0) - jnp.log1p(jnp.exp(-jnp.abs(af)))
    offset = carry_ref[...]
    cums = []
    for rs in subs:
        cums.append(_dot_exact_lhs(tril_ref[...], logf[rs, :]) + offset)
        offset = cums[-1][t - 1:t, :]
    carry_ref[...] = offset
    cum = jnp.concatenate(cums, axis=0)

    lane = lax.broadcasted_iota(jnp.int32, (rows, LANES), 1)
    hi, mid, lo = _split3(cum)
    fl = jnp.where(lane < FL_MID, hi,
                   jnp.where(lane < FL_LO, pltpu.roll(mid, FL_MID, 1),
                             jnp.where(lane < FL_ONE, pltpu.roll(lo, FL_LO, 1),
                                       jnp.where(lane == FL_ONE, 1.0, 0.0))))
    fl = _bf(fl)

    row = lax.broadcasted_iota(jnp.int32, (ST_ROWS, LANES), 0)
    put = lambda st, r, vec: jnp.where(row == r, jnp.broadcast_to(vec, (ST_ROWS, LANES)), st)
    stats = [put(put(jnp.zeros((ST_ROWS, LANES), F32), ST_FFIRST, c[0:1, :]),
                 ST_FLAST, c[t - 1:t, :]) for c in cums]
    for p in range(FOX_HEADS // 2):
        cols = slice(p * LANES, (p + 1) * LANES)
        for x_ref, w_ref, out_ref in ((q_ref, wq_ref, qa_ref), (k_ref, wk_ref, ka_ref),
                                      (v_ref, wv_ref, va_ref)):
            aug = jnp.dot(jnp.concatenate([x_ref[:, cols], fl], axis=1), w_ref[p],
                          preferred_element_type=F32)
            out_ref[2 * p] = _bf(aug[:, :LANES])
            out_ref[2 * p + 1] = _bf(aug[:, LANES:])
        xq, xk = q_ref[:, cols], k_ref[:, cols]
        n2 = jnp.dot(jnp.concatenate([xq * xq, xk * xk], axis=1), nrm_ref[...],
                     preferred_element_type=F32)
        diag = jnp.dot(xq * xk, nrm_ref[:LANES, :LANES], preferred_element_type=F32)
        for k, rs in enumerate(subs):
            top = jnp.max(n2[rs, :], axis=0, keepdims=True)
            stats[k] = put(put(stats[k], ST_QN2 + p, top[:, :LANES]), ST_KN2 + p, top[:, LANES:])
            stats[k] = put(stats[k], ST_DMIN + p, jnp.min(diag[rs, :], axis=0, keepdims=True))
    for k in range(len(subs)):
        stats_ref[k] = stats[k]


def _attn_prep(z, small, bfg, layer, tm):
    s = z.shape[0]
    t = ATTN_STAT_TILE
    tril = jnp.asarray(np.tril(np.ones((t, t), np.float32)), BF16)
    half = np.arange(2 * LANES) // FOX_HEAD_DIM
    nrm = jnp.asarray((half[:, None] == half[None, :]).astype(np.float32), BF16)
    places = _aug_placement()
    out_sds = jax.ShapeDtypeStruct((FOX_HEADS, s, LANES), BF16)
    out_spec = pl.BlockSpec((FOX_HEADS, tm, LANES), lambda i: (0, i, 0))
    const = lambda a: pl.BlockSpec(a.shape, lambda i: (0,) * a.ndim)
    return pl.pallas_call(
        _attn_prep_kernel,
        grid=(s // tm,),
        in_specs=[pl.BlockSpec((tm, WIDTH), lambda i: (i, COL_AQ // WIDTH)),
                  pl.BlockSpec((tm, WIDTH), lambda i: (i, COL_AK // WIDTH)),
                  pl.BlockSpec((tm, WIDTH), lambda i: (i, COL_AV // WIDTH)),
                  pl.BlockSpec((tm, LANES), lambda i: (i, 0)),
                  _layer_spec((1, LANES), layer, 1),
                  const(tril)] + [const(w) for w in places] + [const(nrm)],
        out_specs=[out_spec, out_spec, out_spec,
                   pl.BlockSpec((tm // t, ST_ROWS, LANES), lambda i: (i, 0, 0))],
        out_shape=[out_sds, out_sds, out_sds,
                   jax.ShapeDtypeStruct((s // t, ST_ROWS, LANES), F32)],
        scratch_shapes=[pltpu.VMEM((1, LANES), F32)],
        compiler_params=_cparams(("arbitrary",)),
        name="attn_prep",
    )(z, z, z, small, bfg, tril, *places, nrm)


def _attn_kernel(jlo_ref, q_ref, k_ref, v_ref, o_ref, s0_ref, s1_ref, s2_ref, s3_ref, *, tq, tk):
    pair = pl.program_id(0)
    i = pl.program_id(1)
    lane = lax.broadcasted_iota(jnp.int32, (tq, LANES), 1)
    n_tile = tk // LANES

    def logits(q, hh, j, s_ref):
        start = pl.multiple_of(j * tk, tk)
        s_ref[...] = lax.dot_general(q, k_ref[hh, pl.ds(start, tk), :], (((1,), (1,)), ((), ())),
                                     preferred_element_type=F32)

    def softmax_pv(hh, j, s_ref, carry, masked):
        m, acc = carry
        start = pl.multiple_of(j * tk, tk)
        if masked:
            row = lax.broadcasted_iota(jnp.int32, (tq, tk), 0)
            col = lax.broadcasted_iota(jnp.int32, (tq, tk), 1)
            s_ref[...] = jnp.where(col <= row, s_ref[...], -jnp.inf)
        m_new = jnp.maximum(m, jnp.max(s_ref[...], axis=1, keepdims=True))
        p = jnp.exp(s_ref[...] - jnp.tile(m_new, (1, n_tile)))
        alpha = jnp.exp(m - m_new)
        acc = alpha * acc + jnp.dot(_bf(p), v_ref[hh, pl.ds(start, tk), :],
                                    preferred_element_type=F32)
        return m_new, acc

    bufs = ((s0_ref, s1_ref), (s2_ref, s3_ref))
    jlos = [jlo_ref[(2 * pair + hh) * pl.num_programs(1) + i] for hh in range(2)]
    logits(q_ref[0], 0, jlos[0], bufs[0][0])
    accs = []
    for hh in range(2):
        q = q_ref[hh]
        jlo = jlos[hh]
        cur, nxt = bufs[hh]
        n_full = i - jlo

        def two_blocks(t, carry, q=q, hh=hh, jlo=jlo, cur=cur, nxt=nxt):
            j = jlo + 2 * t
            logits(q, hh, j + 1, nxt)
            carry = softmax_pv(hh, j, cur, carry, False)
            logits(q, hh, j + 2, cur)
            return softmax_pv(hh, j + 1, nxt, carry, False)

        def start_other_head(hh=hh):
            if hh == 0:
                logits(q_ref[1], 1, jlos[1], bufs[1][0])

        def tail_odd(carry, q=q, hh=hh, cur=cur, nxt=nxt, start=start_other_head):
            logits(q, hh, i, nxt)
            start()
            carry = softmax_pv(hh, i - 1, cur, carry, False)
            return softmax_pv(hh, i, nxt, carry, True)

        def tail_even(carry, hh=hh, cur=cur, start=start_other_head):
            start()
            return softmax_pv(hh, i, cur, carry, True)

        carry = (jnp.full((tq, LANES), -jnp.inf, F32), jnp.zeros((tq, LANES), F32))
        carry = lax.fori_loop(0, n_full // 2, two_blocks, carry)
        _, acc = lax.cond(n_full % 2 == 1, tail_odd, tail_even, carry)
        accs.append(acc)
    o0 = accs[0] / pltpu.roll(accs[0], FOX_HEAD_DIM, 1)
    o1 = accs[1] / pltpu.roll(accs[1], FOX_HEAD_DIM, 1)
    o_ref[...] = jnp.where(lane < FOX_HEAD_DIM, o0, o1)


def _skip_starts(stats, tm, tq):
    r = tq // tm
    nq = stats.shape[0] // r
    scale = FOX_HEAD_DIM ** -0.5
    st = stats.reshape(nq, r, ST_ROWS, LANES)
    per_head = lambda row0: jnp.stack(
        [st[:, :, row0 + h // 2, FOX_HEAD_DIM * (h % 2)] for h in range(FOX_HEADS)], axis=-1)
    f_first = st[:, 0, ST_FFIRST, :FOX_HEADS].T
    f_last = st[:, r - 1, ST_FLAST, :FOX_HEADS].T
    qn = jnp.sqrt(jnp.max(per_head(ST_QN2), axis=1)).T
    kn = jnp.sqrt(jnp.max(per_head(ST_KN2), axis=(0, 1)))[:, None]
    bound = NORM_SLACK * scale * qn * kn
    dmin = scale * jnp.min(per_head(ST_DMIN), axis=1).T - (NORM_SLACK - 1.0) * bound
    thresh = f_first + bound - dmin + SKIP_LOG_MARGIN
    need = f_last[:, None, :] <= thresh[:, :, None]
    idx = jnp.arange(nq)
    first = jnp.min(jnp.where(need, idx[None, None, :], nq), axis=-1)
    return jnp.minimum(first, idx[None, :]).astype(jnp.int32).reshape(-1)


def _attention(jlo, qa, ka, va, tq):
    s = qa.shape[1]
    tk = tq
    kv_spec = pl.BlockSpec((2, s, LANES), lambda p, i, jl: (p, 0, 0))
    return pl.pallas_call(
        functools.partial(_attn_kernel, tq=tq, tk=tk),
        grid_spec=pltpu.PrefetchScalarGridSpec(
            num_scalar_prefetch=1,
            grid=(FOX_HEADS // 2, s // tq),
            in_specs=[pl.BlockSpec((2, tq, LANES), lambda p, i, jl: (p, i, 0)), kv_spec, kv_spec],
            out_specs=pl.BlockSpec((tq, LANES), lambda p, i, jl: (i, p)),
            scratch_shapes=[pltpu.VMEM((tq, tk), F32)] * 4),
        out_shape=jax.ShapeDtypeStruct((s, WIDTH), F32),
        compiler_params=_cparams(("parallel", "arbitrary")),
        name="fox_attention",
    )(jlo, qa, ka, va)


def _chunk_cumsum_matrix():
    r = np.arange(GDN_TILE)
    same = (r[:, None] // CHUNK) == (r[None, :] // CHUNK)
    return jnp.asarray((same & (r[None, :] <= r[:, None])).astype(np.float32), BF16)


def _gdn_prep_kernel(xq_ref, xk_ref, xv_ref, hq_ref, hk_ref, hv_ref, small_ref, cw_ref, aneg_ref,
                     dtb_ref, kl_ref, qd_ref, kd_ref, u_ref, w_ref, aqk_ref, egl_ref, xx_ref):
    t = GDN_TILE
    hd = GDN_HEAD_DIM
    pad = BF16_SUBLANES
    rows_in = xq_ref.shape[0]

    for n, (x_ref, halo_ref) in enumerate(((xq_ref, hq_ref), (xk_ref, hk_ref), (xv_ref, hv_ref))):
        cs = slice(n * WIDTH, (n + 1) * WIDTH)
        halo = halo_ref[...].astype(F32)
        xx_ref[0:pad, cs] = jnp.where(pl.program_id(0) == 0, jnp.zeros_like(halo), halo)
        xx_ref[pad:pad + rows_in, cs] = x_ref[...].astype(F32)

    row = lax.broadcasted_iota(jnp.int32, (t, t), 0)
    col = lax.broadcasted_iota(jnp.int32, (t, t), 1)
    same = (row // CHUNK) == (col // CHUNK)
    incl = jnp.logical_and(same, col <= row)
    strict = jnp.logical_and(same, col < row)
    eye = (row == col).astype(F32)
    row8 = lax.broadcasted_iota(jnp.int32, (8, WIDTH), 0)

    invs, bps, rhss, dests = [], [], [], []
    for tile in range(rows_in // t):
        r0 = tile * t
        rs = slice(r0, r0 + t)
        conv = cw_ref[CONV_K - 1:CONV_K, :] * xx_ref[pad + r0:pad + r0 + t, :]
        for j in range(CONV_K - 1):
            off = pad + r0 - (CONV_K - 1) + j
            conv = conv + cw_ref[j:j + 1, :] * xx_ref[off:off + t, :]
        qkv = _silu(conv)

        small = small_ref[rs, :]
        bcast = lambda c0: jnp.concatenate(
            [jnp.broadcast_to(small[:, c0 + h:c0 + h + 1], (t, hd)) for h in range(GDN_HEADS)],
            axis=1)
        sp_in = bcast(SMALL_BA) + dtb_ref[...]
        softplus = jnp.maximum(sp_in, 0.0) + jnp.log1p(jnp.exp(-jnp.abs(sp_in)))
        g = aneg_ref[...] * softplus
        beta = _sigmoid(bcast(SMALL_BB))
        gcum = _dot_exact_lhs(kl_ref[...], g)
        glast = jnp.concatenate(
            [jnp.broadcast_to(gcum[(c + 1) * CHUNK - 1:(c + 1) * CHUNK, :], (CHUNK, WIDTH))
             for c in range(CHUNKS_PER_TILE)], axis=0)
        eg = jnp.exp(gcum)
        egl = jnp.exp(glast)
        egd = jnp.exp(glast - gcum)
        egl_rows = jnp.zeros((8, WIDTH), F32)
        for c in range(CHUNKS_PER_TILE):
            egl_rows = jnp.where(row8 == c, egl[c * CHUNK:c * CHUNK + 8, :], egl_rows)
        egl_ref[tile * 8:(tile + 1) * 8, :] = egl_rows

        for h in range(GDN_HEADS):
            cs = slice(h * hd, (h + 1) * hd)
            qh = qkv[:, h * hd:(h + 1) * hd]
            kh = qkv[:, WIDTH + h * hd:WIDTH + (h + 1) * hd]
            vh = qkv[:, 2 * WIDTH + h * hd:2 * WIDTH + (h + 1) * hd]
            qh = qh * lax.rsqrt(jnp.sum(qh * qh, axis=-1, keepdims=True) + EPS) * (hd ** -0.5)
            kh = kh * lax.rsqrt(jnp.sum(kh * kh, axis=-1, keepdims=True) + EPS)
            bh = beta[:, cs]
            gc = gcum[:, cs]
            dmat = jnp.concatenate([gc, gc], axis=1) - gc.T[0:1, :]
            gamma = jnp.exp(jnp.where(incl, dmat, -jnp.inf))
            kb = kh * bh
            kk_qk = _dot_nt(jnp.concatenate([kb, qh], axis=0), kh)
            a = jnp.where(strict, kk_qk[:t] * gamma, 0.0)
            invs.append(eye - a)
            bps.append(_bf(-a))
            rhss.append(_bf(jnp.concatenate([vh * bh, kb * eg[:, cs]], axis=1)))
            aqk_ref[h, rs, :] = _bf(kk_qk[t:] * gamma)
            qd_ref[rs, cs] = _bf(qh * eg[:, cs])
            kd_ref[rs, cs] = _bf(kh * egd[:, cs])
            dests.append((rs, cs))

    n_sq = 5
    bps = [_bf(jnp.dot(b, b, preferred_element_type=F32)) for b in bps]
    for k in range(n_sq):
        last = k == n_sq - 1
        prods = [jnp.dot(_bf(inv) if last else jnp.concatenate([_bf(inv), b], axis=0), b,
                         preferred_element_type=F32) for inv, b in zip(invs, bps)]
        invs = [inv + p[:t] for inv, p in zip(invs, prods)]
        if not last:
            bps = [_bf(p[t:]) for p in prods]
    for inv, rhs, (rs, cs) in zip(invs, rhss, dests):
        sol = jnp.dot(_bf(inv), rhs, preferred_element_type=F32)
        u_ref[rs, cs] = sol[:, :hd]
        w_ref[rs, cs] = _bf(sol[:, hd:])


def _gdn_prep(z, small, conv_w, aneg, dtb, layer, rows):
    s = z.shape[0]
    t = GDN_TILE
    pad = BF16_SUBLANES
    tiles = rows // t
    kl = _chunk_cumsum_matrix()
    row_spec = pl.BlockSpec((rows, WIDTH), lambda i: (i, 0))
    halo_spec = lambda col: pl.BlockSpec(
        (pad, WIDTH), lambda i: (jnp.maximum(i * (rows // pad) - 1, 0), col // WIDTH))
    return pl.pallas_call(
        _gdn_prep_kernel,
        grid=(s // rows,),
        in_specs=[pl.BlockSpec((rows, WIDTH), lambda i: (i, COL_BQ // WIDTH)),
                  pl.BlockSpec((rows, WIDTH), lambda i: (i, COL_BK // WIDTH)),
                  pl.BlockSpec((rows, WIDTH), lambda i: (i, COL_BV // WIDTH)),
                  halo_spec(COL_BQ), halo_spec(COL_BK), halo_spec(COL_BV),
                  pl.BlockSpec((rows, LANES), lambda i: (i, 0)),
                  _layer_spec((CONV_K, 3 * WIDTH), layer, 1),
                  _layer_spec((1, WIDTH), layer, 1),
                  _layer_spec((1, WIDTH), layer, 1),
                  pl.BlockSpec((t, t), lambda i: (0, 0))],
        out_specs=[row_spec, row_spec, row_spec, row_spec,
                   pl.BlockSpec((GDN_HEADS, rows, t), lambda i: (0, i, 0)),
                   pl.BlockSpec((8 * tiles, WIDTH), lambda i: (i, 0))],
        out_shape=[jax.ShapeDtypeStruct((s, WIDTH), BF16),
                   jax.ShapeDtypeStruct((s, WIDTH), BF16),
                   jax.ShapeDtypeStruct((s, WIDTH), F32),
                   jax.ShapeDtypeStruct((s, WIDTH), BF16),
                   jax.ShapeDtypeStruct((GDN_HEADS, s, t), BF16),
                   jax.ShapeDtypeStruct((s // t * 8, WIDTH), F32)],
        scratch_shapes=[pltpu.VMEM((rows + pad, 3 * WIDTH), F32)],
        compiler_params=_cparams(("parallel",)),
        name="gdn_prep",
    )(z, z, z, z, z, z, small, conv_w, aneg, dtb, kl)


def _gdn_scan_kernel(qd_ref, kd_ref, u_ref, w_ref, aqk_ref, egl_ref, o_ref, state_ref, vn_ref):
    hd = GDN_HEAD_DIM

    @pl.when(pl.program_id(0) == 0)
    def _():
        state_ref[...] = jnp.zeros_like(state_ref)
        vn_ref[...] = jnp.zeros_like(vn_ref)

    heads = range(GDN_HEADS)
    col = lambda h: slice(h * hd, (h + 1) * hd)
    for c in range(CHUNKS_PER_TILE):
        rs = slice(c * CHUNK, (c + 1) * CHUNK)
        sts = [state_ref[h] for h in heads]
        rr = [jnp.dot(jnp.concatenate([w_ref[rs, col(h)], qd_ref[rs, col(h)]], axis=0),
                      _bf(sts[h]), preferred_element_type=F32) for h in heads]
        vns = [_bf(u_ref[rs, col(h)] - rr[h][:CHUNK]) for h in heads]
        for h in heads:
            vn_ref[h, rs, :] = vns[h]
        for h in heads:
            upd = lax.dot_general(kd_ref[rs, col(h)], vns[h], (((0,), (0,)), ((), ())),
                                  preferred_element_type=F32)
            state_ref[h] = sts[h] * egl_ref[c:c + 1, col(h)] + upd
        for h in heads:
            o_ref[rs, col(h)] = rr[h][CHUNK:] + jnp.dot(aqk_ref[h, rs, :], vn_ref[h],
                                                        preferred_element_type=F32)


def _gdn_scan(qd, kd, u, w, aqk, egl):
    s = qd.shape[0]
    t = GDN_TILE
    row_spec = pl.BlockSpec((t, WIDTH), lambda i: (i, 0))
    return pl.pallas_call(
        _gdn_scan_kernel,
        grid=(s // t,),
        in_specs=[row_spec, row_spec, row_spec, row_spec,
                  pl.BlockSpec((GDN_HEADS, t, t), lambda i: (0, i, 0)),
                  pl.BlockSpec((8, WIDTH), lambda i: (i, 0))],
        out_specs=row_spec,
        out_shape=jax.ShapeDtypeStruct((s, WIDTH), F32),
        scratch_shapes=[pltpu.VMEM((GDN_HEADS, GDN_HEAD_DIM, GDN_HEAD_DIM), F32),
                        pltpu.VMEM((GDN_HEADS, t, GDN_HEAD_DIM), BF16)],
        compiler_params=_cparams(("arbitrary",)),
        name="gdn_scan",
    )(qd, kd, u, w, aqk, egl)


def _merge_kernel(x_ref, oa_ref, ob_ref, mq_ref, az_ref, bz_ref, mz_ref, g0_ref, g1_ref, g2_ref,
                  mk_ref, mv_ref, gng_ref, bm_ref, wb_ref, wo_ref, fg_ref, o_ref, *, final_norm):
    hd = GDN_HEAD_DIM
    groups = [slice(r0, r0 + MERGE_SUB) for r0 in range(0, x_ref.shape[0], MERGE_SUB)]
    g_refs = (g0_ref, g1_ref, g2_ref)

    def branch_a(rs):
        return oa_ref[rs, :] * _silu(az_ref[rs, :].astype(F32))

    def branch_b(rs):
        normed = []
        for h in range(GDN_HEADS):
            oh = ob_ref[rs, h * hd:(h + 1) * hd]
            normed.append(oh * lax.rsqrt(jnp.mean(oh * oh, axis=-1, keepdims=True) + EPS))
        return jnp.concatenate(normed, axis=1) * gng_ref[...] * _silu(bz_ref[rs, :].astype(F32))

    def branch_m(rs):
        om = []
        for h in range(MEM_HEADS):
            cs = slice(h * MEM_HEAD_DIM, (h + 1) * MEM_HEAD_DIM)
            sc = _dot_nt(mq_ref[rs, cs], mk_ref[:, cs])
            p = jnp.exp2((sc - jnp.max(sc, axis=-1, keepdims=True))
                         * (-NEG_LOG2_E * MEM_HEAD_DIM ** -0.5))
            om.append(jnp.dot(_bf(p), mv_ref[:, cs], preferred_element_type=F32)
                      / jnp.sum(p, axis=-1, keepdims=True))
        return jnp.concatenate(om, axis=1) * _silu(mz_ref[rs, :].astype(F32))

    merged = [None] * len(groups)
    for n, branch in enumerate((branch_a, branch_b, branch_m)):
        for k, rs in enumerate(groups):
            proj = jnp.dot(_bf(branch(rs)), wb_ref[n], preferred_element_type=F32)
            term = _sigmoid(g_refs[n][rs, :].astype(F32) + bm_ref[n:n + 1, :]) * proj
            merged[k] = term if merged[k] is None else merged[k] + term
    for k, rs in enumerate(groups):
        out = x_ref[rs, :] + jnp.dot(_bf(merged[k]), wo_ref[...], preferred_element_type=F32)
        if final_norm:
            out = out * lax.rsqrt(jnp.mean(out * out, axis=-1, keepdims=True) + EPS) * fg_ref[...]
        o_ref[rs, :] = out


def _merge(x, o_a, o_b, z, memkv, gng, bm, wb, wo, fg_row, layer, tm, final_norm):
    s = x.shape[0]
    zc = lambda col, width: pl.BlockSpec((tm, width), lambda i: (i, col // width))
    n_mem = memkv.shape[0]
    return pl.pallas_call(
        functools.partial(_merge_kernel, final_norm=final_norm),
        grid=(s // tm,),
        in_specs=[pl.BlockSpec((tm, D_MODEL), lambda i: (i, 0)),
                  pl.BlockSpec((tm, WIDTH), lambda i: (i, 0)),
                  pl.BlockSpec((tm, WIDTH), lambda i: (i, 0)),
                  zc(COL_MQ, WIDTH),
                  zc(COL_AZ, WIDTH), zc(COL_BZ, WIDTH), zc(COL_MZ, WIDTH),
                  zc(COL_GATES, D_MODEL), zc(COL_GATES + D_MODEL, D_MODEL),
                  zc(COL_GATES + 2 * D_MODEL, D_MODEL),
                  pl.BlockSpec((n_mem, WIDTH), lambda i: (0, 0)),
                  pl.BlockSpec((n_mem, WIDTH), lambda i: (0, 1)),
                  _layer_spec((1, WIDTH), layer, 1),
                  _layer_spec((N_BRANCH, D_MODEL), layer, 1),
                  _layer_spec((N_BRANCH, WIDTH, D_MODEL), layer, 1),
                  _layer_spec((D_MODEL, D_MODEL), layer, 1),
                  pl.BlockSpec((1, D_MODEL), lambda i: (0, 0))],
        out_specs=pl.BlockSpec((tm, D_MODEL), lambda i: (i, 0)),
        out_shape=jax.ShapeDtypeStruct((s, D_MODEL), F32),
        compiler_params=_cparams(("parallel",)),
        name="merge",
    )(x, o_a, o_b, z, z, z, z, z, z, z, memkv, memkv, gng, bm, wb, wo, fg_row)


def _main_row_start(c):
    col = c * WIDTH
    past_af = (col >= COL_AZ).astype(jnp.int32)
    past_ba = (col >= COL_BZ).astype(jnp.int32)
    return (c * (WIDTH // F32_SUBLANES) + past_af * (FOX_HEADS // F32_SUBLANES)
            + past_ba * (2 * GDN_HEADS // F32_SUBLANES))


NARROW_ROWS = ((SRC_AF, FOX_HEADS), (SRC_BA, 2 * GDN_HEADS))


def kernel(x, mem, norm_g, w_in, b_fg, b_merge, conv_w, a_log, dt_bias, gdn_norm_g, mem_norm_g,
           w_mem_kv, w_branch, w_out, final_norm_g):
    assert x.shape[0] == 1 and mem.shape[0] == 1
    depth = w_in.shape[0]
    s = x.shape[1]
    rows = lambda v: v.astype(F32).reshape(depth, 1, -1)
    assert w_in.shape[2] == N_SRC
    w_in_t = jnp.swapaxes(w_in, 1, 2)
    w_mem_t = jnp.swapaxes(w_mem_kv, 1, 2)
    w_branch_b, w_out_b = _bf(w_branch), _bf(w_out)
    norm_g3, mem_norm_g3 = rows(norm_g), rows(mem_norm_g)
    bfg3 = rows(jnp.pad(b_fg, ((0, 0), (0, LANES - FOX_HEADS))))
    aneg3 = rows(jnp.repeat(-jnp.exp(a_log.astype(F32)), GDN_HEAD_DIM, axis=1))
    dtb3 = rows(jnp.repeat(dt_bias, GDN_HEAD_DIM, axis=1))
    gng3 = rows(jnp.tile(gdn_norm_g, (1, GDN_HEADS)))
    bm3 = b_merge.reshape(depth, N_BRANCH, D_MODEL)
    fg_row = final_norm_g.reshape(1, D_MODEL)
    tq = min(s, ATTN_TQ)

    h = x[0]
    for l in range(depth):
        z, small = _norm_matmul(h, norm_g3, w_in_t, l, tm=min(s, NORM_TM), tn=NORM_TN,
                                n_out=N_MAIN, sub=WIDTH, row_start=_main_row_start,
                                narrow_rows=NARROW_ROWS)
        memkv, = _norm_matmul(mem[0], mem_norm_g3, w_mem_t, l, tm=mem.shape[1], tn=2 * WIDTH)
        qa, ka, va, stats = _attn_prep(z, small, bfg3, l, tm=min(s, ATTN_PREP_ROWS))
        o_a = _attention(_skip_starts(stats, ATTN_STAT_TILE, tq), qa, ka, va, tq=tq)
        qd, kd, u, w, aqk, egl = _gdn_prep(z, small, conv_w, aneg3, dtb3, l, rows=GDN_PREP_ROWS)
        o_b = _gdn_scan(qd, kd, u, w, aqk, egl)
        h = _merge(h, o_a, o_b, z, memkv, gng3, bm3, w_branch_b, w_out_b, fg_row, l,
                   tm=MERGE_TM, final_norm=(l == depth - 1))
    return h[None]
```

```python
import functools

import jax
import jax.numpy as jnp
import numpy as np
from jax import lax
from jax.experimental import pallas as pl
from jax.experimental.pallas import tpu as pltpu

F32 = jnp.float32
BF16 = jnp.bfloat16

D_MODEL = 1024
EPS = 1e-6
FOX_HEADS = 8
FOX_HEAD_DIM = 64
GDN_HEADS = 4
GDN_HEAD_DIM = 128
MEM_HEADS = 4
MEM_HEAD_DIM = 128
WIDTH = 512
N_BRANCH = 3
CHUNK = 64
CONV_K = 4
LANES = 128
F32_SUBLANES = 8
BF16_SUBLANES = 16
GDN_TILE = 256
CHUNKS_PER_TILE = GDN_TILE // CHUNK

SRC_AF = 3 * WIDTH
SRC_BA = SRC_AF + FOX_HEADS + 4 * WIDTH
N_SRC = 8208
COL_AQ, COL_AK, COL_AV, COL_AZ = 0, 512, 1024, 1536
COL_BQ, COL_BK, COL_BV, COL_BZ = 2048, 2560, 3072, 3584
COL_MQ, COL_MZ, COL_GATES = 4096, 4608, 5120
N_MAIN = 8192
SMALL_BA = 8
SMALL_BB = 12

NORM_TM = 2048
NORM_TN = 1024
ATTN_PREP_ROWS = 1024
ATTN_STAT_TILE = 256
ATTN_TQ = 512
GDN_PREP_ROWS = 512
MERGE_TM = 512
MERGE_SUB = 512

SKIP_LOG_MARGIN = 90.0
NORM_SLACK = 1.01

VMEM_LIMIT = 56 * 1024 * 1024


def _cparams(sem):
    return pltpu.CompilerParams(dimension_semantics=sem, vmem_limit_bytes=VMEM_LIMIT)


def _bf(x):
    return x.astype(BF16)


def _dot_nt(a, b):
    return lax.dot_general(_bf(a), _bf(b), (((1,), (1,)), ((), ())), preferred_element_type=F32)


NEG_LOG2_E = -1.4426950408889634


def _sigmoid(x):
    return 1.0 / (1.0 + jnp.exp2(x * NEG_LOG2_E))


def _silu(x):
    return x * _sigmoid(x)


def _split3(x):
    hi = _bf(x).astype(F32)
    r1 = x - hi
    mid = _bf(r1).astype(F32)
    lo = _bf(r1 - mid).astype(F32)
    return hi, mid, lo


def _dot_exact_lhs(mat01, x):
    hi, mid, lo = _split3(x)
    return (jnp.dot(mat01, _bf(hi), preferred_element_type=F32)
            + jnp.dot(mat01, _bf(mid), preferred_element_type=F32)
            + jnp.dot(mat01, _bf(lo), preferred_element_type=F32))


def _layer_spec(shape, layer, n_grid):
    zeros = (0,) * len(shape)
    if n_grid == 1:
        return pl.BlockSpec((None,) + tuple(shape), lambda i: (layer,) + zeros)
    return pl.BlockSpec((None,) + tuple(shape), lambda i, j: (layer,) + zeros)


def _norm_matmul_kernel(x_ref, g_ref, *rest, n_w, n_narrow):
    w_refs, narrow_refs, outs = rest[:n_w], rest[n_w:n_w + n_narrow], rest[n_w + n_narrow:]
    if n_narrow:
        o_ref, os_ref, h_ref = outs
    else:
        o_ref, h_ref = outs

    @pl.when(pl.program_id(1) == 0)
    def _():
        x = x_ref[...]
        y = x * lax.rsqrt(jnp.mean(x * x, axis=-1, keepdims=True) + EPS)
        h = _bf(y * g_ref[...])
        h_ref[...] = h
        if n_narrow:
            rows = [r[...] for r in narrow_refs]
            used = sum(r.shape[0] for r in rows)
            rows.append(jnp.zeros((LANES - used, x.shape[1]), rows[0].dtype))
            os_ref[...] = _dot_nt(h, jnp.concatenate(rows, axis=0))

    sub = o_ref.shape[1] // n_w
    for k, w_ref in enumerate(w_refs):
        o_ref[:, k * sub:(k + 1) * sub] = _dot_nt(h_ref[...], w_ref[...]).astype(o_ref.dtype)


def _norm_matmul(x, g, w_t, layer, tm, tn, n_out=None, sub=None, row_start=None, narrow_rows=()):
    s, d = x.shape
    n_out = w_t.shape[1] if n_out is None else n_out
    sub = tn if sub is None else sub
    n_w = tn // sub
    n_src = w_t.shape[1]
    w_rows = w_t.reshape(w_t.shape[0] * n_src, d)
    unit = F32_SUBLANES
    window = lambda rows, start: pl.BlockSpec(
        (pl.Element(rows), pl.Element(d)),
        lambda i, j: ((layer * (n_src // unit) + start(j)) * unit, 0))
    if row_start is None:
        w_specs = [window(tn, lambda j: j * (tn // unit))]
    else:
        w_specs = [window(sub, lambda j, k=k: row_start(j * n_w + k)) for k in range(n_w)]
    narrow_specs = [window(cnt, lambda j, r=r: r // unit) for r, cnt in narrow_rows]
    out_specs = [pl.BlockSpec((tm, tn), lambda i, j: (i, j))]
    out_shape = [jax.ShapeDtypeStruct((s, n_out), BF16)]
    if narrow_rows:
        out_specs.append(pl.BlockSpec((tm, LANES), lambda i, j: (i, 0)))
        out_shape.append(jax.ShapeDtypeStruct((s, LANES), F32))
    return pl.pallas_call(
        functools.partial(_norm_matmul_kernel, n_w=n_w, n_narrow=len(narrow_rows)),
        grid=(s // tm, n_out // tn),
        in_specs=[pl.BlockSpec((tm, d), lambda i, j: (i, 0)), _layer_spec((1, d), layer, 2)]
        + w_specs + narrow_specs,
        out_specs=out_specs, out_shape=out_shape,
        scratch_shapes=[pltpu.VMEM((tm, d), BF16)],
        compiler_params=_cparams(("parallel", "arbitrary")),
        name="norm_matmul",
    )(x, g, *([w_rows] * (n_w + len(narrow_rows))))


FL_HI, FL_MID, FL_LO, FL_ONE = 0, 8, 16, 24
ST_FFIRST, ST_FLAST, ST_QN2, ST_KN2, ST_DMIN, ST_ROWS = 0, 1, 2, 6, 10, 16


def _aug_placement():
    d = FOX_HEAD_DIM
    wq, wk, wv = (np.zeros((FOX_HEADS // 2, 2 * LANES, 2 * LANES), np.float32) for _ in range(3))
    for p in range(FOX_HEADS // 2):
        for half in range(2):
            h = 2 * p + half
            feat = half * LANES + half * d
            aux = half * LANES + (1 - half) * d
            for w, scale in ((wq, d ** -0.5), (wk, 1.0), (wv, 1.0)):
                w[p, half * d + np.arange(d), feat + np.arange(d)] = scale
            for a, src in enumerate((FL_HI, FL_MID, FL_LO)):
                wq[p, LANES + src + h, aux + a] = 1.0
                wk[p, LANES + src + h, aux + 3 + a] = -1.0
            wq[p, LANES + FL_ONE, aux + 3:aux + 6] = 1.0
            wk[p, LANES + FL_ONE, aux:aux + 3] = 1.0
            wv[p, LANES + FL_ONE, aux:aux + d] = 1.0
    return [jnp.asarray(w, BF16) for w in (wq, wk, wv)]


def _attn_prep_kernel(q_ref, k_ref, v_ref, small_ref, bfg_ref, tril_ref, wq_ref, wk_ref, wv_ref,
                      nrm_ref, qa_ref, ka_ref, va_ref, stats_ref, carry_ref):
    rows = q_ref.shape[0]
    t = ATTN_STAT_TILE
    subs = [slice(r0, r0 + t) for r0 in range(0, rows, t)]

    @pl.when(pl.program_id(0) == 0)
    def _():
        carry_ref[...] = jnp.zeros_like(carry_ref)

    af = small_ref[...] + bfg_ref[...]
    logf = jnp.minimum(af, 0.0) - jnp.log1p(jnp.exp(-jnp.abs(af)))
    offset = carry_ref[...]
    cums = []
    for rs in subs:
        cums.append(_dot_exact_lhs(tril_ref[...], logf[rs, :]) + offset)
        offset = cums[-1][t - 1:t, :]
    carry_ref[...] = offset
    cum = jnp.concatenate(cums, axis=0)

    lane = lax.broadcasted_iota(jnp.int32, (rows, LANES), 1)
    hi, mid, lo = _split3(cum)
    fl = jnp.where(lane < FL_MID, hi,
                   jnp.where(lane < FL_LO, pltpu.roll(mid, FL_MID, 1),
                             jnp.where(lane < FL_ONE, pltpu.roll(lo, FL_LO, 1),
                                       jnp.where(lane == FL_ONE, 1.0, 0.0))))
    fl = _bf(fl)

    row = lax.broadcasted_iota(jnp.int32, (ST_ROWS, LANES), 0)
    put = lambda st, r, vec: jnp.where(row == r, jnp.broadcast_to(vec, (ST_ROWS, LANES)), st)
    stats = [put(put(jnp.zeros((ST_ROWS, LANES), F32), ST_FFIRST, c[0:1, :]),
                 ST_FLAST, c[t - 1:t, :]) for c in cums]
    for p in range(FOX_HEADS // 2):
        cols = slice(p * LANES, (p + 1) * LANES)
        for x_ref, w_ref, out_ref in ((q_ref, wq_ref, qa_ref), (k_ref, wk_ref, ka_ref),
                                      (v_ref, wv_ref, va_ref)):
            aug = jnp.dot(jnp.concatenate([x_ref[:, cols], fl], axis=1), w_ref[p],
                          preferred_element_type=F32)
            out_ref[2 * p] = _bf(aug[:, :LANES])
            out_ref[2 * p + 1] = _bf(aug[:, LANES:])
        xq, xk = q_ref[:, cols], k_ref[:, cols]
        n2 = jnp.dot(jnp.concatenate([xq * xq, xk * xk], axis=1), nrm_ref[...],
                     preferred_element_type=F32)
        diag = jnp.dot(xq * xk, nrm_ref[:LANES, :LANES], preferred_element_type=F32)
        for k, rs in enumerate(subs):
            top = jnp.max(n2[rs, :], axis=0, keepdims=True)
            stats[k] = put(put(stats[k], ST_QN2 + p, top[:, :LANES]), ST_KN2 + p, top[:, LANES:])
            stats[k] = put(stats[k], ST_DMIN + p, jnp.min(diag[rs, :], axis=0, keepdims=True))
    for k in range(len(subs)):
        stats_ref[k] = stats[k]


def _attn_prep(z, small, bfg, layer, tm):
    s = z.shape[0]
    t = ATTN_STAT_TILE
    tril = jnp.asarray(np.tril(np.ones((t, t), np.float32)), BF16)
    half = np.arange(2 * LANES) // FOX_HEAD_DIM
    nrm = jnp.asarray((half[:, None] == half[None, :]).astype(np.float32), BF16)
    places = _aug_placement()
    out_sds = jax.ShapeDtypeStruct((FOX_HEADS, s, LANES), BF16)
    out_spec = pl.BlockSpec((FOX_HEADS, tm, LANES), lambda i: (0, i, 0))
    const = lambda a: pl.BlockSpec(a.shape, lambda i: (0,) * a.ndim)
    return pl.pallas_call(
        _attn_prep_kernel,
        grid=(s // tm,),
        in_specs=[pl.BlockSpec((tm, WIDTH), lambda i: (i, COL_AQ // WIDTH)),
                  pl.BlockSpec((tm, WIDTH), lambda i: (i, COL_AK // WIDTH)),
                  pl.BlockSpec((tm, WIDTH), lambda i: (i, COL_AV // WIDTH)),
                  pl.BlockSpec((tm, LANES), lambda i: (i, 0)),
                  _layer_spec((1, LANES), layer, 1),
                  const(tril)] + [const(w) for w in places] + [const(nrm)],
        out_specs=[out_spec, out_spec, out_spec,
                   pl.BlockSpec((tm // t, ST_ROWS, LANES), lambda i: (i, 0, 0))],
        out_shape=[out_sds, out_sds, out_sds,
                   jax.ShapeDtypeStruct((s // t, ST_ROWS, LANES), F32)],
        scratch_shapes=[pltpu.VMEM((1, LANES), F32)],
        compiler_params=_cparams(("arbitrary",)),
        name="attn_prep",
    )(z, z, z, small, bfg, tril, *places, nrm)


def _attn_kernel(jlo_ref, q_ref, k_ref, v_ref, o_ref, s0_ref, s1_ref, s2_ref, s3_ref, *, tq, tk):
    pair = pl.program_id(0)
    i = pl.program_id(1)
    lane = lax.broadcasted_iota(jnp.int32, (tq, LANES), 1)
    n_tile = tk // LANES

    def logits(q, hh, j, s_ref):
        start = pl.multiple_of(j * tk, tk)
        s_ref[...] = lax.dot_general(q, k_ref[hh, pl.ds(start, tk), :], (((1,), (1,)), ((), ())),
                                     preferred_element_type=F32)

    def softmax_pv(hh, j, s_ref, carry, masked):
        m, acc = carry
        start = pl.multiple_of(j * tk, tk)
        if masked:
            row = lax.broadcasted_iota(jnp.int32, (tq, tk), 0)
            col = lax.broadcasted_iota(jnp.int32, (tq, tk), 1)
            s_ref[...] = jnp.where(col <= row, s_ref[...], -jnp.inf)
        m_new = jnp.maximum(m, jnp.max(s_ref[...], axis=1, keepdims=True))
        p = jnp.exp(s_ref[...] - jnp.tile(m_new, (1, n_tile)))
        alpha = jnp.exp(m - m_new)
        acc = alpha * acc + jnp.dot(_bf(p), v_ref[hh, pl.ds(start, tk), :],
                                    preferred_element_type=F32)
        return m_new, acc

    bufs = ((s0_ref, s1_ref), (s2_ref, s3_ref))
    jlos = [jlo_ref[(2 * pair + hh) * pl.num_programs(1) + i] for hh in range(2)]
    logits(q_ref[0], 0, jlos[0], bufs[0][0])
    accs = []
    for hh in range(2):
        q = q_ref[hh]
        jlo = jlos[hh]
        cur, nxt = bufs[hh]
        n_full = i - jlo

        def two_blocks(t, carry, q=q, hh=hh, jlo=jlo, cur=cur, nxt=nxt):
            j = jlo + 2 * t
            logits(q, hh, j + 1, nxt)
            carry = softmax_pv(hh, j, cur, carry, False)
            logits(q, hh, j + 2, cur)
            return softmax_pv(hh, j + 1, nxt, carry, False)

        def start_other_head(hh=hh):
            if hh == 0:
                logits(q_ref[1], 1, jlos[1], bufs[1][0])

        def tail_odd(carry, q=q, hh=hh, cur=cur, nxt=nxt, start=start_other_head):
            logits(q, hh, i, nxt)
            start()
            carry = softmax_pv(hh, i - 1, cur, carry, False)
            return softmax_pv(hh, i, nxt, carry, True)

        def tail_even(carry, hh=hh, cur=cur, start=start_other_head):
            start()
            return softmax_pv(hh, i, cur, carry, True)

        carry = (jnp.full((tq, LANES), -jnp.inf, F32), jnp.zeros((tq, LANES), F32))
        carry = lax.fori_loop(0, n_full // 2, two_blocks, carry)
        _, acc = lax.cond(n_full % 2 == 1, tail_odd, tail_even, carry)
        accs.append(acc)
    o0 = accs[0] / pltpu.roll(accs[0], FOX_HEAD_DIM, 1)
    o1 = accs[1] / pltpu.roll(accs[1], FOX_HEAD_DIM, 1)
    o_ref[...] = jnp.where(lane < FOX_HEAD_DIM, o0, o1)


def _skip_starts(stats, tm, tq):
    r = tq // tm
    nq = stats.shape[0] // r
    scale = FOX_HEAD_DIM ** -0.5
    st = stats.reshape(nq, r, ST_ROWS, LANES)
    per_head = lambda row0: jnp.stack(
        [st[:, :, row0 + h // 2, FOX_HEAD_DIM * (h % 2)] for h in range(FOX_HEADS)], axis=-1)
    f_first = st[:, 0, ST_FFIRST, :FOX_HEADS].T
    f_last = st[:, r - 1, ST_FLAST, :FOX_HEADS].T
    qn = jnp.sqrt(jnp.max(per_head(ST_QN2), axis=1)).T
    kn = jnp.sqrt(jnp.max(per_head(ST_KN2), axis=(0, 1)))[:, None]
    bound = NORM_SLACK * scale * qn * kn
    dmin = scale * jnp.min(per_head(ST_DMIN), axis=1).T - (NORM_SLACK - 1.0) * bound
    thresh = f_first + bound - dmin + SKIP_LOG_MARGIN
    need = f_last[:, None, :] <= thresh[:, :, None]
    idx = jnp.arange(nq)
    first = jnp.min(jnp.where(need, idx[None, None, :], nq), axis=-1)
    return jnp.minimum(first, idx[None, :]).astype(jnp.int32).reshape(-1)


def _attention(jlo, qa, ka, va, tq):
    s = qa.shape[1]
    tk = tq
    kv_spec = pl.BlockSpec((2, s, LANES), lambda p, i, jl: (p, 0, 0))
    return pl.pallas_call(
        functools.partial(_attn_kernel, tq=tq, tk=tk),
        grid_spec=pltpu.PrefetchScalarGridSpec(
            num_scalar_prefetch=1,
            grid=(FOX_HEADS // 2, s // tq),
            in_specs=[pl.BlockSpec((2, tq, LANES), lambda p, i, jl: (p, i, 0)), kv_spec, kv_spec],
            out_specs=pl.BlockSpec((tq, LANES), lambda p, i, jl: (i, p)),
            scratch_shapes=[pltpu.VMEM((tq, tk), F32)] * 4),
        out_shape=jax.ShapeDtypeStruct((s, WIDTH), F32),
        compiler_params=_cparams(("parallel", "arbitrary")),
        name="fox_attention",
    )(jlo, qa, ka, va)


def _chunk_cumsum_matrix():
    r = np.arange(GDN_TILE)
    same = (r[:, None] // CHUNK) == (r[None, :] // CHUNK)
    return jnp.asarray((same & (r[None, :] <= r[:, None])).astype(np.float32), BF16)


def _gdn_prep_kernel(xq_ref, xk_ref, xv_ref, hq_ref, hk_ref, hv_ref, small_ref, cw_ref, aneg_ref,
                     dtb_ref, kl_ref, qd_ref, kd_ref, u_ref, w_ref, aqk_ref, egl_ref, xx_ref):
    t = GDN_TILE
    hd = GDN_HEAD_DIM
    pad = BF16_SUBLANES
    rows_in = xq_ref.shape[0]

    for n, (x_ref, halo_ref) in enumerate(((xq_ref, hq_ref), (xk_ref, hk_ref), (xv_ref, hv_ref))):
        cs = slice(n * WIDTH, (n + 1) * WIDTH)
        halo = halo_ref[...].astype(F32)
        xx_ref[0:pad, cs] = jnp.where(pl.program_id(0) == 0, jnp.zeros_like(halo), halo)
        xx_ref[pad:pad + rows_in, cs] = x_ref[...].astype(F32)

    row = lax.broadcasted_iota(jnp.int32, (t, t), 0)
    col = lax.broadcasted_iota(jnp.int32, (t, t), 1)
    same = (row // CHUNK) == (col // CHUNK)
    incl = jnp.logical_and(same, col <= row)
    strict = jnp.logical_and(same, col < row)
    eye = (row == col).astype(F32)
    row8 = lax.broadcasted_iota(jnp.int32, (8, WIDTH), 0)

    invs, bps, rhss, dests = [], [], [], []
    for tile in range(rows_in // t):
        r0 = tile * t
        rs = slice(r0, r0 + t)
        conv = cw_ref[CONV_K - 1:CONV_K, :] * xx_ref[pad + r0:pad + r0 + t, :]
        for j in range(CONV_K - 1):
            off = pad + r0 - (CONV_K - 1) + j
            conv = conv + cw_ref[j:j + 1, :] * xx_ref[off:off + t, :]
        qkv = _silu(conv)

        small = small_ref[rs, :]
        bcast = lambda c0: jnp.concatenate(
            [jnp.broadcast_to(small[:, c0 + h:c0 + h + 1], (t, hd)) for h in range(GDN_HEADS)],
            axis=1)
        sp_in = bcast(SMALL_BA) + dtb_ref[...]
        softplus = jnp.maximum(sp_in, 0.0) + jnp.log1p(jnp.exp(-jnp.abs(sp_in)))
        g = aneg_ref[...] * softplus
        beta = _sigmoid(bcast(SMALL_BB))
        gcum = _dot_exact_lhs(kl_ref[...], g)
        glast = jnp.concatenate(
            [jnp.broadcast_to(gcum[(c + 1) * CHUNK - 1:(c + 1) * CHUNK, :], (CHUNK, WIDTH))
             for c in range(CHUNKS_PER_TILE)], axis=0)
        eg = jnp.exp(gcum)
        egl = jnp.exp(glast)
        egd = jnp.exp(glast - gcum)
        egl_rows = jnp.zeros((8, WIDTH), F32)
        for c in range(CHUNKS_PER_TILE):
            egl_rows = jnp.where(row8 == c, egl[c * CHUNK:c * CHUNK + 8, :], egl_rows)
        egl_ref[tile * 8:(tile + 1) * 8, :] = egl_rows

        for h in range(GDN_HEADS):
            cs = slice(h * hd, (h + 1) * hd)
            qh = qkv[:, h * hd:(h + 1) * hd]
            kh = qkv[:, WIDTH + h * hd:WIDTH + (h + 1) * hd]
            vh = qkv[:, 2 * WIDTH + h * hd:2 * WIDTH + (h + 1) * hd]
            qh = qh * lax.rsqrt(jnp.sum(qh * qh, axis=-1, keepdims=True) + EPS) * (hd ** -0.5)
            kh = kh * lax.rsqrt(jnp.sum(kh * kh, axis=-1, keepdims=True) + EPS)
            bh = beta[:, cs]
            gc = gcum[:, cs]
            dmat = jnp.concatenate([gc, gc], axis=1) - gc.T[0:1, :]
            gamma = jnp.exp(jnp.where(incl, dmat, -jnp.inf))
            kb = kh * bh
            kk_qk = _dot_nt(jnp.concatenate([kb, qh], axis=0), kh)
            a = jnp.where(strict, kk_qk[:t] * gamma, 0.0)
            invs.append(eye - a)
            bps.append(_bf(-a))
            rhss.append(_bf(jnp.concatenate([vh * bh, kb * eg[:, cs]], axis=1)))
            aqk_ref[h, rs, :] = _bf(kk_qk[t:] * gamma)
            qd_ref[rs, cs] = _bf(qh * eg[:, cs])
            kd_ref[rs, cs] = _bf(kh * egd[:, cs])
            dests.append((rs, cs))

    n_sq = 5
    bps = [_bf(jnp.dot(b, b, preferred_element_type=F32)) for b in bps]
    for k in range(n_sq):
        last = k == n_sq - 1
        prods = [jnp.dot(_bf(inv) if last else jnp.concatenate([_bf(inv), b], axis=0), b,
                         preferred_element_type=F32) for inv, b in zip(invs, bps)]
        invs = [inv + p[:t] for inv, p in zip(invs, prods)]
        if not last:
            bps = [_bf(p[t:]) for p in prods]
    for inv, rhs, (rs, cs) in zip(invs, rhss, dests):
        sol = jnp.dot(_bf(inv), rhs, preferred_element_type=F32)
        u_ref[rs, cs] = sol[:, :hd]
        w_ref[rs, cs] = _bf(sol[:, hd:])


def _gdn_prep(z, small, conv_w, aneg, dtb, layer, rows):
    s = z.shape[0]
    t = GDN_TILE
    pad = BF16_SUBLANES
    tiles = rows // t
    kl = _chunk_cumsum_matrix()
    row_spec = pl.BlockSpec((rows, WIDTH), lambda i: (i, 0))
    halo_spec = lambda col: pl.BlockSpec(
        (pad, WIDTH), lambda i: (jnp.maximum(i * (rows // pad) - 1, 0), col // WIDTH))
    return pl.pallas_call(
        _gdn_prep_kernel,
        grid=(s // rows,),
        in_specs=[pl.BlockSpec((rows, WIDTH), lambda i: (i, COL_BQ // WIDTH)),
                  pl.BlockSpec((rows, WIDTH), lambda i: (i, COL_BK // WIDTH)),
                  pl.BlockSpec((rows, WIDTH), lambda i: (i, COL_BV // WIDTH)),
                  halo_spec(COL_BQ), halo_spec(COL_BK), halo_spec(COL_BV),
                  pl.BlockSpec((rows, LANES), lambda i: (i, 0)),
                  _layer_spec((CONV_K, 3 * WIDTH), layer, 1),
                  _layer_spec((1, WIDTH), layer, 1),
                  _layer_spec((1, WIDTH), layer, 1),
                  pl.BlockSpec((t, t), lambda i: (0, 0))],
        out_specs=[row_spec, row_spec, row_spec, row_spec,
                   pl.BlockSpec((GDN_HEADS, rows, t), lambda i: (0, i, 0)),
                   pl.BlockSpec((8 * tiles, WIDTH), lambda i: (i, 0))],
        out_shape=[jax.ShapeDtypeStruct((s, WIDTH), BF16),
                   jax.ShapeDtypeStruct((s, WIDTH), BF16),
                   jax.ShapeDtypeStruct((s, WIDTH), F32),
                   jax.ShapeDtypeStruct((s, WIDTH), BF16),
                   jax.ShapeDtypeStruct((GDN_HEADS, s, t), BF16),
                   jax.ShapeDtypeStruct((s // t * 8, WIDTH), F32)],
        scratch_shapes=[pltpu.VMEM((rows + pad, 3 * WIDTH), F32)],
        compiler_params=_cparams(("parallel",)),
        name="gdn_prep",
    )(z, z, z, z, z, z, small, conv_w, aneg, dtb, kl)


def _gdn_scan_kernel(qd_ref, kd_ref, u_ref, w_ref, aqk_ref, egl_ref, o_ref, state_ref, vn_ref):
    hd = GDN_HEAD_DIM

    @pl.when(pl.program_id(0) == 0)
    def _():
        state_ref[...] = jnp.zeros_like(state_ref)
        vn_ref[...] = jnp.zeros_like(vn_ref)

    heads = range(GDN_HEADS)
    col = lambda h: slice(h * hd, (h + 1) * hd)
    for c in range(CHUNKS_PER_TILE):
        rs = slice(c * CHUNK, (c + 1) * CHUNK)
        sts = [state_ref[h] for h in heads]
        rr = [jnp.dot(jnp.concatenate([w_ref[rs, col(h)], qd_ref[rs, col(h)]], axis=0),
                      _bf(sts[h]), preferred_element_type=F32) for h in heads]
        vns = [_bf(u_ref[rs, col(h)] - rr[h][:CHUNK]) for h in heads]
        for h in heads:
            vn_ref[h, rs, :] = vns[h]
        for h in heads:
            upd = lax.dot_general(kd_ref[rs, col(h)], vns[h], (((0,), (0,)), ((), ())),
                                  preferred_element_type=F32)
            state_ref[h] = sts[h] * egl_ref[c:c + 1, col(h)] + upd
        for h in heads:
            o_ref[rs, col(h)] = rr[h][CHUNK:] + jnp.dot(aqk_ref[h, rs, :], vn_ref[h],
                                                        preferred_element_type=F32)


def _gdn_scan(qd, kd, u, w, aqk, egl):
    s = qd.shape[0]
    t = GDN_TILE
    row_spec = pl.BlockSpec((t, WIDTH), lambda i: (i, 0))
    return pl.pallas_call(
        _gdn_scan_kernel,
        grid=(s // t,),
        in_specs=[row_spec, row_spec, row_spec, row_spec,
                  pl.BlockSpec((GDN_HEADS, t, t), lambda i: (0, i, 0)),
                  pl.BlockSpec((8, WIDTH), lambda i: (i, 0))],
        out_specs=row_spec,
        out_shape=jax.ShapeDtypeStruct((s, WIDTH), F32),
        scratch_shapes=[pltpu.VMEM((GDN_HEADS, GDN_HEAD_DIM, GDN_HEAD_DIM), F32),
                        pltpu.VMEM((GDN_HEADS, t, GDN_HEAD_DIM), BF16)],
        compiler_params=_cparams(("arbitrary",)),
        name="gdn_scan",
    )(qd, kd, u, w, aqk, egl)


def _merge_kernel(x_ref, oa_ref, ob_ref, mq_ref, az_ref, bz_ref, mz_ref, g0_ref, g1_ref, g2_ref,
                  mk_ref, mv_ref, gng_ref, bm_ref, wb_ref, wo_ref, fg_ref, o_ref, *, final_norm):
    hd = GDN_HEAD_DIM
    groups = [slice(r0, r0 + MERGE_SUB) for r0 in range(0, x_ref.shape[0], MERGE_SUB)]
    g_refs = (g0_ref, g1_ref, g2_ref)

    def branch_a(rs):
        return oa_ref[rs, :] * _silu(az_ref[rs, :].astype(F32))

    def branch_b(rs):
        normed = []
        for h in range(GDN_HEADS):
            oh = ob_ref[rs, h * hd:(h + 1) * hd]
            normed.append(oh * lax.rsqrt(jnp.mean(oh * oh, axis=-1, keepdims=True) + EPS))
        return jnp.concatenate(normed, axis=1) * gng_ref[...] * _silu(bz_ref[rs, :].astype(F32))

    def branch_m(rs):
        om = []
        for h in range(MEM_HEADS):
            cs = slice(h * MEM_HEAD_DIM, (h + 1) * MEM_HEAD_DIM)
            sc = _dot_nt(mq_ref[rs, cs], mk_ref[:, cs])
            p = jnp.exp2((sc - jnp.max(sc, axis=-1, keepdims=True))
                         * (-NEG_LOG2_E * MEM_HEAD_DIM ** -0.5))
            om.append(jnp.dot(_bf(p), mv_ref[:, cs], preferred_element_type=F32)
                      / jnp.sum(p, axis=-1, keepdims=True))
        return jnp.concatenate(om, axis=1) * _silu(mz_ref[rs, :].astype(F32))

    merged = [None] * len(groups)
    for n, branch in enumerate((branch_a, branch_b, branch_m)):
        for k, rs in enumerate(groups):
            proj = jnp.dot(_bf(branch(rs)), wb_ref[n], preferred_element_type=F32)
            term = _sigmoid(g_refs[n][rs, :].astype(F32) + bm_ref[n:n + 1, :]) * proj
            merged[k] = term if merged[k] is None else merged[k] + term
    for k, rs in enumerate(groups):
        out = x_ref[rs, :] + jnp.dot(_bf(merged[k]), wo_ref[...], preferred_element_type=F32)
        if final_norm:
            out = out * lax.rsqrt(jnp.mean(out * out, axis=-1, keepdims=True) + EPS) * fg_ref[...]
        o_ref[rs, :] = out


def _merge(x, o_a, o_b, z, memkv, gng, bm, wb, wo, fg_row, layer, tm, final_norm):
    s = x.shape[0]
    zc = lambda col, width: pl.BlockSpec((tm, width), lambda i: (i, col // width))
    n_mem = memkv.shape[0]
    return pl.pallas_call(
        functools.partial(_merge_kernel, final_norm=final_norm),
        grid=(s // tm,),
        in_specs=[pl.BlockSpec((tm, D_MODEL), lambda i: (i, 0)),
                  pl.BlockSpec((tm, WIDTH), lambda i: (i, 0)),
                  pl.BlockSpec((tm, WIDTH), lambda i: (i, 0)),
                  zc(COL_MQ, WIDTH),
                  zc(COL_AZ, WIDTH), zc(COL_BZ, WIDTH), zc(COL_MZ, WIDTH),
                  zc(COL_GATES, D_MODEL), zc(COL_GATES + D_MODEL, D_MODEL),
                  zc(COL_GATES + 2 * D_MODEL, D_MODEL),
                  pl.BlockSpec((n_mem, WIDTH), lambda i: (0, 0)),
                  pl.BlockSpec((n_mem, WIDTH), lambda i: (0, 1)),
                  _layer_spec((1, WIDTH), layer, 1),
                  _layer_spec((N_BRANCH, D_MODEL), layer, 1),
                  _layer_spec((N_BRANCH, WIDTH, D_MODEL), layer, 1),
                  _layer_spec((D_MODEL, D_MODEL), layer, 1),
                  pl.BlockSpec((1, D_MODEL), lambda i: (0, 0))],
        out_specs=pl.BlockSpec((tm, D_MODEL), lambda i: (i, 0)),
        out_shape=jax.ShapeDtypeStruct((s, D_MODEL), F32),
        compiler_params=_cparams(("parallel",)),
        name="merge",
    )(x, o_a, o_b, z, z, z, z, z, z, z, memkv, memkv, gng, bm, wb, wo, fg_row)


def _main_row_start(c):
    col = c * WIDTH
    past_af = (col >= COL_AZ).astype(jnp.int32)
    past_ba = (col >= COL_BZ).astype(jnp.int32)
    return (c * (WIDTH // F32_SUBLANES) + past_af * (FOX_HEADS // F32_SUBLANES)
            + past_ba * (2 * GDN_HEADS // F32_SUBLANES))


NARROW_ROWS = ((SRC_AF, FOX_HEADS), (SRC_BA, 2 * GDN_HEADS))


def kernel(x, mem, norm_g, w_in, b_fg, b_merge, conv_w, a_log, dt_bias, gdn_norm_g, mem_norm_g,
           w_mem_kv, w_branch, w_out, final_norm_g):
    assert x.shape[0] == 1 and mem.shape[0] == 1
    depth = w_in.shape[0]
    s = x.shape[1]
    rows = lambda v: v.astype(F32).reshape(depth, 1, -1)
    assert w_in.shape[2] == N_SRC
    w_in_t = jnp.swapaxes(w_in, 1, 2)
    w_mem_t = jnp.swapaxes(w_mem_kv, 1, 2)
    w_branch_b, w_out_b = _bf(w_branch), _bf(w_out)
    norm_g3, mem_norm_g3 = rows(norm_g), rows(mem_norm_g)
    bfg3 = rows(jnp.pad(b_fg, ((0, 0), (0, LANES - FOX_HEADS))))
    aneg3 = rows(jnp.repeat(-jnp.exp(a_log.astype(F32)), GDN_HEAD_DIM, axis=1))
    dtb3 = rows(jnp.repeat(dt_bias, GDN_HEAD_DIM, axis=1))
    gng3 = rows(jnp.tile(gdn_norm_g, (1, GDN_HEADS)))
    bm3 = b_merge.reshape(depth, N_BRANCH, D_MODEL)
    fg_row = final_norm_g.reshape(1, D_MODEL)
    tq = min(s, ATTN_TQ)

    h = x[0]
    for l in range(depth):
        z, small = _norm_matmul(h, norm_g3, w_in_t, l, tm=min(s, NORM_TM), tn=NORM_TN,
                                n_out=N_MAIN, sub=WIDTH, row_start=_main_row_start,
                                narrow_rows=NARROW_ROWS)
        memkv, = _norm_matmul(mem[0], mem_norm_g3, w_mem_t, l, tm=mem.shape[1], tn=2 * WIDTH)
        qa, ka, va, stats = _attn_prep(z, small, bfg3, l, tm=min(s, ATTN_PREP_ROWS))
        o_a = _attention(_skip_starts(stats, ATTN_STAT_TILE, tq), qa, ka, va, tq=tq)
        qd, kd, u, w, aqk, egl = _gdn_prep(z, small, conv_w, aneg3, dtb3, l, rows=GDN_PREP_ROWS)
        o_b = _gdn_scan(qd, kd, u, w, aqk, egl)
        h = _merge(h, o_a, o_b, z, memkv, gng3, bm3, w_branch_b, w_out_b, fg_row, l,
                   tm=MERGE_TM, final_norm=(l == depth - 1))
    return h[None]
```

```python
import functools

import jax
import jax.numpy as jnp
import numpy as np
from jax import lax
from jax.experimental import pallas as pl
from jax.experimental.pallas import tpu as pltpu

F32 = jnp.float32
BF16 = jnp.bfloat16

D_MODEL = 1024
EPS = 1e-6
FOX_HEADS = 8
FOX_HEAD_DIM = 64
GDN_HEADS = 4
GDN_HEAD_DIM = 128
MEM_HEADS = 4
MEM_HEAD_DIM = 128
WIDTH = 512
N_BRANCH = 3
CHUNK = 64
CONV_K = 4
LANES = 128
F32_SUBLANES = 8
BF16_SUBLANES = 16
GDN_TILE = 256
CHUNKS_PER_TILE = GDN_TILE // CHUNK

SRC_AF = 3 * WIDTH
SRC_BA = SRC_AF + FOX_HEADS + 4 * WIDTH
N_SRC = 8208
COL_AQ, COL_AK, COL_AV, COL_AZ = 0, 512, 1024, 1536
COL_BQ, COL_BK, COL_BV, COL_BZ = 2048, 2560, 3072, 3584
COL_MQ, COL_MZ, COL_GATES = 4096, 4608, 5120
N_MAIN = 8192
SMALL_BA = 8
SMALL_BB = 12

NORM_TM = 2048
NORM_TN = 1024
ATTN_PREP_ROWS = 1024
ATTN_STAT_TILE = 256
ATTN_TQ = 512
GDN_PREP_ROWS = 512
MERGE_TM = 512
MERGE_SUB = 512

SKIP_LOG_MARGIN = 90.0
NORM_SLACK = 1.01

VMEM_LIMIT = 56 * 1024 * 1024


def _cparams(sem):
    return pltpu.CompilerParams(dimension_semantics=sem, vmem_limit_bytes=VMEM_LIMIT)


def _bf(x):
    return x.astype(BF16)


def _dot_nt(a, b):
    return lax.dot_general(_bf(a), _bf(b), (((1,), (1,)), ((), ())), preferred_element_type=F32)


NEG_LOG2_E = -1.4426950408889634


def _sigmoid(x):
    return 1.0 / (1.0 + jnp.exp2(x * NEG_LOG2_E))


def _silu(x):
    return x * _sigmoid(x)


def _split3(x):
    hi = _bf(x).astype(F32)
    r1 = x - hi
    mid = _bf(r1).astype(F32)
    lo = _bf(r1 - mid).astype(F32)
    return hi, mid, lo


def _dot_exact_lhs(mat01, x):
    hi, mid, lo = _split3(x)
    return (jnp.dot(mat01, _bf(hi), preferred_element_type=F32)
            + jnp.dot(mat01, _bf(mid), preferred_element_type=F32)
            + jnp.dot(mat01, _bf(lo), preferred_element_type=F32))


def _layer_spec(shape, layer, n_grid):
    zeros = (0,) * len(shape)
    if n_grid == 1:
        return pl.BlockSpec((None,) + tuple(shape), lambda i: (layer,) + zeros)
    return pl.BlockSpec((None,) + tuple(shape), lambda i, j: (layer,) + zeros)


def _norm_matmul_kernel(x_ref, g_ref, *rest, n_w, n_narrow):
    w_refs, narrow_refs, outs = rest[:n_w], rest[n_w:n_w + n_narrow], rest[n_w + n_narrow:]
    if n_narrow:
        o_ref, os_ref, h_ref = outs
    else:
        o_ref, h_ref = outs

    @pl.when(pl.program_id(1) == 0)
    def _():
        x = x_ref[...]
        y = x * lax.rsqrt(jnp.mean(x * x, axis=-1, keepdims=True) + EPS)
        h = _bf(y * g_ref[...])
        h_ref[...] = h
        if n_narrow:
            rows = [r[...] for r in narrow_refs]
            used = sum(r.shape[0] for r in rows)
            rows.append(jnp.zeros((LANES - used, x.shape[1]), rows[0].dtype))
            os_ref[...] = _dot_nt(h, jnp.concatenate(rows, axis=0))

    sub = o_ref.shape[1] // n_w
    for k, w_ref in enumerate(w_refs):
        o_ref[:, k * sub:(k + 1) * sub] = _dot_nt(h_ref[...], w_ref[...]).astype(o_ref.dtype)


def _norm_matmul(x, g, w_t, layer, tm, tn, n_out=None, sub=None, row_start=None, narrow_rows=()):
    s, d = x.shape
    n_out = w_t.shape[1] if n_out is None else n_out
    sub = tn if sub is None else sub
    n_w = tn // sub
    n_src = w_t.shape[1]
    w_rows = w_t.reshape(w_t.shape[0] * n_src, d)
    unit = F32_SUBLANES
    window = lambda rows, start: pl.BlockSpec(
        (pl.Element(rows), pl.Element(d)),
        lambda i, j: ((layer * (n_src // unit) + start(j)) * unit, 0))
    if row_start is None:
        w_specs = [window(tn, lambda j: j * (tn // unit))]
    else:
        w_specs = [window(sub, lambda j, k=k: row_start(j * n_w + k)) for k in range(n_w)]
    narrow_specs = [window(cnt, lambda j, r=r: r // unit) for r, cnt in narrow_rows]
    out_specs = [pl.BlockSpec((tm, tn), lambda i, j: (i, j))]
    out_shape = [jax.ShapeDtypeStruct((s, n_out), BF16)]
    if narrow_rows:
        out_specs.append(pl.BlockSpec((tm, LANES), lambda i, j: (i, 0)))
        out_shape.append(jax.ShapeDtypeStruct((s, LANES), F32))
    return pl.pallas_call(
        functools.partial(_norm_matmul_kernel, n_w=n_w, n_narrow=len(narrow_rows)),
        grid=(s // tm, n_out // tn),
        in_specs=[pl.BlockSpec((tm, d), lambda i, j: (i, 0)), _layer_spec((1, d), layer, 2)]
        + w_specs + narrow_specs,
        out_specs=out_specs, out_shape=out_shape,
        scratch_shapes=[pltpu.VMEM((tm, d), BF16)],
        compiler_params=_cparams(("parallel", "arbitrary")),
        name="norm_matmul",
    )(x, g, *([w_rows] * (n_w + len(narrow_rows))))


FL_HI, FL_MID, FL_LO, FL_ONE = 0, 8, 16, 24
ST_FFIRST, ST_FLAST, ST_QN2, ST_KN2, ST_DMIN, ST_ROWS = 0, 1, 2, 6, 10, 16


def _aug_placement():
    d = FOX_HEAD_DIM
    wq, wk, wv = (np.zeros((FOX_HEADS // 2, 2 * LANES, 2 * LANES), np.float32) for _ in range(3))
    for p in range(FOX_HEADS // 2):
        for half in range(2):
            h = 2 * p + half
            feat = half * LANES + half * d
            aux = half * LANES + (1 - half) * d
            for w, scale in ((wq, d ** -0.5), (wk, 1.0), (wv, 1.0)):
                w[p, half * d + np.arange(d), feat + np.arange(d)] = scale
            for a, src in enumerate((FL_HI, FL_MID, FL_LO)):
                wq[p, LANES + src + h, aux + a] = 1.0
                wk[p, LANES + src + h, aux + 3 + a] = -1.0
            wq[p, LANES + FL_ONE, aux + 3:aux + 6] = 1.0
            wk[p, LANES + FL_ONE, aux:aux + 3] = 1.0
            wv[p, LANES + FL_ONE, aux:aux + d] = 1.0
    return [jnp.asarray(w, BF16) for w in (wq, wk, wv)]


def _attn_prep_kernel(q_ref, k_ref, v_ref, small_ref, bfg_ref, tril_ref, wq_ref, wk_ref, wv_ref,
                      nrm_ref, qa_ref, ka_ref, va_ref, stats_ref, carry_ref):
    rows = q_ref.shape[0]
    t = ATTN_STAT_TILE
    subs = [slice(r0, r0 + t) for r0 in range(0, rows, t)]

    @pl.when(pl.program_id(0) == 0)
    def _():
        carry_ref[...] = jnp.zeros_like(carry_ref)

    af = small_ref[...] + bfg_ref[...]
    logf = jnp.minimum(af, 0.0) - jnp.log1p(jnp.exp(-jnp.abs(af)))
    offset = carry_ref[...]
    cums = []
    for rs in subs:
        cums.append(_dot_exact_lhs(tril_ref[...], logf[rs, :]) + offset)
        offset = cums[-1][t - 1:t, :]
    carry_ref[...] = offset
    cum = jnp.concatenate(cums, axis=0)

    lane = lax.broadcasted_iota(jnp.int32, (rows, LANES), 1)
    hi, mid, lo = _split3(cum)
    fl = jnp.where(lane < FL_MID, hi,
                   jnp.where(lane < FL_LO, pltpu.roll(mid, FL_MID, 1),
                             jnp.where(lane < FL_ONE, pltpu.roll(lo, FL_LO, 1),
                                       jnp.where(lane == FL_ONE, 1.0, 0.0))))
    fl = _bf(fl)

    row = lax.broadcasted_iota(jnp.int32, (ST_ROWS, LANES), 0)
    put = lambda st, r, vec: jnp.where(row == r, jnp.broadcast_to(vec, (ST_ROWS, LANES)), st)
    stats = [put(put(jnp.zeros((ST_ROWS, LANES), F32), ST_FFIRST, c[0:1, :]),
                 ST_FLAST, c[t - 1:t, :]) for c in cums]
    for p in range(FOX_HEADS // 2):
        cols = slice(p * LANES, (p + 1) * LANES)
        for x_ref, w_ref, out_ref in ((q_ref, wq_ref, qa_ref), (k_ref, wk_ref, ka_ref),
                                      (v_ref, wv_ref, va_ref)):
            aug = jnp.dot(jnp.concatenate([x_ref[:, cols], fl], axis=1), w_ref[p],
                          preferred_element_type=F32)
            out_ref[2 * p] = _bf(aug[:, :LANES])
            out_ref[2 * p + 1] = _bf(aug[:, LANES:])
        xq, xk = q_ref[:, cols], k_ref[:, cols]
        n2 = jnp.dot(jnp.concatenate([xq * xq, xk * xk], axis=1), nrm_ref[...],
                     preferred_element_type=F32)
        diag = jnp.dot(xq * xk, nrm_ref[:LANES, :LANES], preferred_element_type=F32)
        for k, rs in enumerate(subs):
            top = jnp.max(n2[rs, :], axis=0, keepdims=True)
            stats[k] = put(put(stats[k], ST_QN2 + p, top[:, :LANES]), ST_KN2 + p, top[:, LANES:])
            stats[k] = put(stats[k], ST_DMIN + p, jnp.min(diag[rs, :], axis=0, keepdims=True))
    for k in range(len(subs)):
        stats_ref[k] = stats[k]


def _attn_prep(z, small, bfg, layer, tm):
    s = z.shape[0]
    t = ATTN_STAT_TILE
    tril = jnp.asarray(np.tril(np.ones((t, t), np.float32)), BF16)
    half = np.arange(2 * LANES) // FOX_HEAD_DIM
    nrm = jnp.asarray((half[:, None] == half[None, :]).astype(np.float32), BF16)
    places = _aug_placement()
    out_sds = jax.ShapeDtypeStruct((FOX_HEADS, s, LANES), BF16)
    out_spec = pl.BlockSpec((FOX_HEADS, tm, LANES), lambda i: (0, i, 0))
    const = lambda a: pl.BlockSpec(a.shape, lambda i: (0,) * a.ndim)
    return pl.pallas_call(
        _attn_prep_kernel,
        grid=(s // tm,),
        in_specs=[pl.BlockSpec((tm, WIDTH), lambda i: (i, COL_AQ // WIDTH)),
                  pl.BlockSpec((tm, WIDTH), lambda i: (i, COL_AK // WIDTH)),
                  pl.BlockSpec((tm, WIDTH), lambda i: (i, COL_AV // WIDTH)),
                  pl.BlockSpec((tm, LANES), lambda i: (i, 0)),
                  _layer_spec((1, LANES), layer, 1),
                  const(tril)] + [const(w) for w in places] + [const(nrm)],
        out_specs=[out_spec, out_spec, out_spec,
                   pl.BlockSpec((tm // t, ST_ROWS, LANES), lambda i: (i, 0, 0))],
        out_shape=[out_sds, out_sds, out_sds,
                   jax.ShapeDtypeStruct((s // t, ST_ROWS, LANES), F32)],
        scratch_shapes=[pltpu.VMEM((1, LANES), F32)],
        compiler_params=_cparams(("arbitrary",)),
        name="attn_prep",
    )(z, z, z, small, bfg, tril, *places, nrm)


def _attn_kernel(jlo_ref, q_ref, k_ref, v_ref, o_ref, s0_ref, s1_ref, s2_ref, s3_ref, *, tq, tk):
    pair = pl.program_id(0)
    i = pl.program_id(1)
    lane = lax.broadcasted_iota(jnp.int32, (tq, LANES), 1)
    n_tile = tk // LANES

    def logits(q, hh, j, s_ref):
        start = pl.multiple_of(j * tk, tk)
        s_ref[...] = lax.dot_general(q, k_ref[hh, pl.ds(start, tk), :], (((1,), (1,)), ((), ())),
                                     preferred_element_type=F32)

    def softmax_pv(hh, j, s_ref, carry, masked):
        m, acc = carry
        start = pl.multiple_of(j * tk, tk)
        if masked:
            row = lax.broadcasted_iota(jnp.int32, (tq, tk), 0)
            col = lax.broadcasted_iota(jnp.int32, (tq, tk), 1)
            s_ref[...] = jnp.where(col <= row, s_ref[...], -jnp.inf)
        m_new = jnp.maximum(m, jnp.max(s_ref[...], axis=1, keepdims=True))
        p = jnp.exp(s_ref[...] - jnp.tile(m_new, (1, n_tile)))
        alpha = jnp.exp(m - m_new)
        acc = alpha * acc + jnp.dot(_bf(p), v_ref[hh, pl.ds(start, tk), :],
                                    preferred_element_type=F32)
        return m_new, acc

    bufs = ((s0_ref, s1_ref), (s2_ref, s3_ref))
    jlos = [jlo_ref[(2 * pair + hh) * pl.num_programs(1) + i] for hh in range(2)]
    logits(q_ref[0], 0, jlos[0], bufs[0][0])
    accs = []
    for hh in range(2):
        q = q_ref[hh]
        jlo = jlos[hh]
        cur, nxt = bufs[hh]
        n_full = i - jlo

        def two_blocks(t, carry, q=q, hh=hh, jlo=jlo, cur=cur, nxt=nxt):
            j = jlo + 2 * t
            logits(q, hh, j + 1, nxt)
            carry = softmax_pv(hh, j, cur, carry, False)
            logits(q, hh, j + 2, cur)
            return softmax_pv(hh, j + 1, nxt, carry, False)

        def start_other_head(hh=hh):
            if hh == 0:
                logits(q_ref[1], 1, jlos[1], bufs[1][0])

        def tail_odd(carry, q=q, hh=hh, cur=cur, nxt=nxt, start=start_other_head):
            logits(q, hh, i, nxt)
            start()
            carry = softmax_pv(hh, i - 1, cur, carry, False)
            return softmax_pv(hh, i, nxt, carry, True)

        def tail_even(carry, hh=hh, cur=cur, start=start_other_head):
            start()
            return softmax_pv(hh, i, cur, carry, True)

        def four_blocks(t, carry, two=two_blocks):
            return two(2 * t + 1, two(2 * t, carry))

        n_quads = n_full // 4
        carry = (jnp.full((tq, LANES), -jnp.inf, F32), jnp.zeros((tq, LANES), F32))
        carry = lax.fori_loop(0, n_quads, four_blocks, carry)
        carry = lax.fori_loop(2 * n_quads, n_full // 2, two_blocks, carry)
        _, acc = lax.cond(n_full % 2 == 1, tail_odd, tail_even, carry)
        accs.append(acc)
    o0 = accs[0] / pltpu.roll(accs[0], FOX_HEAD_DIM, 1)
    o1 = accs[1] / pltpu.roll(accs[1], FOX_HEAD_DIM, 1)
    o_ref[...] = jnp.where(lane < FOX_HEAD_DIM, o0, o1)


def _skip_starts(stats, tm, tq):
    r = tq // tm
    nq = stats.shape[0] // r
    scale = FOX_HEAD_DIM ** -0.5
    st = stats.reshape(nq, r, ST_ROWS, LANES)
    per_head = lambda row0: jnp.stack(
        [st[:, :, row0 + h // 2, FOX_HEAD_DIM * (h % 2)] for h in range(FOX_HEADS)], axis=-1)
    f_first = st[:, 0, ST_FFIRST, :FOX_HEADS].T
    f_last = st[:, r - 1, ST_FLAST, :FOX_HEADS].T
    qn = jnp.sqrt(jnp.max(per_head(ST_QN2), axis=1)).T
    kn = jnp.sqrt(jnp.max(per_head(ST_KN2), axis=(0, 1)))[:, None]
    bound = NORM_SLACK * scale * qn * kn
    dmin = scale * jnp.min(per_head(ST_DMIN), axis=1).T - (NORM_SLACK - 1.0) * bound
    thresh = f_first + bound - dmin + SKIP_LOG_MARGIN
    need = f_last[:, None, :] <= thresh[:, :, None]
    idx = jnp.arange(nq)
    first = jnp.min(jnp.where(need, idx[None, None, :], nq), axis=-1)
    return jnp.minimum(first, idx[None, :]).astype(jnp.int32).reshape(-1)


def _attention(jlo, qa, ka, va, tq):
    s = qa.shape[1]
    tk = tq
    kv_spec = pl.BlockSpec((2, s, LANES), lambda p, i, jl: (p, 0, 0))
    return pl.pallas_call(
        functools.partial(_attn_kernel, tq=tq, tk=tk),
        grid_spec=pltpu.PrefetchScalarGridSpec(
            num_scalar_prefetch=1,
            grid=(FOX_HEADS // 2, s // tq),
            in_specs=[pl.BlockSpec((2, tq, LANES), lambda p, i, jl: (p, i, 0)), kv_spec, kv_spec],
            out_specs=pl.BlockSpec((tq, LANES), lambda p, i, jl: (i, p)),
            scratch_shapes=[pltpu.VMEM((tq, tk), F32)] * 4),
        out_shape=jax.ShapeDtypeStruct((s, WIDTH), F32),
        compiler_params=_cparams(("parallel", "arbitrary")),
        name="fox_attention",
    )(jlo, qa, ka, va)


def _chunk_cumsum_matrix():
    r = np.arange(GDN_TILE)
    same = (r[:, None] // CHUNK) == (r[None, :] // CHUNK)
    return jnp.asarray((same & (r[None, :] <= r[:, None])).astype(np.float32), BF16)


def _gdn_prep_kernel(xq_ref, xk_ref, xv_ref, hq_ref, hk_ref, hv_ref, small_ref, cw_ref, aneg_ref,
                     dtb_ref, kl_ref, qd_ref, kd_ref, u_ref, w_ref, aqk_ref, egl_ref, xx_ref):
    t = GDN_TILE
    hd = GDN_HEAD_DIM
    pad = BF16_SUBLANES
    rows_in = xq_ref.shape[0]

    for n, (x_ref, halo_ref) in enumerate(((xq_ref, hq_ref), (xk_ref, hk_ref), (xv_ref, hv_ref))):
        cs = slice(n * WIDTH, (n + 1) * WIDTH)
        halo = halo_ref[...].astype(F32)
        xx_ref[0:pad, cs] = jnp.where(pl.program_id(0) == 0, jnp.zeros_like(halo), halo)
        xx_ref[pad:pad + rows_in, cs] = x_ref[...].astype(F32)

    row = lax.broadcasted_iota(jnp.int32, (t, t), 0)
    col = lax.broadcasted_iota(jnp.int32, (t, t), 1)
    same = (row // CHUNK) == (col // CHUNK)
    incl = jnp.logical_and(same, col <= row)
    strict = jnp.logical_and(same, col < row)
    eye = (row == col).astype(F32)
    row8 = lax.broadcasted_iota(jnp.int32, (8, WIDTH), 0)

    invs, bps, rhss, dests = [], [], [], []
    for tile in range(rows_in // t):
        r0 = tile * t
        rs = slice(r0, r0 + t)
        conv = cw_ref[CONV_K - 1:CONV_K, :] * xx_ref[pad + r0:pad + r0 + t, :]
        for j in range(CONV_K - 1):
            off = pad + r0 - (CONV_K - 1) + j
            conv = conv + cw_ref[j:j + 1, :] * xx_ref[off:off + t, :]
        qkv = _silu(conv)

        small = small_ref[rs, :]
        bcast = lambda c0: jnp.concatenate(
            [jnp.broadcast_to(small[:, c0 + h:c0 + h + 1], (t, hd)) for h in range(GDN_HEADS)],
            axis=1)
        sp_in = bcast(SMALL_BA) + dtb_ref[...]
        softplus = jnp.maximum(sp_in, 0.0) + jnp.log1p(jnp.exp(-jnp.abs(sp_in)))
        g = aneg_ref[...] * softplus
        beta = _sigmoid(bcast(SMALL_BB))
        gcum = _dot_exact_lhs(kl_ref[...], g)
        glast = jnp.concatenate(
            [jnp.broadcast_to(gcum[(c + 1) * CHUNK - 1:(c + 1) * CHUNK, :], (CHUNK, WIDTH))
             for c in range(CHUNKS_PER_TILE)], axis=0)
        eg = jnp.exp(gcum)
        egl = jnp.exp(glast)
        egd = jnp.exp(glast - gcum)
        egl_rows = jnp.zeros((8, WIDTH), F32)
        for c in range(CHUNKS_PER_TILE):
            egl_rows = jnp.where(row8 == c, egl[c * CHUNK:c * CHUNK + 8, :], egl_rows)
        egl_ref[tile * 8:(tile + 1) * 8, :] = egl_rows

        for h in range(GDN_HEADS):
            cs = slice(h * hd, (h + 1) * hd)
            qh = qkv[:, h * hd:(h + 1) * hd]
            kh = qkv[:, WIDTH + h * hd:WIDTH + (h + 1) * hd]
            vh = qkv[:, 2 * WIDTH + h * hd:2 * WIDTH + (h + 1) * hd]
            qh = qh * lax.rsqrt(jnp.sum(qh * qh, axis=-1, keepdims=True) + EPS) * (hd ** -0.5)
            kh = kh * lax.rsqrt(jnp.sum(kh * kh, axis=-1, keepdims=True) + EPS)
            bh = beta[:, cs]
            gc = gcum[:, cs]
            dmat = jnp.concatenate([gc, gc], axis=1) - gc.T[0:1, :]
            gamma = jnp.exp(jnp.where(incl, dmat, -jnp.inf))
            kb = kh * bh
            kk_qk = _dot_nt(jnp.concatenate([kb, qh], axis=0), kh)
            a = jnp.where(strict, kk_qk[:t] * gamma, 0.0)
            invs.append(eye - a)
            bps.append(_bf(-a))
            rhss.append(_bf(jnp.concatenate([vh * bh, kb * eg[:, cs]], axis=1)))
            aqk_ref[h, rs, :] = _bf(kk_qk[t:] * gamma)
            qd_ref[rs, cs] = _bf(qh * eg[:, cs])
            kd_ref[rs, cs] = _bf(kh * egd[:, cs])
            dests.append((rs, cs))

    n_sq = 5
    bps = [_bf(jnp.dot(b, b, preferred_element_type=F32)) for b in bps]
    for k in range(n_sq):
        last = k == n_sq - 1
        prods = [jnp.dot(_bf(inv) if last else jnp.concatenate([_bf(inv), b], axis=0), b,
                         preferred_element_type=F32) for inv, b in zip(invs, bps)]
        invs = [inv + p[:t] for inv, p in zip(invs, prods)]
        if not last:
            bps = [_bf(p[t:]) for p in prods]
    for inv, rhs, (rs, cs) in zip(invs, rhss, dests):
        sol = jnp.dot(_bf(inv), rhs, preferred_element_type=F32)
        u_ref[rs, cs] = sol[:, :hd]
        w_ref[rs, cs] = _bf(sol[:, hd:])


def _gdn_prep(z, small, conv_w, aneg, dtb, layer, rows):
    s = z.shape[0]
    t = GDN_TILE
    pad = BF16_SUBLANES
    tiles = rows // t
    kl = _chunk_cumsum_matrix()
    row_spec = pl.BlockSpec((rows, WIDTH), lambda i: (i, 0))
    halo_spec = lambda col: pl.BlockSpec(
        (pad, WIDTH), lambda i: (jnp.maximum(i * (rows // pad) - 1, 0), col // WIDTH))
    return pl.pallas_call(
        _gdn_prep_kernel,
        grid=(s // rows,),
        in_specs=[pl.BlockSpec((rows, WIDTH), lambda i: (i, COL_BQ // WIDTH)),
                  pl.BlockSpec((rows, WIDTH), lambda i: (i, COL_BK // WIDTH)),
                  pl.BlockSpec((rows, WIDTH), lambda i: (i, COL_BV // WIDTH)),
                  halo_spec(COL_BQ), halo_spec(COL_BK), halo_spec(COL_BV),
                  pl.BlockSpec((rows, LANES), lambda i: (i, 0)),
                  _layer_spec((CONV_K, 3 * WIDTH), layer, 1),
                  _layer_spec((1, WIDTH), layer, 1),
                  _layer_spec((1, WIDTH), layer, 1),
                  pl.BlockSpec((t, t), lambda i: (0, 0))],
        out_specs=[row_spec, row_spec, row_spec, row_spec,
                   pl.BlockSpec((GDN_HEADS, rows, t), lambda i: (0, i, 0)),
                   pl.BlockSpec((8 * tiles, WIDTH), lambda i: (i, 0))],
        out_shape=[jax.ShapeDtypeStruct((s, WIDTH), BF16),
                   jax.ShapeDtypeStruct((s, WIDTH), BF16),
                   jax.ShapeDtypeStruct((s, WIDTH), F32),
                   jax.ShapeDtypeStruct((s, WIDTH), BF16),
                   jax.ShapeDtypeStruct((GDN_HEADS, s, t), BF16),
                   jax.ShapeDtypeStruct((s // t * 8, WIDTH), F32)],
        scratch_shapes=[pltpu.VMEM((rows + pad, 3 * WIDTH), F32)],
        compiler_params=_cparams(("parallel",)),
        name="gdn_prep",
    )(z, z, z, z, z, z, small, conv_w, aneg, dtb, kl)


def _gdn_scan_kernel(qd_ref, kd_ref, u_ref, w_ref, aqk_ref, egl_ref, o_ref, state_ref, vn_ref):
    hd = GDN_HEAD_DIM

    @pl.when(pl.program_id(0) == 0)
    def _():
        state_ref[...] = jnp.zeros_like(state_ref)
        vn_ref[...] = jnp.zeros_like(vn_ref)

    heads = range(GDN_HEADS)
    col = lambda h: slice(h * hd, (h + 1) * hd)
    for c in range(CHUNKS_PER_TILE):
        rs = slice(c * CHUNK, (c + 1) * CHUNK)
        sts = [state_ref[h] for h in heads]
        rr = [jnp.dot(jnp.concatenate([w_ref[rs, col(h)], qd_ref[rs, col(h)]], axis=0),
                      _bf(sts[h]), preferred_element_type=F32) for h in heads]
        vns = [_bf(u_ref[rs, col(h)] - rr[h][:CHUNK]) for h in heads]
        for h in heads:
            vn_ref[h, rs, :] = vns[h]
        for h in heads:
            upd = lax.dot_general(kd_ref[rs, col(h)], vns[h], (((0,), (0,)), ((), ())),
                                  preferred_element_type=F32)
            state_ref[h] = sts[h] * egl_ref[c:c + 1, col(h)] + upd
        for h in heads:
            o_ref[rs, col(h)] = rr[h][CHUNK:] + jnp.dot(aqk_ref[h, rs, :], vn_ref[h],
                                                        preferred_element_type=F32)


def _gdn_scan(qd, kd, u, w, aqk, egl):
    s = qd.shape[0]
    t = GDN_TILE
    row_spec = pl.BlockSpec((t, WIDTH), lambda i: (i, 0))
    return pl.pallas_call(
        _gdn_scan_kernel,
        grid=(s // t,),
        in_specs=[row_spec, row_spec, row_spec, row_spec,
                  pl.BlockSpec((GDN_HEADS, t, t), lambda i: (0, i, 0)),
                  pl.BlockSpec((8, WIDTH), lambda i: (i, 0))],
        out_specs=row_spec,
        out_shape=jax.ShapeDtypeStruct((s, WIDTH), F32),
        scratch_shapes=[pltpu.VMEM((GDN_HEADS, GDN_HEAD_DIM, GDN_HEAD_DIM), F32),
                        pltpu.VMEM((GDN_HEADS, t, GDN_HEAD_DIM), BF16)],
        compiler_params=_cparams(("arbitrary",)),
        name="gdn_scan",
    )(qd, kd, u, w, aqk, egl)


def _merge_kernel(x_ref, oa_ref, ob_ref, mq_ref, az_ref, bz_ref, mz_ref, g0_ref, g1_ref, g2_ref,
                  mk_ref, mv_ref, gng_ref, bm_ref, wb_ref, wo_ref, fg_ref, o_ref, *, final_norm):
    hd = GDN_HEAD_DIM
    groups = [slice(r0, r0 + MERGE_SUB) for r0 in range(0, x_ref.shape[0], MERGE_SUB)]
    g_refs = (g0_ref, g1_ref, g2_ref)

    def branch_a(rs):
        return oa_ref[rs, :] * _silu(az_ref[rs, :].astype(F32))

    def branch_b(rs):
        normed = []
        for h in range(GDN_HEADS):
            oh = ob_ref[rs, h * hd:(h + 1) * hd]
            normed.append(oh * lax.rsqrt(jnp.mean(oh * oh, axis=-1, keepdims=True) + EPS))
        return jnp.concatenate(normed, axis=1) * gng_ref[...] * _silu(bz_ref[rs, :].astype(F32))

    def branch_m(rs):
        om = []
        for h in range(MEM_HEADS):
            cs = slice(h * MEM_HEAD_DIM, (h + 1) * MEM_HEAD_DIM)
            sc = _dot_nt(mq_ref[rs, cs], mk_ref[:, cs])
            p = jnp.exp2((sc - jnp.max(sc, axis=-1, keepdims=True))
                         * (-NEG_LOG2_E * MEM_HEAD_DIM ** -0.5))
            om.append(jnp.dot(_bf(p), mv_ref[:, cs], preferred_element_type=F32)
                      / jnp.sum(p, axis=-1, keepdims=True))
        return jnp.concatenate(om, axis=1) * _silu(mz_ref[rs, :].astype(F32))

    merged = [None] * len(groups)
    for n, branch in enumerate((branch_a, branch_b, branch_m)):
        for k, rs in enumerate(groups):
            proj = jnp.dot(_bf(branch(rs)), wb_ref[n], preferred_element_type=F32)
            term = _sigmoid(g_refs[n][rs, :].astype(F32) + bm_ref[n:n + 1, :]) * proj
            merged[k] = term if merged[k] is None else merged[k] + term
    for k, rs in enumerate(groups):
        out = x_ref[rs, :] + jnp.dot(_bf(merged[k]), wo_ref[...], preferred_element_type=F32)
        if final_norm:
            out = out * lax.rsqrt(jnp.mean(out * out, axis=-1, keepdims=True) + EPS) * fg_ref[...]
        o_ref[rs, :] = out


def _merge(x, o_a, o_b, z, memkv, gng, bm, wb, wo, fg_row, layer, tm, final_norm):
    s = x.shape[0]
    zc = lambda col, width: pl.BlockSpec((tm, width), lambda i: (i, col // width))
    n_mem = memkv.shape[0]
    return pl.pallas_call(
        functools.partial(_merge_kernel, final_norm=final_norm),
        grid=(s // tm,),
        in_specs=[pl.BlockSpec((tm, D_MODEL), lambda i: (i, 0)),
                  pl.BlockSpec((tm, WIDTH), lambda i: (i, 0)),
                  pl.BlockSpec((tm, WIDTH), lambda i: (i, 0)),
                  zc(COL_MQ, WIDTH),
                  zc(COL_AZ, WIDTH), zc(COL_BZ, WIDTH), zc(COL_MZ, WIDTH),
                  zc(COL_GATES, D_MODEL), zc(COL_GATES + D_MODEL, D_MODEL),
                  zc(COL_GATES + 2 * D_MODEL, D_MODEL),
                  pl.BlockSpec((n_mem, WIDTH), lambda i: (0, 0)),
                  pl.BlockSpec((n_mem, WIDTH), lambda i: (0, 1)),
                  _layer_spec((1, WIDTH), layer, 1),
                  _layer_spec((N_BRANCH, D_MODEL), layer, 1),
                  _layer_spec((N_BRANCH, WIDTH, D_MODEL), layer, 1),
                  _layer_spec((D_MODEL, D_MODEL), layer, 1),
                  pl.BlockSpec((1, D_MODEL), lambda i: (0, 0))],
        out_specs=pl.BlockSpec((tm, D_MODEL), lambda i: (i, 0)),
        out_shape=jax.ShapeDtypeStruct((s, D_MODEL), F32),
        compiler_params=_cparams(("parallel",)),
        name="merge",
    )(x, o_a, o_b, z, z, z, z, z, z, z, memkv, memkv, gng, bm, wb, wo, fg_row)


def _main_row_start(c):
    col = c * WIDTH
    past_af = (col >= COL_AZ).astype(jnp.int32)
    past_ba = (col >= COL_BZ).astype(jnp.int32)
    return (c * (WIDTH // F32_SUBLANES) + past_af * (FOX_HEADS // F32_SUBLANES)
            + past_ba * (2 * GDN_HEADS // F32_SUBLANES))


NARROW_ROWS = ((SRC_AF, FOX_HEADS), (SRC_BA, 2 * GDN_HEADS))


def kernel(x, mem, norm_g, w_in, b_fg, b_merge, conv_w, a_log, dt_bias, gdn_norm_g, mem_norm_g,
           w_mem_kv, w_branch, w_out, final_norm_g):
    assert x.shape[0] == 1 and mem.shape[0] == 1
    depth = w_in.shape[0]
    s = x.shape[1]
    rows = lambda v: v.astype(F32).reshape(depth, 1, -1)
    assert w_in.shape[2] == N_SRC
    w_in_t = jnp.swapaxes(w_in, 1, 2)
    w_mem_t = jnp.swapaxes(w_mem_kv, 1, 2)
    w_branch_b, w_out_b = _bf(w_branch), _bf(w_out)
    norm_g3, mem_norm_g3 = rows(norm_g), rows(mem_norm_g)
    bfg3 = rows(jnp.pad(b_fg, ((0, 0), (0, LANES - FOX_HEADS))))
    aneg3 = rows(jnp.repeat(-jnp.exp(a_log.astype(F32)), GDN_HEAD_DIM, axis=1))
    dtb3 = rows(jnp.repeat(dt_bias, GDN_HEAD_DIM, axis=1))
    gng3 = rows(jnp.tile(gdn_norm_g, (1, GDN_HEADS)))
    bm3 = b_merge.reshape(depth, N_BRANCH, D_MODEL)
    fg_row = final_norm_g.reshape(1, D_MODEL)
    tq = min(s, ATTN_TQ)

    h = x[0]
    for l in range(depth):
        z, small = _norm_matmul(h, norm_g3, w_in_t, l, tm=min(s, NORM_TM), tn=NORM_TN,
                                n_out=N_MAIN, sub=WIDTH, row_start=_main_row_start,
                                narrow_rows=NARROW_ROWS)
        memkv, = _norm_matmul(mem[0], mem_norm_g3, w_mem_t, l, tm=mem.shape[1], tn=2 * WIDTH)
        qa, ka, va, stats = _attn_prep(z, small, bfg3, l, tm=min(s, ATTN_PREP_ROWS))
        o_a = _attention(_skip_starts(stats, ATTN_STAT_TILE, tq), qa, ka, va, tq=tq)
        qd, kd, u, w, aqk, egl = _gdn_prep(z, small, conv_w, aneg3, dtb3, l, rows=GDN_PREP_ROWS)
        o_b = _gdn_scan(qd, kd, u, w, aqk, egl)
        h = _merge(h, o_a, o_b, z, memkv, gng3, bm3, w_branch_b, w_out_b, fg_row, l,
                   tm=MERGE_TM, final_norm=(l == depth - 1))
    return h[None]
```

```python
import functools

import jax
import jax.numpy as jnp
import numpy as np
from jax import lax
from jax.experimental import pallas as pl
from jax.experimental.pallas import tpu as pltpu

F32 = jnp.float32
BF16 = jnp.bfloat16

D_MODEL = 1024
EPS = 1e-6
FOX_HEADS = 8
FOX_HEAD_DIM = 64
GDN_HEADS = 4
GDN_HEAD_DIM = 128
MEM_HEADS = 4
MEM_HEAD_DIM = 128
WIDTH = 512
N_BRANCH = 3
CHUNK = 64
CONV_K = 4
LANES = 128
F32_SUBLANES = 8
BF16_SUBLANES = 16
GDN_TILE = 256
CHUNKS_PER_TILE = GDN_TILE // CHUNK

SRC_AF = 3 * WIDTH
SRC_BA = SRC_AF + FOX_HEADS + 4 * WIDTH
N_SRC = 8208
COL_AQ, COL_AK, COL_AV, COL_AZ = 0, 512, 1024, 1536
COL_BQ, COL_BK, COL_BV, COL_BZ = 2048, 2560, 3072, 3584
COL_MQ, COL_MZ, COL_GATES = 4096, 4608, 5120
N_MAIN = 8192
SMALL_BA = 8
SMALL_BB = 12

NORM_TM = 2048
NORM_TN = 1024
ATTN_PREP_ROWS = 1024
ATTN_STAT_TILE = 256
ATTN_TQ = 512
GDN_PREP_ROWS = 512
MERGE_TM = 512
MERGE_SUB = 512

SKIP_LOG_MARGIN = 90.0
NORM_SLACK = 1.01

VMEM_LIMIT = 56 * 1024 * 1024


def _cparams(sem):
    return pltpu.CompilerParams(dimension_semantics=sem, vmem_limit_bytes=VMEM_LIMIT)


def _bf(x):
    return x.astype(BF16)


def _dot_nt(a, b):
    return lax.dot_general(_bf(a), _bf(b), (((1,), (1,)), ((), ())), preferred_element_type=F32)


NEG_LOG2_E = -1.4426950408889634


def _sigmoid(x):
    return 1.0 / (1.0 + jnp.exp2(x * NEG_LOG2_E))


def _silu(x):
    return x * _sigmoid(x)


def _split3(x):
    hi = _bf(x).astype(F32)
    r1 = x - hi
    mid = _bf(r1).astype(F32)
    lo = _bf(r1 - mid).astype(F32)
    return hi, mid, lo


def _dot_exact_lhs(mat01, x):
    hi, mid, lo = _split3(x)
    return (jnp.dot(mat01, _bf(hi), preferred_element_type=F32)
            + jnp.dot(mat01, _bf(mid), preferred_element_type=F32)
            + jnp.dot(mat01, _bf(lo), preferred_element_type=F32))


def _layer_spec(shape, layer, n_grid):
    zeros = (0,) * len(shape)
    if n_grid == 1:
        return pl.BlockSpec((None,) + tuple(shape), lambda i: (layer,) + zeros)
    return pl.BlockSpec((None,) + tuple(shape), lambda i, j: (layer,) + zeros)


def _norm_matmul_kernel(x_ref, g_ref, *rest, n_w, n_narrow):
    w_refs, narrow_refs, outs = rest[:n_w], rest[n_w:n_w + n_narrow], rest[n_w + n_narrow:]
    if n_narrow:
        o_ref, os_ref, h_ref = outs
    else:
        o_ref, h_ref = outs

    @pl.when(pl.program_id(1) == 0)
    def _():
        x = x_ref[...]
        y = x * lax.rsqrt(jnp.mean(x * x, axis=-1, keepdims=True) + EPS)
        h = _bf(y * g_ref[...])
        h_ref[...] = h
        if n_narrow:
            rows = [r[...] for r in narrow_refs]
            used = sum(r.shape[0] for r in rows)
            rows.append(jnp.zeros((LANES - used, x.shape[1]), rows[0].dtype))
            os_ref[...] = _dot_nt(h, jnp.concatenate(rows, axis=0))

    sub = o_ref.shape[1] // n_w
    for k, w_ref in enumerate(w_refs):
        o_ref[:, k * sub:(k + 1) * sub] = _dot_nt(h_ref[...], w_ref[...]).astype(o_ref.dtype)


def _norm_matmul(x, g, w_t, layer, tm, tn, n_out=None, sub=None, row_start=None, narrow_rows=()):
    s, d = x.shape
    n_out = w_t.shape[1] if n_out is None else n_out
    sub = tn if sub is None else sub
    n_w = tn // sub
    n_src = w_t.shape[1]
    w_rows = w_t.reshape(w_t.shape[0] * n_src, d)
    unit = F32_SUBLANES
    window = lambda rows, start: pl.BlockSpec(
        (pl.Element(rows), pl.Element(d)),
        lambda i, j: ((layer * (n_src // unit) + start(j)) * unit, 0))
    if row_start is None:
        w_specs = [window(tn, lambda j: j * (tn // unit))]
    else:
        w_specs = [window(sub, lambda j, k=k: row_start(j * n_w + k)) for k in range(n_w)]
    narrow_specs = [window(cnt, lambda j, r=r: r // unit) for r, cnt in narrow_rows]
    out_specs = [pl.BlockSpec((tm, tn), lambda i, j: (i, j))]
    out_shape = [jax.ShapeDtypeStruct((s, n_out), BF16)]
    if narrow_rows:
        out_specs.append(pl.BlockSpec((tm, LANES), lambda i, j: (i, 0)))
        out_shape.append(jax.ShapeDtypeStruct((s, LANES), F32))
    return pl.pallas_call(
        functools.partial(_norm_matmul_kernel, n_w=n_w, n_narrow=len(narrow_rows)),
        grid=(s // tm, n_out // tn),
        in_specs=[pl.BlockSpec((tm, d), lambda i, j: (i, 0)), _layer_spec((1, d), layer, 2)]
        + w_specs + narrow_specs,
        out_specs=out_specs, out_shape=out_shape,
        scratch_shapes=[pltpu.VMEM((tm, d), BF16)],
        compiler_params=_cparams(("parallel", "arbitrary")),
        name="norm_matmul",
    )(x, g, *([w_rows] * (n_w + len(narrow_rows))))


FL_HI, FL_MID, FL_LO, FL_ONE = 0, 8, 16, 24
ST_FFIRST, ST_FLAST, ST_QN2, ST_KN2, ST_DMIN, ST_ROWS = 0, 1, 2, 6, 10, 16


def _aug_placement():
    d = FOX_HEAD_DIM
    wq, wk, wv = (np.zeros((FOX_HEADS // 2, 2 * LANES, 2 * LANES), np.float32) for _ in range(3))
    for p in range(FOX_HEADS // 2):
        for half in range(2):
            h = 2 * p + half
            feat = half * LANES + half * d
            aux = half * LANES + (1 - half) * d
            for w, scale in ((wq, d ** -0.5), (wk, 1.0), (wv, 1.0)):
                w[p, half * d + np.arange(d), feat + np.arange(d)] = scale
            for a, src in enumerate((FL_HI, FL_MID, FL_LO)):
                wq[p, LANES + src + h, aux + a] = 1.0
                wk[p, LANES + src + h, aux + 3 + a] = -1.0
            wq[p, LANES + FL_ONE, aux + 3:aux + 6] = 1.0
            wk[p, LANES + FL_ONE, aux:aux + 3] = 1.0
            wv[p, LANES + FL_ONE, aux:aux + d] = 1.0
    return [jnp.asarray(w, BF16) for w in (wq, wk, wv)]


def _attn_prep_kernel(q_ref, k_ref, v_ref, small_ref, bfg_ref, tril_ref, wq_ref, wk_ref, wv_ref,
                      nrm_ref, qa_ref, ka_ref, va_ref, stats_ref, carry_ref):
    rows = q_ref.shape[0]
    t = ATTN_STAT_TILE
    subs = [slice(r0, r0 + t) for r0 in range(0, rows, t)]

    @pl.when(pl.program_id(0) == 0)
    def _():
        carry_ref[...] = jnp.zeros_like(carry_ref)

    af = small_ref[...] + bfg_ref[...]
    logf = jnp.minimum(af, 0.0) - jnp.log1p(jnp.exp(-jnp.abs(af)))
    offset = carry_ref[...]
    cums = []
    for rs in subs:
        cums.append(_dot_exact_lhs(tril_ref[...], logf[rs, :]) + offset)
        offset = cums[-1][t - 1:t, :]
    carry_ref[...] = offset
    cum = jnp.concatenate(cums, axis=0)

    lane = lax.broadcasted_iota(jnp.int32, (rows, LANES), 1)
    hi, mid, lo = _split3(cum)
    fl = jnp.where(lane < FL_MID, hi,
                   jnp.where(lane < FL_LO, pltpu.roll(mid, FL_MID, 1),
                             jnp.where(lane < FL_ONE, pltpu.roll(lo, FL_LO, 1),
                                       jnp.where(lane == FL_ONE, 1.0, 0.0))))
    fl = _bf(fl)

    row = lax.broadcasted_iota(jnp.int32, (ST_ROWS, LANES), 0)
    put = lambda st, r, vec: jnp.where(row == r, jnp.broadcast_to(vec, (ST_ROWS, LANES)), st)
    stats = [put(put(jnp.zeros((ST_ROWS, LANES), F32), ST_FFIRST, c[0:1, :]),
                 ST_FLAST, c[t - 1:t, :]) for c in cums]
    for p in range(FOX_HEADS // 2):
        cols = slice(p * LANES, (p + 1) * LANES)
        for x_ref, w_ref, out_ref in ((q_ref, wq_ref, qa_ref), (k_ref, wk_ref, ka_ref),
                                      (v_ref, wv_ref, va_ref)):
            aug = jnp.dot(jnp.concatenate([x_ref[:, cols], fl], axis=1), w_ref[p],
                          preferred_element_type=F32)
            out_ref[2 * p] = _bf(aug[:, :LANES])
            out_ref[2 * p + 1] = _bf(aug[:, LANES:])
        xq, xk = q_ref[:, cols], k_ref[:, cols]
        n2 = jnp.dot(jnp.concatenate([xq * xq, xk * xk], axis=1), nrm_ref[...],
                     preferred_element_type=F32)
        diag = jnp.dot(xq * xk, nrm_ref[:LANES, :LANES], preferred_element_type=F32)
        for k, rs in enumerate(subs):
            top = jnp.max(n2[rs, :], axis=0, keepdims=True)
            stats[k] = put(put(stats[k], ST_QN2 + p, top[:, :LANES]), ST_KN2 + p, top[:, LANES:])
            stats[k] = put(stats[k], ST_DMIN + p, jnp.min(diag[rs, :], axis=0, keepdims=True))
    for k in range(len(subs)):
        stats_ref[k] = stats[k]


def _attn_prep(z, small, bfg, layer, tm):
    s = z.shape[0]
    t = ATTN_STAT_TILE
    tril = jnp.asarray(np.tril(np.ones((t, t), np.float32)), BF16)
    half = np.arange(2 * LANES) // FOX_HEAD_DIM
    nrm = jnp.asarray((half[:, None] == half[None, :]).astype(np.float32), BF16)
    places = _aug_placement()
    out_sds = jax.ShapeDtypeStruct((FOX_HEADS, s, LANES), BF16)
    out_spec = pl.BlockSpec((FOX_HEADS, tm, LANES), lambda i: (0, i, 0))
    const = lambda a: pl.BlockSpec(a.shape, lambda i: (0,) * a.ndim)
    return pl.pallas_call(
        _attn_prep_kernel,
        grid=(s // tm,),
        in_specs=[pl.BlockSpec((tm, WIDTH), lambda i: (i, COL_AQ // WIDTH)),
                  pl.BlockSpec((tm, WIDTH), lambda i: (i, COL_AK // WIDTH)),
                  pl.BlockSpec((tm, WIDTH), lambda i: (i, COL_AV // WIDTH)),
                  pl.BlockSpec((tm, LANES), lambda i: (i, 0)),
                  _layer_spec((1, LANES), layer, 1),
                  const(tril)] + [const(w) for w in places] + [const(nrm)],
        out_specs=[out_spec, out_spec, out_spec,
                   pl.BlockSpec((tm // t, ST_ROWS, LANES), lambda i: (i, 0, 0))],
        out_shape=[out_sds, out_sds, out_sds,
                   jax.ShapeDtypeStruct((s // t, ST_ROWS, LANES), F32)],
        scratch_shapes=[pltpu.VMEM((1, LANES), F32)],
        compiler_params=_cparams(("arbitrary",)),
        name="attn_prep",
    )(z, z, z, small, bfg, tril, *places, nrm)


def _attn_kernel(jlo_ref, q_ref, qn_ref, k_ref, v_ref, o_ref, s0_ref, s1_ref, s2_ref, s3_ref,
                 *, tq, tk):
    pair = pl.program_id(0)
    i = pl.program_id(1)
    lane = lax.broadcasted_iota(jnp.int32, (tq, LANES), 1)
    n_tile = tk // LANES

    def logits(q, hh, j, s_ref):
        start = pl.multiple_of(j * tk, tk)
        s_ref[...] = lax.dot_general(q, k_ref[hh, pl.ds(start, tk), :], (((1,), (1,)), ((), ())),
                                     preferred_element_type=F32)

    def softmax_pv(hh, j, s_ref, carry, masked):
        m, acc = carry
        start = pl.multiple_of(j * tk, tk)
        if masked:
            row = lax.broadcasted_iota(jnp.int32, (tq, tk), 0)
            col = lax.broadcasted_iota(jnp.int32, (tq, tk), 1)
            s_ref[...] = jnp.where(col <= row, s_ref[...], -jnp.inf)
        m_new = jnp.maximum(m, jnp.max(s_ref[...], axis=1, keepdims=True))
        p = jnp.exp(s_ref[...] - jnp.tile(m_new, (1, n_tile)))
        alpha = jnp.exp(m - m_new)
        acc = alpha * acc + jnp.dot(_bf(p), v_ref[hh, pl.ds(start, tk), :],
                                    preferred_element_type=F32)
        return m_new, acc

    bufs = ((s0_ref, s1_ref), (s2_ref, s3_ref))
    n_q = pl.num_programs(1)
    jlos = [jlo_ref[(2 * pair + hh) * n_q + i] for hh in range(2)]

    @pl.when(i == 0)
    def _():
        logits(q_ref[0], 0, jlos[0], bufs[0][0])

    accs = []
    for hh in range(2):
        q = q_ref[hh]
        jlo = jlos[hh]
        cur, nxt = bufs[hh]
        n_full = i - jlo

        def two_blocks(t, carry, q=q, hh=hh, jlo=jlo, cur=cur, nxt=nxt):
            j = jlo + 2 * t
            logits(q, hh, j + 1, nxt)
            carry = softmax_pv(hh, j, cur, carry, False)
            logits(q, hh, j + 2, cur)
            return softmax_pv(hh, j + 1, nxt, carry, False)

        def start_other_head(hh=hh):
            if hh == 0:
                logits(q_ref[1], 1, jlos[1], bufs[1][0])
            else:
                i_next = jnp.minimum(i + 1, n_q - 1)
                logits(qn_ref[0], 0, jlo_ref[2 * pair * n_q + i_next], bufs[0][0])

        def tail_odd(carry, q=q, hh=hh, cur=cur, nxt=nxt, start=start_other_head):
            logits(q, hh, i, nxt)
            start()
            carry = softmax_pv(hh, i - 1, cur, carry, False)
            return softmax_pv(hh, i, nxt, carry, True)

        def tail_even(carry, hh=hh, cur=cur, start=start_other_head):
            start()
            return softmax_pv(hh, i, cur, carry, True)

        def four_blocks(t, carry, two=two_blocks):
            return two(2 * t + 1, two(2 * t, carry))

        n_quads = n_full // 4
        carry = (jnp.full((tq, LANES), -jnp.inf, F32), jnp.zeros((tq, LANES), F32))
        carry = lax.fori_loop(0, n_quads, four_blocks, carry)
        carry = lax.fori_loop(2 * n_quads, n_full // 2, two_blocks, carry)
        _, acc = lax.cond(n_full % 2 == 1, tail_odd, tail_even, carry)
        accs.append(acc)
    o0 = accs[0] / pltpu.roll(accs[0], FOX_HEAD_DIM, 1)
    o1 = accs[1] / pltpu.roll(accs[1], FOX_HEAD_DIM, 1)
    o_ref[...] = jnp.where(lane < FOX_HEAD_DIM, o0, o1)


def _skip_starts(stats, tm, tq):
    r = tq // tm
    nq = stats.shape[0] // r
    scale = FOX_HEAD_DIM ** -0.5
    st = stats.reshape(nq, r, ST_ROWS, LANES)
    per_head = lambda row0: jnp.stack(
        [st[:, :, row0 + h // 2, FOX_HEAD_DIM * (h % 2)] for h in range(FOX_HEADS)], axis=-1)
    f_first = st[:, 0, ST_FFIRST, :FOX_HEADS].T
    f_last = st[:, r - 1, ST_FLAST, :FOX_HEADS].T
    qn = jnp.sqrt(jnp.max(per_head(ST_QN2), axis=1)).T
    kn = jnp.sqrt(jnp.max(per_head(ST_KN2), axis=(0, 1)))[:, None]
    bound = NORM_SLACK * scale * qn * kn
    dmin = scale * jnp.min(per_head(ST_DMIN), axis=1).T - (NORM_SLACK - 1.0) * bound
    thresh = f_first + bound - dmin + SKIP_LOG_MARGIN
    need = f_last[:, None, :] <= thresh[:, :, None]
    idx = jnp.arange(nq)
    first = jnp.min(jnp.where(need, idx[None, None, :], nq), axis=-1)
    return jnp.minimum(first, idx[None, :]).astype(jnp.int32).reshape(-1)


def _attention(jlo, qa, ka, va, tq):
    s = qa.shape[1]
    tk = tq
    kv_spec = pl.BlockSpec((2, s, LANES), lambda p, i, jl: (p, 0, 0))
    return pl.pallas_call(
        functools.partial(_attn_kernel, tq=tq, tk=tk),
        grid_spec=pltpu.PrefetchScalarGridSpec(
            num_scalar_prefetch=1,
            grid=(FOX_HEADS // 2, s // tq),
            in_specs=[pl.BlockSpec((2, tq, LANES), lambda p, i, jl: (p, i, 0)),
                      pl.BlockSpec((2, tq, LANES),
                                   lambda p, i, jl: (p, jnp.minimum(i + 1, s // tq - 1), 0)),
                      kv_spec, kv_spec],
            out_specs=pl.BlockSpec((tq, LANES), lambda p, i, jl: (i, p)),
            scratch_shapes=[pltpu.VMEM((tq, tk), F32)] * 4),
        out_shape=jax.ShapeDtypeStruct((s, WIDTH), F32),
        compiler_params=_cparams(("parallel", "arbitrary")),
        name="fox_attention",
    )(jlo, qa, qa, ka, va)


def _chunk_cumsum_matrix():
    r = np.arange(GDN_TILE)
    same = (r[:, None] // CHUNK) == (r[None, :] // CHUNK)
    return jnp.asarray((same & (r[None, :] <= r[:, None])).astype(np.float32), BF16)


def _gdn_prep_kernel(xq_ref, xk_ref, xv_ref, hq_ref, hk_ref, hv_ref, small_ref, cw_ref, aneg_ref,
                     dtb_ref, kl_ref, qd_ref, kd_ref, u_ref, w_ref, aqk_ref, egl_ref, xx_ref):
    t = GDN_TILE
    hd = GDN_HEAD_DIM
    pad = BF16_SUBLANES
    rows_in = xq_ref.shape[0]

    for n, (x_ref, halo_ref) in enumerate(((xq_ref, hq_ref), (xk_ref, hk_ref), (xv_ref, hv_ref))):
        cs = slice(n * WIDTH, (n + 1) * WIDTH)
        halo = halo_ref[...].astype(F32)
        xx_ref[0:pad, cs] = jnp.where(pl.program_id(0) == 0, jnp.zeros_like(halo), halo)
        xx_ref[pad:pad + rows_in, cs] = x_ref[...].astype(F32)

    row = lax.broadcasted_iota(jnp.int32, (t, t), 0)
    col = lax.broadcasted_iota(jnp.int32, (t, t), 1)
    same = (row // CHUNK) == (col // CHUNK)
    incl = jnp.logical_and(same, col <= row)
    strict = jnp.logical_and(same, col < row)
    eye = (row == col).astype(F32)
    row8 = lax.broadcasted_iota(jnp.int32, (8, WIDTH), 0)

    invs, bps, rhss, dests = [], [], [], []
    for tile in range(rows_in // t):
        r0 = tile * t
        rs = slice(r0, r0 + t)
        conv = cw_ref[CONV_K - 1:CONV_K, :] * xx_ref[pad + r0:pad + r0 + t, :]
        for j in range(CONV_K - 1):
            off = pad + r0 - (CONV_K - 1) + j
            conv = conv + cw_ref[j:j + 1, :] * xx_ref[off:off + t, :]
        qkv = _silu(conv)

        small = small_ref[rs, :]
        bcast = lambda c0: jnp.concatenate(
            [jnp.broadcast_to(small[:, c0 + h:c0 + h + 1], (t, hd)) for h in range(GDN_HEADS)],
            axis=1)
        sp_in = bcast(SMALL_BA) + dtb_ref[...]
        softplus = jnp.maximum(sp_in, 0.0) + jnp.log1p(jnp.exp(-jnp.abs(sp_in)))
        g = aneg_ref[...] * softplus
        beta = _sigmoid(bcast(SMALL_BB))
        gcum = _dot_exact_lhs(kl_ref[...], g)
        glast = jnp.concatenate(
            [jnp.broadcast_to(gcum[(c + 1) * CHUNK - 1:(c + 1) * CHUNK, :], (CHUNK, WIDTH))
             for c in range(CHUNKS_PER_TILE)], axis=0)
        eg = jnp.exp(gcum)
        egl = jnp.exp(glast)
        egd = jnp.exp(glast - gcum)
        egl_rows = jnp.zeros((8, WIDTH), F32)
        for c in range(CHUNKS_PER_TILE):
            egl_rows = jnp.where(row8 == c, egl[c * CHUNK:c * CHUNK + 8, :], egl_rows)
        egl_ref[tile * 8:(tile + 1) * 8, :] = egl_rows

        for h in range(GDN_HEADS):
            cs = slice(h * hd, (h + 1) * hd)
            qh = qkv[:, h * hd:(h + 1) * hd]
            kh = qkv[:, WIDTH + h * hd:WIDTH + (h + 1) * hd]
            vh = qkv[:, 2 * WIDTH + h * hd:2 * WIDTH + (h + 1) * hd]
            qh = qh * lax.rsqrt(jnp.sum(qh * qh, axis=-1, keepdims=True) + EPS) * (hd ** -0.5)
            kh = kh * lax.rsqrt(jnp.sum(kh * kh, axis=-1, keepdims=True) + EPS)
            bh = beta[:, cs]
            gc = gcum[:, cs]
            dmat = jnp.concatenate([gc, gc], axis=1) - gc.T[0:1, :]
            gamma = jnp.exp(jnp.where(incl, dmat, -jnp.inf))
            kb = kh * bh
            kk_qk = _dot_nt(jnp.concatenate([kb, qh], axis=0), kh)
            a = jnp.where(strict, kk_qk[:t] * gamma, 0.0)
            invs.append(eye - a)
            bps.append(_bf(-a))
            rhss.append(_bf(jnp.concatenate([vh * bh, kb * eg[:, cs]], axis=1)))
            aqk_ref[h, rs, :] = _bf(kk_qk[t:] * gamma)
            qd_ref[rs, cs] = _bf(qh * eg[:, cs])
            kd_ref[rs, cs] = _bf(kh * egd[:, cs])
            dests.append((rs, cs))

    n_sq = 5
    bps = [_bf(jnp.dot(b, b, preferred_element_type=F32)) for b in bps]
    for k in range(n_sq):
        last = k == n_sq - 1
        prods = [jnp.dot(_bf(inv) if last else jnp.concatenate([_bf(inv), b], axis=0), b,
                         preferred_element_type=F32) for inv, b in zip(invs, bps)]
        invs = [inv + p[:t] for inv, p in zip(invs, prods)]
        if not last:
            bps = [_bf(p[t:]) for p in prods]
    for inv, rhs, (rs, cs) in zip(invs, rhss, dests):
        sol = jnp.dot(_bf(inv), rhs, preferred_element_type=F32)
        u_ref[rs, cs] = sol[:, :hd]
        w_ref[rs, cs] = _bf(sol[:, hd:])


def _gdn_prep(z, small, conv_w, aneg, dtb, layer, rows):
    s = z.shape[0]
    t = GDN_TILE
    pad = BF16_SUBLANES
    tiles = rows // t
    kl = _chunk_cumsum_matrix()
    row_spec = pl.BlockSpec((rows, WIDTH), lambda i: (i, 0))
    halo_spec = lambda col: pl.BlockSpec(
        (pad, WIDTH), lambda i: (jnp.maximum(i * (rows // pad) - 1, 0), col // WIDTH))
    return pl.pallas_call(
        _gdn_prep_kernel,
        grid=(s // rows,),
        in_specs=[pl.BlockSpec((rows, WIDTH), lambda i: (i, COL_BQ // WIDTH)),
                  pl.BlockSpec((rows, WIDTH), lambda i: (i, COL_BK // WIDTH)),
                  pl.BlockSpec((rows, WIDTH), lambda i: (i, COL_BV // WIDTH)),
                  halo_spec(COL_BQ), halo_spec(COL_BK), halo_spec(COL_BV),
                  pl.BlockSpec((rows, LANES), lambda i: (i, 0)),
                  _layer_spec((CONV_K, 3 * WIDTH), layer, 1),
                  _layer_spec((1, WIDTH), layer, 1),
                  _layer_spec((1, WIDTH), layer, 1),
                  pl.BlockSpec((t, t), lambda i: (0, 0))],
        out_specs=[row_spec, row_spec, row_spec, row_spec,
                   pl.BlockSpec((GDN_HEADS, rows, t), lambda i: (0, i, 0)),
                   pl.BlockSpec((8 * tiles, WIDTH), lambda i: (i, 0))],
        out_shape=[jax.ShapeDtypeStruct((s, WIDTH), BF16),
                   jax.ShapeDtypeStruct((s, WIDTH), BF16),
                   jax.ShapeDtypeStruct((s, WIDTH), F32),
                   jax.ShapeDtypeStruct((s, WIDTH), BF16),
                   jax.ShapeDtypeStruct((GDN_HEADS, s, t), BF16),
                   jax.ShapeDtypeStruct((s // t * 8, WIDTH), F32)],
        scratch_shapes=[pltpu.VMEM((rows + pad, 3 * WIDTH), F32)],
        compiler_params=_cparams(("parallel",)),
        name="gdn_prep",
    )(z, z, z, z, z, z, small, conv_w, aneg, dtb, kl)


def _gdn_scan_kernel(qd_ref, kd_ref, u_ref, w_ref, aqk_ref, egl_ref, o_ref, state_ref, vn_ref):
    hd = GDN_HEAD_DIM

    @pl.when(pl.program_id(0) == 0)
    def _():
        state_ref[...] = jnp.zeros_like(state_ref)
        vn_ref[...] = jnp.zeros_like(vn_ref)

    heads = range(GDN_HEADS)
    col = lambda h: slice(h * hd, (h + 1) * hd)
    for c in range(CHUNKS_PER_TILE):
        rs = slice(c * CHUNK, (c + 1) * CHUNK)
        sts = [state_ref[h] for h in heads]
        rr = [jnp.dot(jnp.concatenate([w_ref[rs, col(h)], qd_ref[rs, col(h)]], axis=0),
                      _bf(sts[h]), preferred_element_type=F32) for h in heads]
        vns = [_bf(u_ref[rs, col(h)] - rr[h][:CHUNK]) for h in heads]
        for h in heads:
            vn_ref[h, rs, :] = vns[h]
        for h in heads:
            upd = lax.dot_general(kd_ref[rs, col(h)], vns[h], (((0,), (0,)), ((), ())),
                                  preferred_element_type=F32)
            state_ref[h] = sts[h] * egl_ref[c:c + 1, col(h)] + upd
        for h in heads:
            o_ref[rs, col(h)] = rr[h][CHUNK:] + jnp.dot(aqk_ref[h, rs, :], vn_ref[h],
                                                        preferred_element_type=F32)


def _gdn_scan(qd, kd, u, w, aqk, egl):
    s = qd.shape[0]
    t = GDN_TILE
    row_spec = pl.BlockSpec((t, WIDTH), lambda i: (i, 0))
    return pl.pallas_call(
        _gdn_scan_kernel,
        grid=(s // t,),
        in_specs=[row_spec, row_spec, row_spec, row_spec,
                  pl.BlockSpec((GDN_HEADS, t, t), lambda i: (0, i, 0)),
                  pl.BlockSpec((8, WIDTH), lambda i: (i, 0))],
        out_specs=row_spec,
        out_shape=jax.ShapeDtypeStruct((s, WIDTH), F32),
        scratch_shapes=[pltpu.VMEM((GDN_HEADS, GDN_HEAD_DIM, GDN_HEAD_DIM), F32),
                        pltpu.VMEM((GDN_HEADS, t, GDN_HEAD_DIM), BF16)],
        compiler_params=_cparams(("arbitrary",)),
        name="gdn_scan",
    )(qd, kd, u, w, aqk, egl)


def _merge_kernel(x_ref, oa_ref, ob_ref, mq_ref, az_ref, bz_ref, mz_ref, g0_ref, g1_ref, g2_ref,
                  mk_ref, mv_ref, gng_ref, bm_ref, wb_ref, wo_ref, fg_ref, o_ref, *, final_norm):
    hd = GDN_HEAD_DIM
    groups = [slice(r0, r0 + MERGE_SUB) for r0 in range(0, x_ref.shape[0], MERGE_SUB)]
    g_refs = (g0_ref, g1_ref, g2_ref)

    def branch_a(rs):
        return oa_ref[rs, :] * _silu(az_ref[rs, :].astype(F32))

    def branch_b(rs):
        normed = []
        for h in range(GDN_HEADS):
            oh = ob_ref[rs, h * hd:(h + 1) * hd]
            normed.append(oh * lax.rsqrt(jnp.mean(oh * oh, axis=-1, keepdims=True) + EPS))
        return jnp.concatenate(normed, axis=1) * gng_ref[...] * _silu(bz_ref[rs, :].astype(F32))

    def branch_m(rs):
        om = []
        for h in range(MEM_HEADS):
            cs = slice(h * MEM_HEAD_DIM, (h + 1) * MEM_HEAD_DIM)
            sc = _dot_nt(mq_ref[rs, cs], mk_ref[:, cs])
            p = jnp.exp2((sc - jnp.max(sc, axis=-1, keepdims=True))
                         * (-NEG_LOG2_E * MEM_HEAD_DIM ** -0.5))
            om.append(jnp.dot(_bf(p), mv_ref[:, cs], preferred_element_type=F32)
                      / jnp.sum(p, axis=-1, keepdims=True))
        return jnp.concatenate(om, axis=1) * _silu(mz_ref[rs, :].astype(F32))

    merged = [None] * len(groups)
    for n, branch in enumerate((branch_a, branch_b, branch_m)):
        for k, rs in enumerate(groups):
            proj = jnp.dot(_bf(branch(rs)), wb_ref[n], preferred_element_type=F32)
            term = _sigmoid(g_refs[n][rs, :].astype(F32) + bm_ref[n:n + 1, :]) * proj
            merged[k] = term if merged[k] is None else merged[k] + term
    for k, rs in enumerate(groups):
        out = x_ref[rs, :] + jnp.dot(_bf(merged[k]), wo_ref[...], preferred_element_type=F32)
        if final_norm:
            out = out * lax.rsqrt(jnp.mean(out * out, axis=-1, keepdims=True) + EPS) * fg_ref[...]
        o_ref[rs, :] = out


def _merge(x, o_a, o_b, z, memkv, gng, bm, wb, wo, fg_row, layer, tm, final_norm):
    s = x.shape[0]
    zc = lambda col, width: pl.BlockSpec((tm, width), lambda i: (i, col // width))
    n_mem = memkv.shape[0]
    return pl.pallas_call(
        functools.partial(_merge_kernel, final_norm=final_norm),
        grid=(s // tm,),
        in_specs=[pl.BlockSpec((tm, D_MODEL), lambda i: (i, 0)),
                  pl.BlockSpec((tm, WIDTH), lambda i: (i, 0)),
                  pl.BlockSpec((tm, WIDTH), lambda i: (i, 0)),
                  zc(COL_MQ, WIDTH),
                  zc(COL_AZ, WIDTH), zc(COL_BZ, WIDTH), zc(COL_MZ, WIDTH),
                  zc(COL_GATES, D_MODEL), zc(COL_GATES + D_MODEL, D_MODEL),
                  zc(COL_GATES + 2 * D_MODEL, D_MODEL),
                  pl.BlockSpec((n_mem, WIDTH), lambda i: (0, 0)),
                  pl.BlockSpec((n_mem, WIDTH), lambda i: (0, 1)),
                  _layer_spec((1, WIDTH), layer, 1),
                  _layer_spec((N_BRANCH, D_MODEL), layer, 1),
                  _layer_spec((N_BRANCH, WIDTH, D_MODEL), layer, 1),
                  _layer_spec((D_MODEL, D_MODEL), layer, 1),
                  pl.BlockSpec((1, D_MODEL), lambda i: (0, 0))],
        out_specs=pl.BlockSpec((tm, D_MODEL), lambda i: (i, 0)),
        out_shape=jax.ShapeDtypeStruct((s, D_MODEL), F32),
        compiler_params=_cparams(("parallel",)),
        name="merge",
    )(x, o_a, o_b, z, z, z, z, z, z, z, memkv, memkv, gng, bm, wb, wo, fg_row)


def _main_row_start(c):
    col = c * WIDTH
    past_af = (col >= COL_AZ).astype(jnp.int32)
    past_ba = (col >= COL_BZ).astype(jnp.int32)
    return (c * (WIDTH // F32_SUBLANES) + past_af * (FOX_HEADS // F32_SUBLANES)
            + past_ba * (2 * GDN_HEADS // F32_SUBLANES))


NARROW_ROWS = ((SRC_AF, FOX_HEADS), (SRC_BA, 2 * GDN_HEADS))


def kernel(x, mem, norm_g, w_in, b_fg, b_merge, conv_w, a_log, dt_bias, gdn_norm_g, mem_norm_g,
           w_mem_kv, w_branch, w_out, final_norm_g):
    assert x.shape[0] == 1 and mem.shape[0] == 1
    depth = w_in.shape[0]
    s = x.shape[1]
    rows = lambda v: v.astype(F32).reshape(depth, 1, -1)
    assert w_in.shape[2] == N_SRC
    w_in_t = jnp.swapaxes(w_in, 1, 2)
    w_mem_t = jnp.swapaxes(w_mem_kv, 1, 2)
    w_branch_b, w_out_b = _bf(w_branch), _bf(w_out)
    norm_g3, mem_norm_g3 = rows(norm_g), rows(mem_norm_g)
    bfg3 = rows(jnp.pad(b_fg, ((0, 0), (0, LANES - FOX_HEADS))))
    aneg3 = rows(jnp.repeat(-jnp.exp(a_log.astype(F32)), GDN_HEAD_DIM, axis=1))
    dtb3 = rows(jnp.repeat(dt_bias, GDN_HEAD_DIM, axis=1))
    gng3 = rows(jnp.tile(gdn_norm_g, (1, GDN_HEADS)))
    bm3 = b_merge.reshape(depth, N_BRANCH, D_MODEL)
    fg_row = final_norm_g.reshape(1, D_MODEL)
    tq = min(s, ATTN_TQ)

    h = x[0]
    for l in range(depth):
        z, small = _norm_matmul(h, norm_g3, w_in_t, l, tm=min(s, NORM_TM), tn=NORM_TN,
                                n_out=N_MAIN, sub=WIDTH, row_start=_main_row_start,
                                narrow_rows=NARROW_ROWS)
        memkv, = _norm_matmul(mem[0], mem_norm_g3, w_mem_t, l, tm=mem.shape[1], tn=2 * WIDTH)
        qa, ka, va, stats = _attn_prep(z, small, bfg3, l, tm=min(s, ATTN_PREP_ROWS))
        o_a = _attention(_skip_starts(stats, ATTN_STAT_TILE, tq), qa, ka, va, tq=tq)
        qd, kd, u, w, aqk, egl = _gdn_prep(z, small, conv_w, aneg3, dtb3, l, rows=GDN_PREP_ROWS)
        o_b = _gdn_scan(qd, kd, u, w, aqk, egl)
        h = _merge(h, o_a, o_b, z, memkv, gng3, bm3, w_branch_b, w_out_b, fg_row, l,
                   tm=MERGE_TM, final_norm=(l == depth - 1))
    return h[None]
```
